```python
import jax
import jax.numpy as jnp
from jax import lax
import numpy as np

D_MODEL = 1024
BATCH = 4
SEQ = 8192
DEPTH = 4

GRID_W = 64
CTX_LEN = 256
N_MIXERS = 3
EPS = 1e-6

GLA_HEADS = 4
GLA_DK = D_MODEL // 2
GLA_DV = D_MODEL
GLA_HK = GLA_DK // GLA_HEADS
GLA_HV = GLA_DV // GLA_HEADS
GLA_GATE_RANK = 16
GLA_TAU = 16.0
GLA_CHUNK = 64

ATT_HEAD_DIM = 64
ATT_Q_HEADS = D_MODEL // ATT_HEAD_DIM
ATT_KV_HEADS = 4
ATT_GROUP = ATT_Q_HEADS // ATT_KV_HEADS
ATT_Q_WIDTH = ATT_Q_HEADS * ATT_HEAD_DIM
ATT_KV_WIDTH = ATT_KV_HEADS * ATT_HEAD_DIM
ATT_SCALE = ATT_HEAD_DIM ** -0.5
WINDOW = 128
ATT_BLOCK = 128
ROPE_PAIRS = ATT_HEAD_DIM // 4
ROPE_BASE = 10000.0

RNN_WIDTH = 1280
RNN_HEADS = 10
RNN_HD = RNN_WIDTH // RNN_HEADS
CONV_W = 4
CONV_LEFT = 2
LRU_C = 8.0

kernel_name = 'hybrid_gla_swa_rglru_prefix_dit'


def rmsnorm(x, g):
    xf = x.astype(jnp.float32)
    y = xf * lax.rsqrt(jnp.mean(xf * xf, axis=-1, keepdims=True) + EPS)
    return (y * g.astype(jnp.float32)).astype(x.dtype)


def split_heads(a, n):
    b, t, w = a.shape
    return a.reshape(b, t, n, w // n).transpose(0, 2, 1, 3)


def merge_heads(a):
    b, n, t, d = a.shape
    return a.transpose(0, 2, 1, 3).reshape(b, t, n * d)


def flip_t(a):
    return jnp.flip(a, axis=2)


def gla_chunked(q, k, v, log_a, s0):
    b, h, t, _ = q.shape
    n = t // GLA_CHUNK

    def to_chunks(a):
        return jnp.moveaxis(a.reshape(b, h, n, GLA_CHUNK, a.shape[-1]), 2, 0)

    cum = jnp.cumsum(to_chunks(log_a), axis=-2)
    lower = jnp.tril(jnp.ones((GLA_CHUNK, GLA_CHUNK), dtype=bool))[:, :, None]

    def step(s, inp):
        q_n, k_n, v_n, b_n = inp
        rel = b_n[:, :, :, None, :] - b_n[:, :, None, :, :]
        decay = jnp.exp(jnp.where(lower, rel, -jnp.inf))
        scores = jnp.einsum('bhtd,bhsd,bhtsd->bhts', q_n, k_n, decay)
        o = scores @ v_n + (q_n * jnp.exp(b_n)) @ s
        b_last = b_n[:, :, -1:, :]
        s = jnp.exp(b_last)[:, :, 0, :, None] * s + jnp.swapaxes(k_n * jnp.exp(b_last - b_n), -1, -2) @ v_n
        return s, o

    s_fin, o = lax.scan(step, s0, (to_chunks(q), to_chunks(k), to_chunks(v), cum))
    return jnp.moveaxis(o, 0, 2).reshape(b, h, t, v.shape[-1]), s_fin


def gla_final_state(k, v, log_a):
    cum = jnp.cumsum(log_a, axis=2)
    return jnp.swapaxes(k * jnp.exp(cum[:, :, -1:, :] - cum), -1, -2) @ v


def gla_bidir(q, k, v, la_f, la_b, s0_f, s0_b):
    o_f, s_f = gla_chunked(q, k, v, la_f, s0_f)
    o_b, s_b = gla_chunked(flip_t(q), flip_t(k), flip_t(v), flip_t(la_b), s0_b)
    return o_f + flip_t(o_b), s_f, s_b


def mixer_gla(h_lat, h_ctx, w_in, w_g1, w_g2, b_g, g_head, w_out, ctx_out):
    def project(h):
        q, k, v, z = jnp.split(h @ w_in, [GLA_DK, 2 * GLA_DK, 2 * GLA_DK + GLA_DV], axis=-1)
        la_f, la_b = [split_heads(jax.nn.log_sigmoid(((h @ w_g1[d]) @ w_g2[d] + b_g[d]).astype(jnp.float32)) / GLA_TAU, GLA_HEADS)
                      for d in range(2)]
        q = split_heads(q, GLA_HEADS).astype(jnp.float32) * GLA_HK ** -0.5
        k = split_heads(k, GLA_HEADS).astype(jnp.float32)
        v = split_heads(v, GLA_HEADS).astype(jnp.float32)
        return q, k, v, z, la_f, la_b

    def finish(o, z):
        o = merge_heads(rmsnorm(o, g_head[:, None, :]))
        return (o.astype(z.dtype) * jax.nn.silu(z)) @ w_out

    q_c, k_c, v_c, z_c, la_cf, la_cb = project(h_ctx)
    q_l, k_l, v_l, z_l, la_lf, la_lb = project(h_lat)
    if ctx_out:
        zero = jnp.zeros((h_ctx.shape[0], GLA_HEADS, GLA_HK, GLA_HV), jnp.float32)
        o_c, s_f, s_b = gla_bidir(q_c, k_c, v_c, la_cf, la_cb, zero, zero)
        y_ctx = finish(o_c, z_c)
    else:
        s_f = gla_final_state(k_c, v_c, la_cf)
        s_b = gla_final_state(flip_t(k_c), flip_t(v_c), flip_t(la_cb))
        y_ctx = None
    o_l, _, _ = gla_bidir(q_l, k_l, v_l, la_lf, la_lb, s_f, s_b)
    return finish(o_l, z_l), y_ctx


def rope_1d(x, cos, sin):
    x1, x2 = jnp.split(x, 2, axis=-1)
    return jnp.concatenate([x1 * cos - x2 * sin, x2 * cos + x1 * sin], axis=-1)


def rope_axial(x, rope):
    cos_r, sin_r, cos_c, sin_c = rope
    half = ATT_HEAD_DIM // 2
    return jnp.concatenate([rope_1d(x[..., :half], cos_r, sin_r), rope_1d(x[..., half:], cos_c, sin_c)], axis=-1)


def attn_project(h, w_in):
    b, t, _ = h.shape
    q, k, v, z = jnp.split(h @ w_in, [ATT_Q_WIDTH, ATT_Q_WIDTH + ATT_KV_WIDTH, ATT_Q_WIDTH + 2 * ATT_KV_WIDTH], axis=-1)
    q = q.reshape(b, t, ATT_KV_HEADS, ATT_GROUP, ATT_HEAD_DIM).transpose(0, 2, 3, 1, 4)
    k = k.reshape(b, t, ATT_KV_HEADS, ATT_HEAD_DIM).transpose(0, 2, 1, 3)
    v = v.reshape(b, t, ATT_KV_HEADS, ATT_HEAD_DIM).transpose(0, 2, 1, 3)
    return q, k, v, z


def sink_column(sink, lead_shape):
    return jnp.broadcast_to(sink.astype(jnp.float32)[None, :, :, None, None], lead_shape + (1,))


def context_attention(q, k, v, sink):
    s = jnp.einsum('bkgqd,bksd->bkgqs', q, k).astype(jnp.float32) * ATT_SCALE
    p = jax.nn.softmax(jnp.concatenate([s, sink_column(sink, s.shape[:-1])], axis=-1), axis=-1)[..., :-1]
    return jnp.einsum('bkgqs,bksd->bkgqd', p.astype(v.dtype), v)


def banded_attention(q, k, v, k_ctx, v_ctx, sink):
    b, hk, g, t, dh = q.shape
    n = t // ATT_BLOCK
    span = 3 * ATT_BLOCK
    pad = ((0, 0), (0, 0), (ATT_BLOCK, ATT_BLOCK), (0, 0))
    kp, vp = jnp.pad(k, pad), jnp.pad(v, pad)
    q_blocks = jnp.moveaxis(q.reshape(b, hk, g, n, ATT_BLOCK, dh), 3, 0)
    offs = jnp.arange(span) - ATT_BLOCK
    in_window = jnp.abs(offs[None, :] - jnp.arange(ATT_BLOCK)[:, None]) <= WINDOW

    def block(args):
        q_n, idx = args
        start = idx * ATT_BLOCK
        k_n = lax.dynamic_slice_in_dim(kp, start, span, axis=2)
        v_n = lax.dynamic_slice_in_dim(vp, start, span, axis=2)
        kpos = start + offs
        valid = in_window & ((kpos >= 0) & (kpos < t))[None, :]
        s_loc = jnp.einsum('bkgqd,bksd->bkgqs', q_n, k_n).astype(jnp.float32) * ATT_SCALE
        s_loc = jnp.where(valid, s_loc, -jnp.inf)
        s_ctx = jnp.einsum('bkgqd,bksd->bkgqs', q_n, k_ctx).astype(jnp.float32) * ATT_SCALE
        logits = jnp.concatenate([s_loc, s_ctx, sink_column(sink, s_loc.shape[:-1])], axis=-1)
        p = jax.nn.softmax(logits, axis=-1).astype(v.dtype)
        return (jnp.einsum('bkgqs,bksd->bkgqd', p[..., :span], v_n)
                + jnp.einsum('bkgqs,bksd->bkgqd', p[..., span:-1], v_ctx))

    o = lax.map(block, (q_blocks, jnp.arange(n)))
    return jnp.moveaxis(o, 0, 3).reshape(b, hk, g, t, dh)


def mixer_swa(h_lat, h_ctx, w_in, sink, w_out, rope, ctx_out):
    sink = sink.reshape(ATT_KV_HEADS, ATT_GROUP)
    q_c, k_c, v_c, z_c = attn_project(h_ctx, w_in)
    q_l, k_l, v_l, z_l = attn_project(h_lat, w_in)
    q_l, k_l = rope_axial(q_l, rope), rope_axial(k_l, rope)

    def finish(o, z):
        b, hk, g, t, d = o.shape
        o = o.transpose(0, 3, 1, 2, 4).reshape(b, t, hk * g * d)
        return (o * jax.nn.silu(z)) @ w_out

    y_lat = finish(banded_attention(q_l, k_l, v_l, k_c, v_c, sink), z_l)
    y_ctx = finish(context_attention(q_c, k_c, v_c, sink), z_c) if ctx_out else None
    return y_lat, y_ctx


def depthwise_conv(u, w, bias):
    t = u.shape[1]
    up = jnp.pad(u, ((0, 0), (CONV_LEFT, CONV_W - 1 - CONV_LEFT), (0, 0)))
    out = up[:, 0:t] * w[0]
    for j in range(1, CONV_W):
        out = out + up[:, j:j + t] * w[j]
    return out + bias


def block_diag(x, w, bias):
    xh = x.reshape(x.shape[:-1] + (RNN_HEADS, RNN_HD))
    return jnp.einsum('bthi,hij->bthj', xh, w).reshape(x.shape) + bias


def rglru_gates(u, w_a, b_a, w_x, b_x, lam):
    r = jax.nn.sigmoid(block_diag(u, w_a, b_a).astype(jnp.float32))
    i = jax.nn.sigmoid(block_diag(u, w_x, b_x).astype(jnp.float32))
    log_a = -LRU_C * r * jax.nn.softplus(-lam.astype(jnp.float32))
    x = jnp.sqrt(-jnp.expm1(2.0 * log_a)) * i * u.astype(jnp.float32)
    return jnp.exp(log_a), x


def scan_combine(left, right):
    a_l, h_l = left
    a_r, h_r = right
    return a_l * a_r, a_r * h_l + h_r


def linear_scan(a, x, h0, reverse):
    edge = -1 if reverse else 0
    x = x.at[:, edge].add(a[:, edge] * h0)
    return lax.associative_scan(scan_combine, (a, x), reverse=reverse, axis=1)[1]


def mixer_rglru(h_lat, h_ctx, w_in, conv_w, conv_b, w_ra, b_ra, w_ri, b_ri, lam, w_out, ctx_out):
    def branch(h):
        u, z = jnp.split(h @ w_in, 2, axis=-1)
        return depthwise_conv(u, conv_w, conv_b), z

    u_c, z_c = branch(h_ctx)
    u_l, z_l = branch(h_lat)
    h0 = jnp.zeros((h_ctx.shape[0], RNN_WIDTH), jnp.float32)
    hs_c, hs_l = [], []
    for d, reverse in ((0, False), (1, True)):
        a_c, x_c = rglru_gates(u_c, w_ra[d], b_ra[d], w_ri[d], b_ri[d], lam[d])
        h_c = linear_scan(a_c, x_c, h0, reverse)
        a_l, x_l = rglru_gates(u_l, w_ra[d], b_ra[d], w_ri[d], b_ri[d], lam[d])
        hs_l.append(linear_scan(a_l, x_l, h_c[:, 0 if reverse else -1], reverse))
        hs_c.append(h_c)

    def finish(hs, z):
        return ((hs[0] + hs[1]).astype(z.dtype) * jax.nn.silu(z)) @ w_out

    return finish(hs_l, z_l), (finish(hs_c, z_c) if ctx_out else None)


def setup_inputs(seed: int = 0) -> dict:
    key = jax.random.key(seed)
    keys = iter(jax.random.split(key, 32))

    def normal(shape, scale):
        return jax.random.normal(next(keys), shape, jnp.float32) * scale

    d = D_MODEL
    n_a = len(range(0, DEPTH, N_MIXERS))
    n_b = len(range(1, DEPTH, N_MIXERS))
    n_c = len(range(2, DEPTH, N_MIXERS))
    a_pow_c = jax.random.uniform(next(keys), (n_c, 2, RNN_WIDTH), jnp.float32, 0.9, 0.999)
    s = a_pow_c ** (1.0 / LRU_C)
    c_lam = jnp.log(s) - jnp.log1p(-s)
    return {
        'x': normal((BATCH, SEQ, d), 1.0),
        'c': normal((BATCH, d), 1.0),
        'ctx': normal((BATCH, CTX_LEN, d), 1.0),
        'c_ctx': normal((d,), 1.0),
        'w_mod': normal((DEPTH, d, 3 * d), 0.5 * d ** -0.5),
        'b_mod': normal((DEPTH, 3 * d), 0.02),
        'g_pre': 1.0 + normal((DEPTH, d), 0.05),
        'g_post': 1.0 + normal((DEPTH, d), 0.05),
        'a_w_in': normal((n_a, d, 2 * GLA_DK + 2 * GLA_DV), d ** -0.5),
        'a_w_g1': normal((n_a, 2, d, GLA_GATE_RANK), d ** -0.5),
        'a_w_g2': normal((n_a, 2, GLA_GATE_RANK, GLA_DK), GLA_GATE_RANK ** -0.5),
        'a_b_g': normal((n_a, 2, GLA_DK), 0.1),
        'a_g_head': 1.0 + normal((n_a, GLA_HEADS, GLA_HV), 0.05),
        'a_w_out': normal((n_a, GLA_DV, d), GLA_DV ** -0.5),
        'b_w_in': normal((n_b, d, 2 * ATT_Q_WIDTH + 2 * ATT_KV_WIDTH), d ** -0.5),
        'b_sink': normal((n_b, ATT_Q_HEADS), 0.5),
        'b_w_out': normal((n_b, ATT_Q_WIDTH, d), ATT_Q_WIDTH ** -0.5),
        'c_w_in': normal((n_c, d, 2 * RNN_WIDTH), d ** -0.5),
        'c_conv_w': normal((n_c, CONV_W, RNN_WIDTH), CONV_W ** -0.5),
        'c_conv_b': normal((n_c, RNN_WIDTH), 0.02),
        'c_w_ra': normal((n_c, 2, RNN_HEADS, RNN_HD, RNN_HD), RNN_HD ** -0.5),
        'c_b_ra': normal((n_c, 2, RNN_WIDTH), 0.02),
        'c_w_ri': normal((n_c, 2, RNN_HEADS, RNN_HD, RNN_HD), RNN_HD ** -0.5),
        'c_b_ri': normal((n_c, 2, RNN_WIDTH), 0.02),
        'c_lam': c_lam,
        'c_w_out': normal((n_c, RNN_WIDTH, d), RNN_WIDTH ** -0.5),
    }


def reference(x, c, ctx, c_ctx, w_mod, b_mod, g_pre, g_post,
              a_w_in, a_w_g1, a_w_g2, a_b_g, a_g_head, a_w_out,
              b_w_in, b_sink, b_w_out,
              c_w_in, c_conv_w, c_conv_b, c_w_ra, c_b_ra, c_w_ri, c_b_ri, c_lam, c_w_out):
    t = x.shape[1]
    rows = t // GRID_W
    row = jnp.repeat(jnp.arange(rows, dtype=jnp.float32), GRID_W)
    col = jnp.tile(jnp.arange(GRID_W, dtype=jnp.float32), rows)
    freqs = ROPE_BASE ** (-jnp.arange(ROPE_PAIRS, dtype=jnp.float32) / ROPE_PAIRS)
    ang_r = row[:, None] * freqs
    ang_c = col[:, None] * freqs
    rope = (jnp.cos(ang_r).astype(x.dtype), jnp.sin(ang_r).astype(x.dtype),
            jnp.cos(ang_c).astype(x.dtype), jnp.sin(ang_c).astype(x.dtype))

    s_lat = jax.nn.silu(c)
    s_ctx = jax.nn.silu(c_ctx)
    xc = ctx
    for i in range(DEPTH):
        kind, j = i % N_MIXERS, i // N_MIXERS
        ctx_out = i < DEPTH - 1
        shift, scale, gate = jnp.split(s_lat @ w_mod[i] + b_mod[i], 3, axis=-1)
        shift_c, scale_c, gate_c = jnp.split(s_ctx @ w_mod[i] + b_mod[i], 3, axis=-1)
        h = rmsnorm(x, g_pre[i]) * (1.0 + scale[:, None, :]) + shift[:, None, :]
        hc = rmsnorm(xc, g_pre[i]) * (1.0 + scale_c) + shift_c
        if kind == 0:
            y, yc = mixer_gla(h, hc, a_w_in[j], a_w_g1[j], a_w_g2[j], a_b_g[j], a_g_head[j], a_w_out[j], ctx_out)
        elif kind == 1:
            y, yc = mixer_swa(h, hc, b_w_in[j], b_sink[j], b_w_out[j], rope, ctx_out)
        else:
            y, yc = mixer_rglru(h, hc, c_w_in[j], c_conv_w[j], c_conv_b[j], c_w_ra[j], c_b_ra[j],
                                c_w_ri[j], c_b_ri[j], c_lam[j], c_w_out[j], ctx_out)
        x = x + gate[:, None, :] * rmsnorm(y, g_post[i])
        if ctx_out:
            xc = xc + gate_c * rmsnorm(yc, g_post[i])
    return x
```

```python
import functools

import jax
import jax.numpy as jnp
from jax import lax
from jax.experimental import pallas as pl
from jax.experimental.pallas import tpu as pltpu

F32 = jnp.float32
BF16 = jnp.bfloat16

EPS = 1e-6
GRID_W = 64
N_MIXERS = 3

GLA_HEADS = 4
GLA_TAU = 16.0
GLA_GATE_PAD = 128

ATT_HEAD_DIM = 64
ATT_KV_HEADS = 4
ATT_GROUP = 4
ATT_BLOCK = 128
ROPE_BASE = 10000.0

RNN_HEADS = 10
RNN_HD = 128
CONV_W = 4
CONV_LEFT = 2
LRU_C = 8.0

ROW_TILE = 256
GLA_CHUNK = 64
RNN_TILE = 256
HALO = 16
MOD_ROWS = 8
VMEM_LIMIT = 48 * 1024 * 1024


def _cparams(*sem):
    return pltpu.CompilerParams(dimension_semantics=sem, vmem_limit_bytes=VMEM_LIMIT)


def _sigmoid(x):
    return 1.0 / (1.0 + jnp.exp(-x))


def _silu(x):
    return x * _sigmoid(x)


def _softplus(x):
    return jnp.maximum(x, 0.0) + jnp.log1p(jnp.exp(-jnp.abs(x)))


def _split3(a):
    hi = a.astype(BF16)
    r = a - hi.astype(F32)
    mid = r.astype(BF16)
    lo = (r - mid.astype(F32)).astype(BF16)
    return hi, mid, lo


def _dot(a, b):
    return jnp.dot(a, b, preferred_element_type=F32)


def _dot_nt(a, b):
    return lax.dot_general(a, b, (((1,), (1,)), ((), ())), preferred_element_type=F32)


def _dot_tn(a, b):
    return lax.dot_general(a, b, (((0,), (0,)), ((), ())), preferred_element_type=F32)


def _rms(x):
    return x * lax.rsqrt(jnp.mean(x * x, axis=-1, keepdims=True) + EPS)


def _mod_kernel(c_ref, w_ref, b_ref, o_ref):
    s = _silu(c_ref[...])
    s_hi, s_mid, s_lo = _split3(s)
    w_hi, w_mid, w_lo = _split3(w_ref[...])
    acc = _dot(s_hi, w_hi)
    acc += _dot(s_hi, w_mid) + _dot(s_mid, w_hi)
    acc += _dot(s_hi, w_lo) + _dot(s_mid, w_mid) + _dot(s_lo, w_hi)
    o_ref[...] = acc + b_ref[...]


def _modulation(cvec, w_mod, b_mod):
    depth, d, d3 = w_mod.shape
    out = pl.pallas_call(
        _mod_kernel,
        out_shape=jax.ShapeDtypeStruct((depth, MOD_ROWS, d3), F32),
        grid=(depth, d3 // d),
        in_specs=[
            pl.BlockSpec((MOD_ROWS, d), lambda i, j: (0, 0)),
            pl.BlockSpec((None, d, d), lambda i, j: (i, 0, j)),
            pl.BlockSpec((None, 1, d), lambda i, j: (i, 0, j)),
        ],
        out_specs=pl.BlockSpec((None, MOD_ROWS, d), lambda i, j: (i, 0, j)),
        compiler_params=_cparams("parallel", "parallel"),
        name="modulation",
    )(cvec, w_mod, b_mod.reshape(depth, 1, d3))
    return out.reshape(depth, MOD_ROWS, 3, d)


def _prenorm(x, g, mod):
    return _rms(x) * g * (1.0 + mod[1:2]) + mod[0:1]


def _mod_spec(d, n_ctx_tiles, ctx_row, offset=0):
    def index(b, t):
        return (jnp.where(t + offset < n_ctx_tiles, ctx_row, b), 0, 0)
    return pl.BlockSpec((None, 3, d), index)


def _row_spec(width, offset=0):
    return pl.BlockSpec((None, ROW_TILE, width), lambda b, t: (b, t + offset, 0))


def _full_spec(shape):
    zeros = (0,) * len(shape)
    return pl.BlockSpec(shape, lambda b, t: zeros)


def _gla_pre_kernel(x_ref, mod_ref, g_ref, w_ref, wg1_ref, wg2_ref, bg_ref,
                    qk_ref, v_ref, z_ref, la_ref, *, dk):
    h = _prenorm(x_ref[...], g_ref[...], mod_ref[...]).astype(BF16)
    hk = dk // GLA_HEADS
    q = _dot(h, w_ref[:, :dk]) * hk ** -0.5
    qk_ref[:, :dk] = q.astype(BF16)
    qk_ref[:, dk:] = _dot(h, w_ref[:, dk:2 * dk]).astype(BF16)
    dv = v_ref.shape[-1]
    v_ref[...] = _dot(h, w_ref[:, 2 * dk:2 * dk + dv]).astype(BF16)
    z_ref[...] = _dot(h, w_ref[:, 2 * dk + dv:]).astype(BF16)
    g1 = _dot(h, wg1_ref[...]).astype(BF16)
    gate = _dot(g1, wg2_ref[...]) + bg_ref[...]
    la_ref[...] = -_softplus(-gate) / GLA_TAU


def _gla_core_kernel(qkf_ref, vf_ref, laf_ref, qkb_ref, vb_ref, lab_ref,
                     of_ref, ob_ref, st_ref, *, dk):
    c = qkf_ref.shape[0]
    hk = dk // GLA_HEADS
    hv = vf_ref.shape[-1] // GLA_HEADS

    @pl.when(pl.program_id(1) == 0)
    def _():
        st_ref[...] = jnp.zeros_like(st_ref)

    row = lax.broadcasted_iota(jnp.int32, (c, c), 0)
    col = lax.broadcasted_iota(jnp.int32, (c, c), 1)
    mid = c // 2
    for d, (qk_ref, v_ref, la_ref, o_ref) in enumerate(
            ((qkf_ref, vf_ref, laf_ref, of_ref), (qkb_ref, vb_ref, lab_ref, ob_ref))):
        keep = (col <= row) if d == 0 else (col >= row)
        last = c - 1 if d == 0 else 0
        tri = jnp.where(keep, 1.0, 0.0).astype(BF16)
        la_hi, la_mid, la_lo = _split3(la_ref[:, d * dk:(d + 1) * dk])
        cum = _dot(tri, la_hi) + _dot(tri, la_mid) + _dot(tri, la_lo)
        for h in range(GLA_HEADS):
            b = cum[:, h * hk:(h + 1) * hk]
            b_last = b[last:last + 1]
            b_mid = b[mid:mid + 1]
            q = qk_ref[:, h * hk:(h + 1) * hk].astype(F32)
            k = qk_ref[:, dk + h * hk:dk + (h + 1) * hk].astype(F32)
            v = v_ref[:, h * hv:(h + 1) * hv]
            qe = (q * jnp.exp(b - b_mid)).astype(BF16)
            ke = (k * jnp.exp(b_mid - b)).astype(BF16)
            qs = (q * jnp.exp(b)).astype(BF16)
            kd = (k * jnp.exp(b_last - b)).astype(BF16)
            scores = jnp.where(keep, _dot_nt(qe, ke), 0.0).astype(BF16)
            st = st_ref[d, h]
            o = _dot(scores, v) + _dot_nt(qs, st.astype(BF16))
            o_ref[:, h * hv:(h + 1) * hv] = o.astype(BF16)
            st_ref[d, h] = st * jnp.exp(b_last) + _dot_tn(v, kd)


def _bwd_order(n, n_ctx, n_all):
    return jnp.where(n < n_ctx, n_ctx - 1 - n, n_all - 1 - (n - n_ctx))


def _gla_core(qk, v, la, n_ctx_rows):
    bsz, tt, dv = v.shape
    dk = qk.shape[-1] // 2
    c = GLA_CHUNK
    n_all, n_ctx = tt // c, n_ctx_rows // c
    fwd = lambda b, n: (b, n, 0)
    bwd = lambda b, n: (b, _bwd_order(n, n_ctx, n_all), 0)
    spec = lambda w, idx: pl.BlockSpec((None, c, w), idx)
    return pl.pallas_call(
        functools.partial(_gla_core_kernel, dk=dk),
        out_shape=[jax.ShapeDtypeStruct((bsz, tt, dv), BF16)] * 2,
        grid=(bsz, n_all),
        in_specs=[spec(2 * dk, fwd), spec(dv, fwd), spec(2 * dk, fwd),
                  spec(2 * dk, bwd), spec(dv, bwd), spec(2 * dk, bwd)],
        out_specs=[spec(dv, fwd), spec(dv, bwd)],
        scratch_shapes=[pltpu.VMEM((2, GLA_HEADS, dv // GLA_HEADS, dk // GLA_HEADS), F32)],
        compiler_params=_cparams("arbitrary", "arbitrary"),
        name="gla_core",
    )(qk, v, la, qk, v, la)


def _rope(x, cos, sin_signed):
    lane = lax.broadcasted_iota(jnp.int32, (x.shape[0], 128), 1)
    first = (lane % 32) < 16
    out = []
    for j in range(x.shape[1] // 128):
        xj = x[:, j * 128:(j + 1) * 128]
        partner = jnp.where(first, pltpu.roll(xj, 112, 1), pltpu.roll(xj, 16, 1))
        out.append(xj * cos + partner * sin_signed)
    return jnp.concatenate(out, axis=1)


def _swa_pre_kernel(x_ref, mod_ref, g_ref, w_ref, cos_ref, sin_ref, q_ref, kv_ref, z_ref):
    h = _prenorm(x_ref[...], g_ref[...], mod_ref[...]).astype(BF16)
    qw = q_ref.shape[-1]
    kw = kv_ref.shape[-1] // 2
    cos, sin = cos_ref[...], sin_ref[...]
    q = _rope(_dot(h, w_ref[:, :qw]), cos, sin) * ATT_HEAD_DIM ** -0.5
    q_ref[...] = q.astype(BF16)
    kv_ref[:, :kw] = _rope(_dot(h, w_ref[:, qw:qw + kw]), cos, sin).astype(BF16)
    kv_ref[:, kw:] = _dot(h, w_ref[:, qw + kw:qw + 2 * kw]).astype(BF16)
    z_ref[...] = _dot(h, w_ref[:, qw + 2 * kw:]).astype(BF16)


def _swa_core_kernel(sink_ref, q_ref, kvp_ref, kvc_ref, kvn_ref, kvx_ref, o_ref, *, n_ctx, n_all):
    t = pl.program_id(1)
    blk = ATT_BLOCK
    n_keys = 3 * blk + kvx_ref.shape[0]
    kw = kvc_ref.shape[-1] // 2
    i = lax.broadcasted_iota(jnp.int32, (blk, n_keys), 0)
    j = lax.broadcasted_iota(jnp.int32, (blk, n_keys), 1)
    never = n_keys
    off_prev = jnp.where(t > n_ctx, 0, never)
    off_cur = jnp.where(t >= n_ctx, 0, never)
    off_next = jnp.where(jnp.logical_and(t >= n_ctx, t < n_all - 1), 0, never)
    valid = (((j >= i + off_prev) & (j < blk))
             | ((j >= blk + off_cur) & (j < 2 * blk))
             | ((j >= 2 * blk) & (j <= i + 2 * blk - off_next) & (j < 3 * blk))
             | (j >= 3 * blk))
    grow = lax.broadcasted_iota(jnp.int32, (ATT_GROUP * blk, 1), 0)
    for h in range(ATT_KV_HEADS):
        ks = slice(h * ATT_HEAD_DIM, (h + 1) * ATT_HEAD_DIM)
        vs = slice(kw + h * ATT_HEAD_DIM, kw + (h + 1) * ATT_HEAD_DIM)
        k = jnp.concatenate([kvp_ref[:, ks], kvc_ref[:, ks], kvn_ref[:, ks], kvx_ref[:, ks]], axis=0)
        v = jnp.concatenate([kvp_ref[:, vs], kvc_ref[:, vs], kvn_ref[:, vs], kvx_ref[:, vs]], axis=0)
        q = jnp.concatenate(
            [q_ref[:, (h * ATT_GROUP + g) * ATT_HEAD_DIM:(h * ATT_GROUP + g + 1) * ATT_HEAD_DIM]
             for g in range(ATT_GROUP)], axis=0)
        s = _dot_nt(q, k)
        s = jnp.concatenate([jnp.where(valid, s[g * blk:(g + 1) * blk], -jnp.inf)
                             for g in range(ATT_GROUP)], axis=0)
        sink = jnp.zeros((ATT_GROUP * blk, 1), F32)
        for g in range(ATT_GROUP):
            sink = jnp.where(grow >= g * blk, sink_ref[h * ATT_GROUP + g], sink)
        m = jnp.maximum(jnp.max(s, axis=-1, keepdims=True), sink)
        p = jnp.exp(s - m)
        denom = jnp.sum(p, axis=-1, keepdims=True) + jnp.exp(sink - m)
        o = _dot(p.astype(BF16), v) * (1.0 / denom)
        for g in range(ATT_GROUP):
            lo = (h * ATT_GROUP + g) * ATT_HEAD_DIM
            o_ref[:, lo:lo + ATT_HEAD_DIM] = o[g * blk:(g + 1) * blk].astype(BF16)


def _swa_core(q, kv, sink, n_ctx_rows):
    bsz, tt, qw = q.shape
    blk = ATT_BLOCK
    n_all, n_ctx = tt // blk, n_ctx_rows // blk
    kvw = kv.shape[-1]
    kv_spec = lambda idx: pl.BlockSpec((None, blk, kvw), idx)
    return pl.pallas_call(
        functools.partial(_swa_core_kernel, n_ctx=n_ctx, n_all=n_all),
        out_shape=jax.ShapeDtypeStruct((bsz, tt, qw), BF16),
        grid=(bsz, n_all),
        in_specs=[
            pl.BlockSpec(memory_space=pltpu.SMEM),
            pl.BlockSpec((None, blk, qw), lambda b, t: (b, t, 0)),
            kv_spec(lambda b, t: (b, jnp.maximum(t - 1, 0), 0)),
            kv_spec(lambda b, t: (b, t, 0)),
            kv_spec(lambda b, t: (b, jnp.minimum(t + 1, n_all - 1), 0)),
            pl.BlockSpec((None, n_ctx_rows, kvw), lambda b, t: (b, 0, 0)),
        ],
        out_specs=pl.BlockSpec((None, blk, qw), lambda b, t: (b, t, 0)),
        compiler_params=_cparams("parallel", "parallel"),
        name="swa_core",
    )(sink, q, kv, kv, kv, kv)


def _rnn_pre_kernel(x_ref, mod_ref, g_ref, w_ref, u_ref, z_ref):
    h = _prenorm(x_ref[...], g_ref[...], mod_ref[...]).astype(BF16)
    rw = u_ref.shape[-1]
    u_ref[...] = _dot(h, w_ref[:, :rw]).astype(BF16)
    z_ref[...] = _dot(h, w_ref[:, rw:]).astype(BF16)


def _conv_tile(prev_ref, cur_ref, next_ref, cw_ref, cb_ref, prev_ok, next_ok):
    n = cur_ref.shape[0]
    top = jnp.where(prev_ok, prev_ref[...].astype(F32)[HALO - 8:], 0.0)
    bot = jnp.where(next_ok, next_ref[...].astype(F32)[:8], 0.0)
    ext = jnp.concatenate([top, cur_ref[...].astype(F32), bot], axis=0)
    rows = ext.shape[0]
    out = cb_ref[...] + ext[8:8 + n] * cw_ref[CONV_LEFT:CONV_LEFT + 1]
    for tap in range(CONV_W):
        shift = CONV_LEFT - tap
        if shift != 0:
            out = out + pltpu.roll(ext, shift % rows, 0)[8:8 + n] * cw_ref[tap:tap + 1]
    return out


def _scan_tile(a, x, carry, reverse):
    n = a.shape[0]
    rowmod = lax.broadcasted_iota(jnp.int32, a.shape, 0) % 8
    for s in (1, 2, 4):
        shift = (n - s) if reverse else s
        inside = (rowmod < 8 - s) if reverse else (rowmod >= s)
        x = x + a * jnp.where(inside, pltpu.roll(x, shift, 0), 0.0)
        a = a * jnp.where(inside, pltpu.roll(a, shift, 0), 1.0)
    groups = range(n // 8 - 1, -1, -1) if reverse else range(n // 8)
    edge = 0 if reverse else 7
    out = [None] * (n // 8)
    for g in groups:
        hg = x[8 * g:8 * g + 8] + a[8 * g:8 * g + 8] * carry
        out[g] = hg
        carry = jnp.broadcast_to(hg[edge:edge + 1], hg.shape)
    return jnp.concatenate(out, axis=0), carry


def _rnn_core_kernel(upf_ref, ucf_ref, unf_ref, upb_ref, ucb_ref, unb_ref,
                     cw_ref, cb_ref, wg_ref, ba_ref, bx_ref, lam_ref,
                     hf_ref, hb_ref, conv_ref, carry_ref, *, n_ctx, n_all):
    n = pl.program_id(1)

    @pl.when(n == 0)
    def _():
        carry_ref[...] = jnp.zeros_like(carry_ref)

    tiles = (n, _bwd_order(n, n_ctx, n_all))
    halos = ((upf_ref, ucf_ref, unf_ref), (upb_ref, ucb_ref, unb_ref))
    for d in range(2):
        t = tiles[d]
        prev_ok = jnp.logical_and(t != 0, t != n_ctx)
        next_ok = jnp.logical_and(t != n_ctx - 1, t != n_all - 1)
        conv_ref[d] = _conv_tile(*halos[d], cw_ref, cb_ref, prev_ok, next_ok)

    def head(j, _):
        lanes = pl.ds(pl.multiple_of(j * RNN_HD, RNN_HD), RNN_HD)
        for d, h_ref in enumerate((hf_ref, hb_ref)):
            u = conv_ref[d, :, lanes]
            pre = _dot(u.astype(BF16), wg_ref[d, j])
            r = _sigmoid(pre[:, :RNN_HD] + ba_ref[d, j])
            i = _sigmoid(pre[:, RNN_HD:] + bx_ref[d, j])
            log_a = -LRU_C * r * _softplus(-lam_ref[d, j])
            a = jnp.exp(log_a)
            x = jnp.sqrt(-jnp.tanh(log_a) * (1.0 + a * a)) * i * u
            hs, carry = _scan_tile(a, x, carry_ref[d, j], reverse=(d == 1))
            h_ref[:, lanes] = hs.astype(BF16)
            carry_ref[d, j] = carry
        return 0

    lax.fori_loop(0, RNN_HEADS, head, 0)


def _rnn_core(u, conv_w, conv_b, wg, b_a, b_x, lam, n_ctx_rows):
    bsz, tt, rw = u.shape
    tile = RNN_TILE
    n_all, n_ctx = tt // tile, n_ctx_rows // tile
    per = tile // HALO
    n_halo = tt // HALO
    fwd = lambda n: n
    bwd = lambda n: _bwd_order(n, n_ctx, n_all)

    def specs(order):
        return [
            pl.BlockSpec((None, HALO, rw), lambda b, n: (b, jnp.maximum(order(n) * per - 1, 0), 0)),
            pl.BlockSpec((None, tile, rw), lambda b, n: (b, order(n), 0)),
            pl.BlockSpec((None, HALO, rw), lambda b, n: (b, jnp.minimum((order(n) + 1) * per, n_halo - 1), 0)),
        ]

    out_spec = lambda order: pl.BlockSpec((None, tile, rw), lambda b, n: (b, order(n), 0))
    return pl.pallas_call(
        functools.partial(_rnn_core_kernel, n_ctx=n_ctx, n_all=n_all),
        out_shape=[jax.ShapeDtypeStruct((bsz, tt, rw), BF16)] * 2,
        grid=(bsz, n_all),
        in_specs=specs(fwd) + specs(bwd) + [
            _full_spec(conv_w.shape), _full_spec(conv_b.shape), _full_spec(wg.shape),
            _full_spec(b_a.shape), _full_spec(b_x.shape), _full_spec(lam.shape)],
        out_specs=[out_spec(fwd), out_spec(bwd)],
        scratch_shapes=[pltpu.VMEM((2, tile, rw), F32),
                        pltpu.VMEM((2, RNN_HEADS, 8, RNN_HD), F32)],
        compiler_params=_cparams("arbitrary", "arbitrary"),
        name="rglru_core",
    )(u, u, u, u, u, u, conv_w, conv_b, wg, b_a, b_x, lam)


def _finish(u, x_ref, mod_ref, gp_ref, w_ref, o_ref):
    y = _dot(u.astype(BF16), w_ref[...])
    o_ref[...] = x_ref[...] + mod_ref[2:3] * (_rms(y) * gp_ref[...])


def _gla_post_kernel(of_ref, ob_ref, z_ref, gh_ref, x_ref, mod_ref, gp_ref, w_ref, o_ref):
    o = of_ref[...].astype(F32) + ob_ref[...].astype(F32)
    hv = o.shape[-1] // GLA_HEADS
    o = jnp.concatenate([_rms(o[:, h * hv:(h + 1) * hv]) for h in range(GLA_HEADS)], axis=1)
    u = o * gh_ref[...] * _silu(z_ref[...].astype(F32))
    _finish(u, x_ref, mod_ref, gp_ref, w_ref, o_ref)


def _swa_post_kernel(a_ref, z_ref, x_ref, mod_ref, gp_ref, w_ref, o_ref):
    u = a_ref[...].astype(F32) * _silu(z_ref[...].astype(F32))
    _finish(u, x_ref, mod_ref, gp_ref, w_ref, o_ref)


def _rnn_post_kernel(hf_ref, hb_ref, z_ref, x_ref, mod_ref, gp_ref, w_ref, o_ref):
    u = (hf_ref[...].astype(F32) + hb_ref[...].astype(F32)) * _silu(z_ref[...].astype(F32))
    _finish(u, x_ref, mod_ref, gp_ref, w_ref, o_ref)


def _post_call(body, name, acts, vecs, xs, mod, g_post, w_out, n_ctx_rows, latent_only):
    bsz, tt, d = xs.shape
    n_ctx = n_ctx_rows // ROW_TILE
    off = n_ctx if latent_only else 0
    n_tiles = tt // ROW_TILE - off
    return pl.pallas_call(
        body,
        out_shape=jax.ShapeDtypeStruct((bsz, n_tiles * ROW_TILE, d), F32),
        grid=(bsz, n_tiles),
        in_specs=[_row_spec(a.shape[-1], off) for a in acts]
        + [_full_spec(v.shape) for v in vecs]
        + [_row_spec(d, off), _mod_spec(d, n_ctx, bsz, off), _full_spec(g_post.shape), _full_spec(w_out.shape)],
        out_specs=_row_spec(d),
        compiler_params=_cparams("parallel", "parallel"),
        name=name,
    )(*acts, *vecs, xs, mod, g_post, w_out)


def _pre_call(body, name, xs, mod, g_pre, consts, out_widths, out_dtypes, n_ctx_rows, row_consts=()):
    bsz, tt, d = xs.shape
    n_ctx = n_ctx_rows // ROW_TILE
    return pl.pallas_call(
        body,
        out_shape=[jax.ShapeDtypeStruct((bsz, tt, w), dt) for w, dt in zip(out_widths, out_dtypes)],
        grid=(bsz, tt // ROW_TILE),
        in_specs=[_row_spec(d), _mod_spec(d, n_ctx, bsz), _full_spec(g_pre.shape)]
        + [_full_spec(c.shape) for c in consts]
        + [pl.BlockSpec((ROW_TILE, r.shape[-1]), lambda b, t: (t, 0)) for r in row_consts],
        out_specs=[_row_spec(w) for w in out_widths],
        compiler_params=_cparams("parallel", "parallel"),
        name=name,
    )(xs, mod, g_pre, *consts, *row_consts)


def _layer_gla(xs, mod, g_pre, g_post, w_in, w_g1, w_g2, b_g, g_head, w_out, n_ctx_rows, last):
    d = xs.shape[-1]
    rank, dk = w_g2.shape[1], w_g2.shape[2]
    dv = w_out.shape[0]
    wg1 = jnp.zeros((d, GLA_GATE_PAD), F32).at[:, :2 * rank].set(jnp.concatenate([w_g1[0], w_g1[1]], axis=1))
    wg2 = jnp.zeros((GLA_GATE_PAD, 2 * dk), F32)
    wg2 = wg2.at[:rank, :dk].set(w_g2[0]).at[rank:2 * rank, dk:].set(w_g2[1])
    qk, v, z, la = _pre_call(
        functools.partial(_gla_pre_kernel, dk=dk), "gla_pre", xs, mod, g_pre,
        (w_in.astype(BF16), wg1.astype(BF16), wg2.astype(BF16), b_g.reshape(1, 2 * dk)),
        (2 * dk, dv, dv, 2 * dk), (BF16, BF16, BF16, F32), n_ctx_rows)
    o_f, o_b = _gla_core(qk, v, la, n_ctx_rows)
    return _post_call(_gla_post_kernel, "gla_post", (o_f, o_b, z), (g_head.reshape(1, dv),),
                      xs, mod, g_post, w_out.astype(BF16), n_ctx_rows, last)


def _layer_swa(xs, mod, g_pre, g_post, w_in, sink, w_out, rope, n_ctx_rows, last):
    qw = w_out.shape[0]
    kw = (w_in.shape[1] - 2 * qw) // 2
    q, kv, z = _pre_call(_swa_pre_kernel, "swa_pre", xs, mod, g_pre, (w_in.astype(BF16),),
                         (qw, 2 * kw, qw), (BF16, BF16, BF16), n_ctx_rows, row_consts=rope)
    a = _swa_core(q, kv, sink, n_ctx_rows)
    return _post_call(_swa_post_kernel, "swa_post", (a, z), (), xs, mod, g_post, w_out.astype(BF16),
                      n_ctx_rows, last)


def _layer_rnn(xs, mod, g_pre, g_post, w_in, conv_w, conv_b, w_ra, b_ra, w_ri, b_ri, lam, w_out,
               n_ctx_rows, last):
    rw = w_out.shape[0]
    u, z = _pre_call(_rnn_pre_kernel, "rglru_pre", xs, mod, g_pre, (w_in.astype(BF16),),
                     (rw, rw), (BF16, BF16), n_ctx_rows)
    wg = jnp.concatenate([w_ra, w_ri], axis=-1).astype(BF16)
    vec = lambda a: a.reshape(2, RNN_HEADS, 1, RNN_HD)
    h_f, h_b = _rnn_core(u, conv_w, conv_b.reshape(1, rw), wg, vec(b_ra), vec(b_ri), vec(lam), n_ctx_rows)
    return _post_call(_rnn_post_kernel, "rglru_post", (h_f, h_b, z), (), xs, mod, g_post,
                      w_out.astype(BF16), n_ctx_rows, last)


def _rope_tables(t, n_ctx_rows):
    pairs = ATT_HEAD_DIM // 4
    pos = jnp.arange(t, dtype=jnp.int32)
    row = (pos // GRID_W).astype(F32)
    col = (pos % GRID_W).astype(F32)
    freqs = ROPE_BASE ** (-jnp.arange(pairs, dtype=F32) / pairs)
    ang_r, ang_c = row[:, None] * freqs, col[:, None] * freqs
    cos = jnp.concatenate([jnp.cos(ang_r)] * 2 + [jnp.cos(ang_c)] * 2, axis=1)
    sin = jnp.concatenate([-jnp.sin(ang_r), jnp.sin(ang_r), -jnp.sin(ang_c), jnp.sin(ang_c)], axis=1)
    cos = jnp.concatenate([jnp.ones((n_ctx_rows, ATT_HEAD_DIM), F32), cos], axis=0)
    sin = jnp.concatenate([jnp.zeros((n_ctx_rows, ATT_HEAD_DIM), F32), sin], axis=0)
    return jnp.tile(cos, (1, 2)), jnp.tile(sin, (1, 2))


def kernel(x, c, ctx, c_ctx, w_mod, b_mod, g_pre, g_post, a_w_in, a_w_g1, a_w_g2, a_b_g, a_g_head, a_w_out, b_w_in, b_sink, b_w_out, c_w_in, c_conv_w, c_conv_b, c_w_ra, c_b_ra, c_w_ri, c_b_ri, c_lam, c_w_out):
    bsz, t, d = x.shape
    n_ctx_rows = ctx.shape[1]
    depth = w_mod.shape[0]
    assert bsz < MOD_ROWS and n_ctx_rows % ROW_TILE == 0 and t % ROW_TILE == 0

    cvec = jnp.zeros((MOD_ROWS, d), F32).at[:bsz].set(c).at[bsz].set(c_ctx)
    mods = _modulation(cvec, w_mod, b_mod)
    rope = _rope_tables(t, n_ctx_rows)
    xs = jnp.concatenate([ctx, x], axis=1)
    for i in range(depth):
        kind, j = i % N_MIXERS, i // N_MIXERS
        last = i == depth - 1
        gpre, gpost = g_pre[i].reshape(1, d), g_post[i].reshape(1, d)
        if kind == 0:
            xs = _layer_gla(xs, mods[i], gpre, gpost, a_w_in[j], a_w_g1[j], a_w_g2[j], a_b_g[j],
                            a_g_head[j], a_w_out[j], n_ctx_rows, last)
        elif kind == 1:
            xs = _layer_swa(xs, mods[i], gpre, gpost, b_w_in[j], b_sink[j], b_w_out[j], rope,
                            n_ctx_rows, last)
        else:
            xs = _layer_rnn(xs, mods[i], gpre, gpost, c_w_in[j], c_conv_w[j], c_conv_b[j], c_w_ra[j],
                            c_b_ra[j], c_w_ri[j], c_b_ri[j], c_lam[j], c_w_out[j], n_ctx_rows, last)
    return xs
```

```python
import functools

import jax
import jax.numpy as jnp
from jax import lax
from jax.experimental import pallas as pl
from jax.experimental.pallas import tpu as pltpu

F32 = jnp.float32
BF16 = jnp.bfloat16

EPS = 1e-6
GRID_W = 64
N_MIXERS = 3

GLA_HEADS = 4
GLA_TAU = 16.0
GLA_GATE_PAD = 128

ATT_HEAD_DIM = 64
ATT_KV_HEADS = 4
ATT_GROUP = 4
ATT_BLOCK = 128
ROPE_BASE = 10000.0

RNN_HEADS = 10
RNN_HD = 128
CONV_W = 4
LRU_C = 8.0

ROW_TILE = 256
GLA_CHUNK = 64
RNN_TILE = 256
HALO = 16
MOD_ROWS = 8
VMEM_LIMIT = 48 * 1024 * 1024


def _cparams(*sem):
    return pltpu.CompilerParams(dimension_semantics=sem, vmem_limit_bytes=VMEM_LIMIT)


def _sigmoid(x):
    return 0.5 * jnp.tanh(0.5 * x) + 0.5


def _silu(x):
    return x * _sigmoid(x)


def _softplus(x):
    return jnp.maximum(x, 0.0) + jnp.log1p(jnp.exp(-jnp.abs(x)))


def _log_sigmoid(x):
    return jnp.minimum(x, 0.0) - jnp.log(1.0 + jnp.exp(-jnp.abs(x)))


def _split3(a):
    hi = a.astype(BF16)
    r = a - hi.astype(F32)
    mid = r.astype(BF16)
    lo = (r - mid.astype(F32)).astype(BF16)
    return hi, mid, lo


def _dot(a, b):
    return jnp.dot(a, b, preferred_element_type=F32)


def _dot_nt(a, b):
    return lax.dot_general(a, b, (((1,), (1,)), ((), ())), preferred_element_type=F32)


def _dot_tn(a, b):
    return lax.dot_general(a, b, (((0,), (0,)), ((), ())), preferred_element_type=F32)


def _rms(x):
    return x * lax.rsqrt(jnp.mean(x * x, axis=-1, keepdims=True) + EPS)


def _mod_kernel(c_ref, w_ref, b_ref, o_ref):
    s = _silu(c_ref[...])
    s_hi, s_mid, s_lo = _split3(s)
    w_hi, w_mid, w_lo = _split3(w_ref[...])
    acc = _dot(s_hi, w_hi)
    acc += _dot(s_hi, w_mid) + _dot(s_mid, w_hi)
    acc += _dot(s_hi, w_lo) + _dot(s_mid, w_mid) + _dot(s_lo, w_hi)
    o_ref[...] = acc + b_ref[...]


def _modulation(cvec, w_mod, b_mod):
    depth, d, d3 = w_mod.shape
    out = pl.pallas_call(
        _mod_kernel,
        out_shape=jax.ShapeDtypeStruct((depth, MOD_ROWS, d3), F32),
        grid=(depth, d3 // d),
        in_specs=[
            pl.BlockSpec((MOD_ROWS, d), lambda i, j: (0, 0)),
            pl.BlockSpec((None, d, d), lambda i, j: (i, 0, j)),
            pl.BlockSpec((None, 1, d), lambda i, j: (i, 0, j)),
        ],
        out_specs=pl.BlockSpec((None, MOD_ROWS, d), lambda i, j: (i, 0, j)),
        compiler_params=_cparams("parallel", "parallel"),
        name="modulation",
    )(cvec, w_mod, b_mod.reshape(depth, 1, d3))
    return out.reshape(depth, MOD_ROWS, 3, d)


def _prenorm(x, g, mod):
    return _rms(x) * g * (1.0 + mod[1:2]) + mod[0:1]


def _mod_spec(d, n_ctx_tiles, ctx_row, offset=0):
    def index(b, t):
        return (jnp.where(t + offset < n_ctx_tiles, ctx_row, b), 0, 0)
    return pl.BlockSpec((None, 3, d), index)


def _row_spec(width, offset=0):
    return pl.BlockSpec((None, ROW_TILE, width), lambda b, t: (b, t + offset, 0))


def _token_specs(xs, n_ctx, offset=0):
    if not isinstance(xs, tuple):
        return [_row_spec(xs.shape[-1], offset)]
    d = xs[0].shape[-1]
    return [pl.BlockSpec((None, ROW_TILE, d), lambda b, t: (b, jnp.minimum(t + offset, n_ctx - 1), 0)),
            pl.BlockSpec((None, ROW_TILE, d), lambda b, t: (b, jnp.maximum(t + offset - n_ctx, 0), 0))]


def _token_tile(x_refs, n_ctx_tiles):
    if len(x_refs) == 1:
        return x_refs[0][...]
    return jnp.where(pl.program_id(1) < n_ctx_tiles, x_refs[0][...], x_refs[1][...])


def _full_spec(shape):
    zeros = (0,) * len(shape)
    return pl.BlockSpec(shape, lambda b, t: zeros)


def _gla_pre_kernel(*refs, dk, n_ctx):
    (mod_ref, g_ref, w_ref, wg1_ref, wg2_ref, bg_ref, qk_ref, v_ref, z_ref, la_ref) = refs[-10:]
    h = _prenorm(_token_tile(refs[:-10], n_ctx), g_ref[...], mod_ref[...]).astype(BF16)
    hk = dk // GLA_HEADS
    g1 = _dot(h, wg1_ref[...]).astype(BF16)
    gate = _dot(g1, wg2_ref[...]) + bg_ref[...]
    la_ref[...] = _log_sigmoid(gate) / GLA_TAU
    q = _dot(h, w_ref[:, :dk]) * hk ** -0.5
    qk_ref[:, :dk] = q.astype(BF16)
    qk_ref[:, dk:] = _dot(h, w_ref[:, dk:2 * dk]).astype(BF16)
    dv = v_ref.shape[-1]
    v_ref[...] = _dot(h, w_ref[:, 2 * dk:2 * dk + dv]).astype(BF16)
    z_ref[...] = _dot(h, w_ref[:, 2 * dk + dv:]).astype(BF16)


def _gla_core_kernel(qkf_ref, vf_ref, laf_ref, qkb_ref, vb_ref, lab_ref,
                     of_ref, ob_ref, st_ref, *, dk):
    c = qkf_ref.shape[0]
    hk = dk // GLA_HEADS
    hv = vf_ref.shape[-1] // GLA_HEADS

    @pl.when(pl.program_id(1) == 0)
    def _():
        st_ref[...] = jnp.zeros_like(st_ref)

    row = lax.broadcasted_iota(jnp.int32, (c, c), 0)
    col = lax.broadcasted_iota(jnp.int32, (c, c), 1)
    mid = c // 2
    for d, (qk_ref, v_ref, la_ref, o_ref) in enumerate(
            ((qkf_ref, vf_ref, laf_ref, of_ref), (qkb_ref, vb_ref, lab_ref, ob_ref))):
        keep = (col <= row) if d == 0 else (col >= row)
        last = c - 1 if d == 0 else 0
        tri = jnp.where(keep, 1.0, 0.0).astype(BF16)
        la_hi, la_mid, la_lo = _split3(la_ref[:, d * dk:(d + 1) * dk])
        cum = _dot(tri, la_hi) + _dot(tri, la_mid) + _dot(tri, la_lo)
        for h in range(GLA_HEADS):
            b = cum[:, h * hk:(h + 1) * hk]
            b_last = b[last:last + 1]
            b_mid = b[mid:mid + 1]
            q = qk_ref[:, h * hk:(h + 1) * hk].astype(F32)
            k = qk_ref[:, dk + h * hk:dk + (h + 1) * hk].astype(F32)
            v = v_ref[:, h * hv:(h + 1) * hv]
            qe = (q * jnp.exp(b - b_mid)).astype(BF16)
            ke = (k * jnp.exp(b_mid - b)).astype(BF16)
            qs = (q * jnp.exp(b)).astype(BF16)
            kd = (k * jnp.exp(b_last - b)).astype(BF16)
            scores = jnp.where(keep, _dot_nt(qe, ke), 0.0).astype(BF16)
            st = st_ref[d, h]
            o = _dot(scores, v) + _dot_nt(qs, st.astype(BF16))
            o_ref[:, h * hv:(h + 1) * hv] = o.astype(BF16)
            st_ref[d, h] = st * jnp.exp(b_last) + _dot_tn(v, kd)


def _bwd_order(n, n_ctx, n_all):
    return jnp.where(n < n_ctx, n_ctx - 1 - n, n_all - 1 - (n - n_ctx))


def _gla_core(qk, v, la, n_ctx_rows):
    bsz, tt, dv = v.shape
    dk = qk.shape[-1] // 2
    c = GLA_CHUNK
    n_all, n_ctx = tt // c, n_ctx_rows // c
    fwd = lambda b, n: (b, n, 0)
    bwd = lambda b, n: (b, _bwd_order(n, n_ctx, n_all), 0)
    spec = lambda w, idx: pl.BlockSpec((None, c, w), idx)
    return pl.pallas_call(
        functools.partial(_gla_core_kernel, dk=dk),
        out_shape=[jax.ShapeDtypeStruct((bsz, tt, dv), BF16)] * 2,
        grid=(bsz, n_all),
        in_specs=[spec(2 * dk, fwd), spec(dv, fwd), spec(2 * dk, fwd),
                  spec(2 * dk, bwd), spec(dv, bwd), spec(2 * dk, bwd)],
        out_specs=[spec(dv, fwd), spec(dv, bwd)],
        scratch_shapes=[pltpu.VMEM((2, GLA_HEADS, dv // GLA_HEADS, dk // GLA_HEADS), F32)],
        compiler_params=_cparams("arbitrary", "arbitrary"),
        name="gla_core",
    )(qk, v, la, qk, v, la)


def _rope(x, cos, sin_signed):
    lane = lax.broadcasted_iota(jnp.int32, (x.shape[0], 128), 1)
    first = (lane % 32) < 16
    out = []
    for j in range(x.shape[1] // 128):
        xj = x[:, j * 128:(j + 1) * 128]
        partner = jnp.where(first, pltpu.roll(xj, 112, 1), pltpu.roll(xj, 16, 1))
        out.append(xj * cos + partner * sin_signed)
    return jnp.concatenate(out, axis=1)


def _swa_pre_kernel(*refs, n_ctx):
    (mod_ref, g_ref, w_ref, cos_ref, sin_ref, q_ref, kv_ref, z_ref) = refs[-8:]
    h = _prenorm(_token_tile(refs[:-8], n_ctx), g_ref[...], mod_ref[...]).astype(BF16)
    qw = q_ref.shape[-1]
    kw = kv_ref.shape[-1] // 2
    cos, sin = cos_ref[...], sin_ref[...]
    q = _rope(_dot(h, w_ref[:, :qw]), cos, sin) * ATT_HEAD_DIM ** -0.5
    q_ref[...] = q.astype(BF16)
    kv_ref[:, :kw] = _rope(_dot(h, w_ref[:, qw:qw + kw]), cos, sin).astype(BF16)
    kv_ref[:, kw:] = _dot(h, w_ref[:, qw + kw:qw + 2 * kw]).astype(BF16)
    z_ref[...] = _dot(h, w_ref[:, qw + 2 * kw:]).astype(BF16)


def _swa_core_kernel(sink_ref, q_ref, kvp_ref, kvc_ref, kvn_ref, kvx_ref, o_ref, *, n_ctx, n_all):
    t = pl.program_id(1)
    blk = ATT_BLOCK
    n_keys = 3 * blk + kvx_ref.shape[0]
    kw = kvc_ref.shape[-1] // 2
    i = lax.broadcasted_iota(jnp.int32, (blk, n_keys), 0)
    j = lax.broadcasted_iota(jnp.int32, (blk, n_keys), 1)
    never = n_keys
    off_prev = jnp.where(t > n_ctx, 0, never)
    off_cur = jnp.where(t >= n_ctx, 0, never)
    off_next = jnp.where(jnp.logical_and(t >= n_ctx, t < n_all - 1), 0, never)
    valid = (((j >= i + off_prev) & (j < blk))
             | ((j >= blk + off_cur) & (j < 2 * blk))
             | ((j >= 2 * blk) & (j <= i + 2 * blk - off_next) & (j < 3 * blk))
             | (j >= 3 * blk))
    grow = lax.broadcasted_iota(jnp.int32, (ATT_GROUP * blk, 1), 0)
    for h in range(ATT_KV_HEADS):
        ks = slice(h * ATT_HEAD_DIM, (h + 1) * ATT_HEAD_DIM)
        vs = slice(kw + h * ATT_HEAD_DIM, kw + (h + 1) * ATT_HEAD_DIM)
        k = jnp.concatenate([kvp_ref[:, ks], kvc_ref[:, ks], kvn_ref[:, ks], kvx_ref[:, ks]], axis=0)
        v = jnp.concatenate([kvp_ref[:, vs], kvc_ref[:, vs], kvn_ref[:, vs], kvx_ref[:, vs]], axis=0)
        q = jnp.concatenate(
            [q_ref[:, (h * ATT_GROUP + g) * ATT_HEAD_DIM:(h * ATT_GROUP + g + 1) * ATT_HEAD_DIM]
             for g in range(ATT_GROUP)], axis=0)
        s = _dot_nt(q, k)
        s = jnp.concatenate([jnp.where(valid, s[g * blk:(g + 1) * blk], -jnp.inf)
                             for g in range(ATT_GROUP)], axis=0)
        sink = jnp.zeros((ATT_GROUP * blk, 1), F32)
        for g in range(ATT_GROUP):
            sink = jnp.where(grow >= g * blk, sink_ref[h * ATT_GROUP + g], sink)
        m = jnp.maximum(jnp.max(s, axis=-1, keepdims=True), sink)
        p = jnp.exp(s - m)
        denom = jnp.sum(p, axis=-1, keepdims=True) + jnp.exp(sink - m)
        o = _dot(p.astype(BF16), v) * (1.0 / denom)
        for g in range(ATT_GROUP):
            lo = (h * ATT_GROUP + g) * ATT_HEAD_DIM
            o_ref[:, lo:lo + ATT_HEAD_DIM] = o[g * blk:(g + 1) * blk].astype(BF16)


def _swa_core(q, kv, sink, n_ctx_rows):
    bsz, tt, qw = q.shape
    blk = ATT_BLOCK
    n_all, n_ctx = tt // blk, n_ctx_rows // blk
    kvw = kv.shape[-1]
    kv_spec = lambda idx: pl.BlockSpec((None, blk, kvw), idx)
    return pl.pallas_call(
        functools.partial(_swa_core_kernel, n_ctx=n_ctx, n_all=n_all),
        out_shape=jax.ShapeDtypeStruct((bsz, tt, qw), BF16),
        grid=(bsz, n_all),
        in_specs=[
            pl.BlockSpec(memory_space=pltpu.SMEM),
            pl.BlockSpec((None, blk, qw), lambda b, t: (b, t, 0)),
            kv_spec(lambda b, t: (b, jnp.maximum(t - 1, 0), 0)),
            kv_spec(lambda b, t: (b, t, 0)),
            kv_spec(lambda b, t: (b, jnp.minimum(t + 1, n_all - 1), 0)),
            pl.BlockSpec((None, n_ctx_rows, kvw), lambda b, t: (b, 0, 0)),
        ],
        out_specs=pl.BlockSpec((None, blk, qw), lambda b, t: (b, t, 0)),
        compiler_params=_cparams("parallel", "parallel"),
        name="swa_core",
    )(sink, q, kv, kv, kv, kv)


def _segment_permutation(n):
    r = jnp.arange(n)
    p = (r[None, :] == ((r % 8) * (n // 8) + r // 8)[:, None]).astype(BF16)
    return p, p.T


def _rnn_pre_kernel(x_ref, mod_ref, g_ref, w_ref, u_ref, z_ref, *, n_ctx):
    del n_ctx
    h = _prenorm(x_ref[...], g_ref[...], mod_ref[...]).astype(BF16)
    rw = u_ref.shape[-1]
    u_ref[...] = _dot(h, w_ref[:, :rw]).astype(BF16)
    z_ref[...] = _dot(h, w_ref[:, rw:]).astype(BF16)


def _shift_rows(cur, other, up):
    sub = lax.broadcasted_iota(jnp.int32, cur.shape, 0)
    if up:
        return jnp.where(sub == 7, pltpu.roll(other, 7, 0), pltpu.roll(cur, 7, 0))
    return jnp.where(sub == 0, pltpu.roll(other, 1, 0), pltpu.roll(cur, 1, 0))


def _segment_scan(a, x, carry, reverse):
    n_g = a.shape[0] // 8
    order = range(n_g - 1, -1, -1) if reverse else range(n_g)
    hs, ps = [None] * n_g, [None] * n_g
    h = p = None
    for g in order:
        ag, xg = a[8 * g:8 * g + 8], x[8 * g:8 * g + 8]
        h = xg if h is None else ag * h + xg
        p = ag if p is None else ag * p
        hs[g], ps[g] = h, p
    sub = lax.broadcasted_iota(jnp.int32, h.shape, 0)
    for s in (1, 2, 4):
        shift = 8 - s if reverse else s
        inside = (sub < 8 - s) if reverse else (sub >= s)
        h = h + p * jnp.where(inside, pltpu.roll(h, shift, 0), 0.0)
        p = p * jnp.where(inside, pltpu.roll(p, shift, 0), 1.0)
    full = h + p * carry
    first, last = (7, 0) if reverse else (0, 7)
    seg_in = jnp.where(sub == first, carry, pltpu.roll(full, 7 if reverse else 1, 0))
    out = jnp.concatenate([hs[g] + ps[g] * seg_in for g in range(n_g)], axis=0)
    return out, jnp.broadcast_to(full[last:last + 1], full.shape)


def _rnn_core_kernel(upf_ref, ucf_ref, unf_ref, upb_ref, ucb_ref, unb_ref,
                     cw_ref, cb_ref, wg_ref, ba_ref, bx_ref, lam_ref,
                     perm_ref, unperm_ref, hf_ref, hb_ref, carry_ref, *, n_ctx, n_all):
    n = pl.program_id(1)

    @pl.when(n == 0)
    def _():
        carry_ref[...] = jnp.zeros_like(carry_ref)

    tiles = (n, _bwd_order(n, n_ctx, n_all))
    prev_ok = [jnp.where(jnp.logical_and(t != 0, t != n_ctx), 1.0, 0.0) for t in tiles]
    next_ok = [jnp.where(jnp.logical_and(t != n_ctx - 1, t != n_all - 1), 1.0, 0.0) for t in tiles]
    refs = ((upf_ref, ucf_ref, unf_ref, hf_ref), (upb_ref, ucb_ref, unb_ref, hb_ref))

    useg = [_dot(perm_ref[...], refs[d][1][...]) for d in range(2)]
    rows = useg[0].shape[0]
    convs, pres = {}, {}
    for d, (p_ref, _, n_ref, _) in enumerate(refs):
        before = p_ref[...].astype(F32)[HALO - 8:] * prev_ok[d]
        after = n_ref[...].astype(F32)[:8] * next_ok[d]
        before2 = pltpu.roll(before, 1, 0)
        for j in range(RNN_HEADS):
            lanes = slice(j * RNN_HD, (j + 1) * RNN_HD)
            u = useg[d][:, lanes]
            ext = jnp.concatenate([
                _shift_rows(u[rows - 16:rows - 8], before2[:, lanes], up=False),
                _shift_rows(u[rows - 8:], before[:, lanes], up=False),
                u,
                _shift_rows(u[:8], after[:, lanes], up=True)], axis=0)
            conv = cb_ref[:, lanes]
            for tap in range(CONV_W):
                conv = conv + ext[8 * tap:8 * tap + rows] * cw_ref[tap:tap + 1, lanes]
            convs[d, j] = conv
            pres[d, j] = _dot(conv.astype(BF16), wg_ref[d, j])

    half = RNN_HEADS // 2
    for j0 in (0, half):
        for d in range(2):
            outs = []
            for j in range(j0, j0 + half):
                pre, conv = pres[d, j], convs[d, j]
                r = _sigmoid(pre[:, :RNN_HD] + ba_ref[d, j])
                i = _sigmoid(pre[:, RNN_HD:] + bx_ref[d, j])
                log_a = -LRU_C * r * _softplus(-lam_ref[d, j])
                a = jnp.exp(log_a)
                y = -jnp.tanh(log_a) * (1.0 + a * a)
                x = jnp.where(y > 0.0, y * lax.rsqrt(y), 0.0) * i * conv
                hs, carry = _segment_scan(a, x, carry_ref[d, j], reverse=(d == 1))
                carry_ref[d, j] = carry
                outs.append(hs.astype(BF16))
            h = _dot(unperm_ref[...], jnp.concatenate(outs, axis=1))
            refs[d][3][:, j0 * RNN_HD:(j0 + half) * RNN_HD] = h.astype(BF16)


def _rnn_core(u, conv_w, conv_b, wg, b_a, b_x, lam, n_ctx_rows):
    bsz, tt, rw = u.shape
    tile = RNN_TILE
    n_all, n_ctx = tt // tile, n_ctx_rows // tile
    per = tile // HALO
    n_halo = tt // HALO
    fwd = lambda n: n
    bwd = lambda n: _bwd_order(n, n_ctx, n_all)

    def specs(order):
        return [
            pl.BlockSpec((None, HALO, rw), lambda b, n: (b, jnp.maximum(order(n) * per - 1, 0), 0)),
            pl.BlockSpec((None, tile, rw), lambda b, n: (b, order(n), 0)),
            pl.BlockSpec((None, HALO, rw), lambda b, n: (b, jnp.minimum((order(n) + 1) * per, n_halo - 1), 0)),
        ]

    out_spec = lambda order: pl.BlockSpec((None, tile, rw), lambda b, n: (b, order(n), 0))
    perm, unperm = _segment_permutation(tile)
    consts = (conv_w, conv_b, wg, b_a, b_x, lam, perm, unperm)
    return pl.pallas_call(
        functools.partial(_rnn_core_kernel, n_ctx=n_ctx, n_all=n_all),
        out_shape=[jax.ShapeDtypeStruct((bsz, tt, rw), BF16)] * 2,
        grid=(bsz, n_all),
        in_specs=specs(fwd) + specs(bwd) + [_full_spec(c.shape) for c in consts],
        out_specs=[out_spec(fwd), out_spec(bwd)],
        scratch_shapes=[pltpu.VMEM((2, RNN_HEADS, 8, RNN_HD), F32)],
        compiler_params=_cparams("arbitrary", "arbitrary"),
        name="rglru_core",
    )(u, u, u, u, u, u, *consts)


def _residual(u, x, mod_ref, gp_ref, w_ref):
    y = _dot(u.astype(BF16), w_ref[...])
    return x + mod_ref[2:3] * (_rms(y) * gp_ref[...])


def _gla_post_kernel(*refs, n_ctx):
    (mod_ref, gp_ref, w_ref, of_ref, ob_ref, z_ref, gh_ref, o_ref) = refs[-8:]
    o = of_ref[...].astype(F32) + ob_ref[...].astype(F32)
    hv = o.shape[-1] // GLA_HEADS
    o = jnp.concatenate([_rms(o[:, h * hv:(h + 1) * hv]) for h in range(GLA_HEADS)], axis=1)
    u = o * gh_ref[...] * _silu(z_ref[...].astype(F32))
    o_ref[...] = _residual(u, _token_tile(refs[:-8], n_ctx), mod_ref, gp_ref, w_ref)


def _swa_post_kernel(x_ref, mod_ref, gp_ref, w_ref, a_ref, z_ref, o_ref, *, n_ctx):
    del n_ctx
    u = a_ref[...].astype(F32) * _silu(z_ref[...].astype(F32))
    o_ref[...] = _residual(u, x_ref[...], mod_ref, gp_ref, w_ref)


def _rnn_post_kernel(x_ref, mod_ref, gp_ref, w_ref, hf_ref, hb_ref, z_ref, o_ref, *, n_ctx):
    del n_ctx
    u = (hf_ref[...].astype(F32) + hb_ref[...].astype(F32)) * _silu(z_ref[...].astype(F32))
    o_ref[...] = _residual(u, x_ref[...], mod_ref, gp_ref, w_ref)


def _stream_shape(xs):
    if isinstance(xs, tuple):
        bsz, n_ctx_rows, d = xs[0].shape
        return bsz, n_ctx_rows + xs[1].shape[1], d
    return xs.shape


def _post_call(body, name, acts, vecs, xs, mod, g_post, w_out, n_ctx_rows, latent_only):
    bsz, tt, d = _stream_shape(xs)
    n_ctx = n_ctx_rows // ROW_TILE
    off = n_ctx if latent_only else 0
    n_tiles = tt // ROW_TILE - off
    streams = xs if isinstance(xs, tuple) else (xs,)
    return pl.pallas_call(
        functools.partial(body, n_ctx=n_ctx - off),
        out_shape=jax.ShapeDtypeStruct((bsz, n_tiles * ROW_TILE, d), F32),
        grid=(bsz, n_tiles),
        in_specs=_token_specs(xs, n_ctx, off)
        + [_mod_spec(d, n_ctx, bsz, off), _full_spec(g_post.shape), _full_spec(w_out.shape)]
        + [_row_spec(a.shape[-1], off) for a in acts]
        + [_full_spec(v.shape) for v in vecs],
        out_specs=_row_spec(d),
        compiler_params=_cparams("parallel", "parallel"),
        name=name,
    )(*streams, mod, g_post, w_out, *acts, *vecs)


def _pre_call(body, name, xs, mod, g_pre, consts, out_widths, out_dtypes, n_ctx_rows, row_consts=()):
    bsz, tt, d = _stream_shape(xs)
    n_ctx = n_ctx_rows // ROW_TILE
    streams = xs if isinstance(xs, tuple) else (xs,)
    return pl.pallas_call(
        functools.partial(body, n_ctx=n_ctx),
        out_shape=[jax.ShapeDtypeStruct((bsz, tt, w), dt) for w, dt in zip(out_widths, out_dtypes)],
        grid=(bsz, tt // ROW_TILE),
        in_specs=_token_specs(xs, n_ctx)
        + [_mod_spec(d, n_ctx, bsz), _full_spec(g_pre.shape)]
        + [_full_spec(c.shape) for c in consts]
        + [pl.BlockSpec((ROW_TILE, r.shape[-1]), lambda b, t: (t, 0)) for r in row_consts],
        out_specs=[_row_spec(w) for w in out_widths],
        compiler_params=_cparams("parallel", "parallel"),
        name=name,
    )(*streams, mod, g_pre, *consts, *row_consts)


def _layer_gla(xs, mod, g_pre, g_post, w_in, w_g1, w_g2, b_g, g_head, w_out, n_ctx_rows, last):
    d = w_in.shape[0]
    rank, dk = w_g2.shape[1], w_g2.shape[2]
    dv = w_out.shape[0]
    wg1 = jnp.zeros((d, GLA_GATE_PAD), F32).at[:, :2 * rank].set(jnp.concatenate([w_g1[0], w_g1[1]], axis=1))
    wg2 = jnp.zeros((GLA_GATE_PAD, 2 * dk), F32)
    wg2 = wg2.at[:rank, :dk].set(w_g2[0]).at[rank:2 * rank, dk:].set(w_g2[1])
    qk, v, z, la = _pre_call(
        functools.partial(_gla_pre_kernel, dk=dk), "gla_pre", xs, mod, g_pre,
        (w_in.astype(BF16), wg1.astype(BF16), wg2.astype(BF16), b_g.reshape(1, 2 * dk)),
        (2 * dk, dv, dv, 2 * dk), (BF16, BF16, BF16, F32), n_ctx_rows)
    o_f, o_b = _gla_core(qk, v, la, n_ctx_rows)
    return _post_call(_gla_post_kernel, "gla_post", (o_f, o_b, z), (g_head.reshape(1, dv),),
                      xs, mod, g_post, w_out.astype(BF16), n_ctx_rows, last)


def _layer_swa(xs, mod, g_pre, g_post, w_in, sink, w_out, rope, n_ctx_rows, last):
    qw = w_out.shape[0]
    kw = (w_in.shape[1] - 2 * qw) // 2
    q, kv, z = _pre_call(_swa_pre_kernel, "swa_pre", xs, mod, g_pre, (w_in.astype(BF16),),
                         (qw, 2 * kw, qw), (BF16, BF16, BF16), n_ctx_rows, row_consts=rope)
    a = _swa_core(q, kv, sink, n_ctx_rows)
    return _post_call(_swa_post_kernel, "swa_post", (a, z), (), xs, mod, g_post, w_out.astype(BF16),
                      n_ctx_rows, last)


def _layer_rnn(xs, mod, g_pre, g_post, w_in, conv_w, conv_b, w_ra, b_ra, w_ri, b_ri, lam, w_out,
               n_ctx_rows, last):
    rw = w_out.shape[0]
    u, z = _pre_call(_rnn_pre_kernel, "rglru_pre", xs, mod, g_pre, (w_in.astype(BF16),),
                     (rw, rw), (BF16, BF16), n_ctx_rows)
    wg = jnp.concatenate([w_ra, w_ri], axis=-1).astype(BF16)
    vec = lambda a: a.reshape(2, RNN_HEADS, 1, RNN_HD)
    h_f, h_b = _rnn_core(u, conv_w, conv_b.reshape(1, rw), wg, vec(b_ra), vec(b_ri), vec(lam), n_ctx_rows)
    return _post_call(_rnn_post_kernel, "rglru_post", (h_f, h_b, z), (), xs, mod, g_post,
                      w_out.astype(BF16), n_ctx_rows, last)


def _rope_tables(t, n_ctx_rows):
    pairs = ATT_HEAD_DIM // 4
    pos = jnp.arange(t, dtype=jnp.int32)
    row = (pos // GRID_W).astype(F32)
    col = (pos % GRID_W).astype(F32)
    freqs = ROPE_BASE ** (-jnp.arange(pairs, dtype=F32) / pairs)
    ang_r, ang_c = row[:, None] * freqs, col[:, None] * freqs
    cos = jnp.concatenate([jnp.cos(ang_r)] * 2 + [jnp.cos(ang_c)] * 2, axis=1)
    sin = jnp.concatenate([-jnp.sin(ang_r), jnp.sin(ang_r), -jnp.sin(ang_c), jnp.sin(ang_c)], axis=1)
    cos = jnp.concatenate([jnp.ones((n_ctx_rows, ATT_HEAD_DIM), F32), cos], axis=0)
    sin = jnp.concatenate([jnp.zeros((n_ctx_rows, ATT_HEAD_DIM), F32), sin], axis=0)
    return jnp.tile(cos, (1, 2)), jnp.tile(sin, (1, 2))


def kernel(x, c, ctx, c_ctx, w_mod, b_mod, g_pre, g_post, a_w_in, a_w_g1, a_w_g2, a_b_g, a_g_head, a_w_out, b_w_in, b_sink, b_w_out, c_w_in, c_conv_w, c_conv_b, c_w_ra, c_b_ra, c_w_ri, c_b_ri, c_lam, c_w_out):
    bsz, t, d = x.shape
    n_ctx_rows = ctx.shape[1]
    depth = w_mod.shape[0]
    assert bsz < MOD_ROWS and n_ctx_rows % ROW_TILE == 0 and t % ROW_TILE == 0

    cvec = jnp.zeros((MOD_ROWS, d), F32).at[:bsz].set(c).at[bsz].set(c_ctx)
    mods = _modulation(cvec, w_mod, b_mod)
    rope = _rope_tables(t, n_ctx_rows)
    xs = (ctx, x)
    for i in range(depth):
        kind, j = i % N_MIXERS, i // N_MIXERS
        last = i == depth - 1
        gpre, gpost = g_pre[i].reshape(1, d), g_post[i].reshape(1, d)
        if kind == 0:
            xs = _layer_gla(xs, mods[i], gpre, gpost, a_w_in[j], a_w_g1[j], a_w_g2[j], a_b_g[j],
                            a_g_head[j], a_w_out[j], n_ctx_rows, last)
        elif kind == 1:
            xs = _layer_swa(xs, mods[i], gpre, gpost, b_w_in[j], b_sink[j], b_w_out[j], rope,
                            n_ctx_rows, last)
        else:
            xs = _layer_rnn(xs, mods[i], gpre, gpost, c_w_in[j], c_conv_w[j], c_conv_b[j], c_w_ra[j],
                            c_b_ra[j], c_w_ri[j], c_b_ri[j], c_lam[j], c_w_out[j], n_ctx_rows, last)
    return xs
```

```python
import functools

import jax
import jax.numpy as jnp
from jax import lax
from jax.experimental import pallas as pl
from jax.experimental.pallas import tpu as pltpu

F32 = jnp.float32
BF16 = jnp.bfloat16

EPS = 1e-6
GRID_W = 64
N_MIXERS = 3

GLA_HEADS = 4
GLA_TAU = 16.0
GLA_GATE_PAD = 128

ATT_HEAD_DIM = 64
ATT_KV_HEADS = 4
ATT_GROUP = 4
ATT_BLOCK = 128
ATT_ROWS = 32
ROPE_BASE = 10000.0
LOG2_E = 1.4426950408889634

RNN_HEADS = 10
RNN_HD = 128
CONV_W = 4
LRU_C = 8.0

ROW_TILE = 256
GLA_CHUNK = 128
GLA_BLOCK = 64
RNN_TILE = 256
HALO = 16
MOD_ROWS = 8
VMEM_LIMIT = 48 * 1024 * 1024


def _cparams(*sem):
    return pltpu.CompilerParams(dimension_semantics=sem, vmem_limit_bytes=VMEM_LIMIT)


def _sigmoid(x):
    return 0.5 * jnp.tanh(0.5 * x) + 0.5


def _silu(x):
    return x * _sigmoid(x)


def _softplus(x):
    return jnp.maximum(x, 0.0) + jnp.log1p(jnp.exp(-jnp.abs(x)))


def _log_sigmoid(x):
    return jnp.minimum(x, 0.0) - jnp.log(1.0 + jnp.exp(-jnp.abs(x)))


def _split3(a):
    hi = a.astype(BF16)
    r = a - hi.astype(F32)
    mid = r.astype(BF16)
    lo = (r - mid.astype(F32)).astype(BF16)
    return hi, mid, lo


def _dot(a, b):
    return jnp.dot(a, b, preferred_element_type=F32)


def _dot_nt(a, b):
    return lax.dot_general(a, b, (((1,), (1,)), ((), ())), preferred_element_type=F32)


def _dot_tn(a, b):
    return lax.dot_general(a, b, (((0,), (0,)), ((), ())), preferred_element_type=F32)


def _rms(x):
    return x * lax.rsqrt(jnp.mean(x * x, axis=-1, keepdims=True) + EPS)


def _mod_kernel(c_ref, w_ref, b_ref, o_ref):
    s = _silu(c_ref[...])
    s_hi, s_mid, s_lo = _split3(s)
    w_hi, w_mid, w_lo = _split3(w_ref[...])
    acc = _dot(s_hi, w_hi)
    acc += _dot(s_hi, w_mid) + _dot(s_mid, w_hi)
    acc += _dot(s_hi, w_lo) + _dot(s_mid, w_mid) + _dot(s_lo, w_hi)
    o_ref[...] = acc + b_ref[...]


def _modulation(cvec, w_mod, b_mod):
    depth, d, d3 = w_mod.shape
    out = pl.pallas_call(
        _mod_kernel,
        out_shape=jax.ShapeDtypeStruct((depth, MOD_ROWS, d3), F32),
        grid=(depth, d3 // d),
        in_specs=[
            pl.BlockSpec((MOD_ROWS, d), lambda i, j: (0, 0)),
            pl.BlockSpec((None, d, d), lambda i, j: (i, 0, j)),
            pl.BlockSpec((None, 1, d), lambda i, j: (i, 0, j)),
        ],
        out_specs=pl.BlockSpec((None, MOD_ROWS, d), lambda i, j: (i, 0, j)),
        compiler_params=_cparams("parallel", "parallel"),
        name="modulation",
    )(cvec, w_mod, b_mod.reshape(depth, 1, d3))
    return out.reshape(depth, MOD_ROWS, 3, d)


def _prenorm(x, g, mod):
    return _rms(x) * g * (1.0 + mod[1:2]) + mod[0:1]


def _mod_spec(d, n_ctx_tiles, ctx_row, offset=0):
    def index(b, t):
        return (jnp.where(t + offset < n_ctx_tiles, ctx_row, b), 0, 0)
    return pl.BlockSpec((None, 3, d), index)


def _row_spec(width, offset=0):
    return pl.BlockSpec((None, ROW_TILE, width), lambda b, t: (b, t + offset, 0))


def _token_specs(xs, n_ctx, offset=0):
    if not isinstance(xs, tuple):
        return [_row_spec(xs.shape[-1], offset)]
    d = xs[0].shape[-1]
    return [pl.BlockSpec((None, ROW_TILE, d), lambda b, t: (b, jnp.minimum(t + offset, n_ctx - 1), 0)),
            pl.BlockSpec((None, ROW_TILE, d), lambda b, t: (b, jnp.maximum(t + offset - n_ctx, 0), 0))]


def _token_tile(x_refs, n_ctx_tiles):
    if len(x_refs) == 1:
        return x_refs[0][...]
    return jnp.where(pl.program_id(1) < n_ctx_tiles, x_refs[0][...], x_refs[1][...])


def _full_spec(shape):
    zeros = (0,) * len(shape)
    return pl.BlockSpec(shape, lambda b, t: zeros)


def _gla_pre_kernel(*refs, dk, n_ctx):
    (mod_ref, g_ref, w_ref, wg1_ref, wg2_ref, bg_ref, qk_ref, v_ref, z_ref, la_ref) = refs[-10:]
    h = _prenorm(_token_tile(refs[:-10], n_ctx), g_ref[...], mod_ref[...]).astype(BF16)
    hk = dk // GLA_HEADS
    g1 = _dot(h, wg1_ref[...]).astype(BF16)
    gate = _dot(g1, wg2_ref[...]) + bg_ref[...]
    la_ref[...] = _log_sigmoid(gate) / GLA_TAU
    q = _dot(h, w_ref[:, :dk]) * hk ** -0.5
    qk_ref[:, :dk] = q.astype(BF16)
    qk_ref[:, dk:] = _dot(h, w_ref[:, dk:2 * dk]).astype(BF16)
    dv = v_ref.shape[-1]
    v_ref[...] = _dot(h, w_ref[:, 2 * dk:2 * dk + dv]).astype(BF16)
    z_ref[...] = _dot(h, w_ref[:, 2 * dk + dv:]).astype(BF16)


def _gla_core_kernel(qkf_ref, vf_ref, laf_ref, qkb_ref, vb_ref, lab_ref,
                     of_ref, ob_ref, st_ref, *, dk):
    c = qkf_ref.shape[0]
    hk = dk // GLA_HEADS
    hv = vf_ref.shape[-1] // GLA_HEADS

    @pl.when(pl.program_id(1) == 0)
    def _():
        st_ref[...] = jnp.zeros_like(st_ref)

    row = lax.broadcasted_iota(jnp.int32, (c, c), 0)
    col = lax.broadcasted_iota(jnp.int32, (c, c), 1)
    blk = GLA_BLOCK
    io = ((qkf_ref, vf_ref, laf_ref, of_ref), (qkb_ref, vb_ref, lab_ref, ob_ref))
    keeps = (col <= row, col >= row)
    lasts = (c - 1, 0)
    cums = []
    for d, (_, _, la_ref, _) in enumerate(io):
        tri = jnp.where(keeps[d], 1.0, 0.0).astype(BF16)
        la_hi, la_mid, la_lo = _split3(la_ref[:, d * dk:(d + 1) * dk])
        cums.append(_dot(tri, la_hi) + _dot(tri, la_mid) + _dot(tri, la_lo))

    stage = {}
    for d, (qk_ref, v_ref, _, _) in enumerate(io):
        for h in range(GLA_HEADS):
            b = cums[d][:, h * hk:(h + 1) * hk]
            b_last = b[lasts[d]:lasts[d] + 1]
            q = qk_ref[:, h * hk:(h + 1) * hk].astype(F32)
            k = qk_ref[:, dk + h * hk:dk + (h + 1) * hk].astype(F32)
            v = v_ref[:, h * hv:(h + 1) * hv]
            scores = []
            for i in range(c // blk):
                rows = slice(i * blk, (i + 1) * blk)
                keys = slice(0, (i + 1) * blk) if d == 0 else slice(i * blk, c)
                b_ref = b[i * blk + blk // 2:i * blk + blk // 2 + 1]
                qe = (q[rows] * jnp.exp(b[rows] - b_ref)).astype(BF16)
                ke = (k[keys] * jnp.exp(b_ref - b[keys])).astype(BF16)
                scores.append((rows, keys, _dot_nt(qe, ke)))
            qs = (q * jnp.exp(b)).astype(BF16)
            kd = (k * jnp.exp(b_last - b)).astype(BF16)
            st = st_ref[d, h]
            inter = _dot_nt(qs, st.astype(BF16))
            st_ref[d, h] = st * jnp.exp(b_last) + _dot_tn(v, kd)
            stage[d, h] = (scores, inter, v)

    for d, (_, _, _, o_ref) in enumerate(io):
        for h in range(GLA_HEADS):
            scores, inter, v = stage[d, h]
            for rows, keys, s in scores:
                t_idx = lax.broadcasted_iota(jnp.int32, s.shape, 0) + rows.start
                s_idx = lax.broadcasted_iota(jnp.int32, s.shape, 1) + keys.start
                p = jnp.where((s_idx <= t_idx) if d == 0 else (s_idx >= t_idx), s, 0.0).astype(BF16)
                o = _dot(p, v[keys]) + inter[rows]
                o_ref[rows, h * hv:(h + 1) * hv] = o.astype(BF16)


def _bwd_order(n, n_ctx, n_all):
    return jnp.where(n < n_ctx, n_ctx - 1 - n, n_all - 1 - (n - n_ctx))


def _gla_core(qk, v, la, n_ctx_rows):
    bsz, tt, dv = v.shape
    dk = qk.shape[-1] // 2
    c = GLA_CHUNK
    n_all, n_ctx = tt // c, n_ctx_rows // c
    fwd = lambda b, n: (b, n, 0)
    bwd = lambda b, n: (b, _bwd_order(n, n_ctx, n_all), 0)
    spec = lambda w, idx: pl.BlockSpec((None, c, w), idx)
    return pl.pallas_call(
        functools.partial(_gla_core_kernel, dk=dk),
        out_shape=[jax.ShapeDtypeStruct((bsz, tt, dv), BF16)] * 2,
        grid=(bsz, n_all),
        in_specs=[spec(2 * dk, fwd), spec(dv, fwd), spec(2 * dk, fwd),
                  spec(2 * dk, bwd), spec(dv, bwd), spec(2 * dk, bwd)],
        out_specs=[spec(dv, fwd), spec(dv, bwd)],
        scratch_shapes=[pltpu.VMEM((2, GLA_HEADS, dv // GLA_HEADS, dk // GLA_HEADS), F32)],
        compiler_params=_cparams("arbitrary", "arbitrary"),
        name="gla_core",
    )(qk, v, la, qk, v, la)


def _rope(x, cos, sin_signed):
    lane = lax.broadcasted_iota(jnp.int32, (x.shape[0], 128), 1)
    first = (lane % 32) < 16
    out = []
    for j in range(x.shape[1] // 128):
        xj = x[:, j * 128:(j + 1) * 128]
        partner = jnp.where(first, pltpu.roll(xj, 112, 1), pltpu.roll(xj, 16, 1))
        out.append(xj * cos + partner * sin_signed)
    return jnp.concatenate(out, axis=1)


def _swa_pre_kernel(*refs, n_ctx):
    (mod_ref, g_ref, w_ref, cos_ref, sin_ref, q_ref, kv_ref, z_ref) = refs[-8:]
    h = _prenorm(_token_tile(refs[:-8], n_ctx), g_ref[...], mod_ref[...]).astype(BF16)
    qw = q_ref.shape[-1]
    kw = kv_ref.shape[-1] // 2
    cos, sin = cos_ref[...], sin_ref[...]
    q = _rope(_dot(h, w_ref[:, :qw]), cos, sin) * (ATT_HEAD_DIM ** -0.5 * LOG2_E)
    q_ref[...] = q.astype(BF16)
    kv_ref[:, :kw] = _rope(_dot(h, w_ref[:, qw:qw + kw]), cos, sin).astype(BF16)
    kv_ref[:, kw:] = _dot(h, w_ref[:, qw + kw:qw + 2 * kw]).astype(BF16)
    z_ref[...] = _dot(h, w_ref[:, qw + 2 * kw:]).astype(BF16)


def _swa_core_kernel(sink_ref, q_ref, kvp_ref, kvc_ref, kvn_ref, kvx_ref, o_ref, *, n_ctx, n_all):
    t = pl.program_id(1)
    blk = ATT_BLOCK
    n_keys = 3 * blk + kvx_ref.shape[0]
    kw = kvc_ref.shape[-1] // 2
    n_loc = 3 * blk
    i = lax.broadcasted_iota(jnp.int32, (blk, n_loc), 0)
    j = lax.broadcasted_iota(jnp.int32, (blk, n_loc), 1)
    never = n_keys
    off_prev = jnp.where(t > n_ctx, 0, never)
    off_cur = jnp.where(t >= n_ctx, 0, never)
    off_next = jnp.where(jnp.logical_and(t >= n_ctx, t < n_all - 1), 0, never)
    valid = (((j >= i + off_prev) & (j < blk))
             | ((j >= blk + off_cur) & (j < 2 * blk))
             | ((j >= 2 * blk) & (j <= i + 2 * blk - off_next)))
    bias = jnp.where(valid, 0.0, -jnp.inf)
    logits = []
    for h in range(ATT_KV_HEADS):
        ks = slice(h * ATT_HEAD_DIM, (h + 1) * ATT_HEAD_DIM)
        k = jnp.concatenate([kvp_ref[:, ks], kvc_ref[:, ks], kvn_ref[:, ks], kvx_ref[:, ks]], axis=0)
        q = jnp.concatenate(
            [q_ref[:, (h * ATT_GROUP + g) * ATT_HEAD_DIM:(h * ATT_GROUP + g + 1) * ATT_HEAD_DIM]
             for g in range(ATT_GROUP)], axis=0)
        logits.append(_dot_nt(q, k))
    chunks = [(h, r0) for h in range(ATT_KV_HEADS) for r0 in range(0, ATT_GROUP * blk, ATT_ROWS)]

    def slabs(h, r0):
        out = []
        for c0 in range(0, n_keys, 128):
            s = logits[h][r0:r0 + ATT_ROWS, c0:c0 + 128]
            if c0 < n_loc:
                s = s + bias[r0 % blk:r0 % blk + ATT_ROWS, c0:c0 + 128]
            out.append(s)
        return out

    sinks = {(h, r0): sink_ref[h * ATT_GROUP + r0 // blk] * LOG2_E for h, r0 in chunks}
    maxes = {}
    for h, r0 in chunks:
        lane_max = functools.reduce(jnp.maximum, slabs(h, r0))
        maxes[h, r0] = jnp.maximum(jnp.max(lane_max, axis=-1, keepdims=True), sinks[h, r0])
    for h in range(ATT_KV_HEADS):
        vs = slice(kw + h * ATT_HEAD_DIM, kw + (h + 1) * ATT_HEAD_DIM)
        v = jnp.concatenate([kvp_ref[:, vs], kvc_ref[:, vs], kvn_ref[:, vs], kvx_ref[:, vs]], axis=0)
        probs, lane_sums = [], []
        for _, r0 in chunks[:len(chunks) // ATT_KV_HEADS]:
            p = [jnp.exp2(s - maxes[h, r0]) for s in slabs(h, r0)]
            probs.append(jnp.concatenate(p, axis=1).astype(BF16))
            lane_sums.append(functools.reduce(jnp.add, p))
        pv = _dot(jnp.concatenate(probs, axis=0), v)
        denom = jnp.concatenate(
            [jnp.sum(ls, axis=-1, keepdims=True) + jnp.exp2(sinks[h, r0] - maxes[h, r0])
             for ls, (_, r0) in zip(lane_sums, chunks)], axis=0)
        o = pv * (1.0 / denom)
        for g in range(ATT_GROUP):
            lo = (h * ATT_GROUP + g) * ATT_HEAD_DIM
            o_ref[:, lo:lo + ATT_HEAD_DIM] = o[g * blk:(g + 1) * blk].astype(BF16)


def _swa_core(q, kv, sink, n_ctx_rows):
    bsz, tt, qw = q.shape
    blk = ATT_BLOCK
    n_all, n_ctx = tt // blk, n_ctx_rows // blk
    kvw = kv.shape[-1]
    kv_spec = lambda idx: pl.BlockSpec((None, blk, kvw), idx)
    return pl.pallas_call(
        functools.partial(_swa_core_kernel, n_ctx=n_ctx, n_all=n_all),
        out_shape=jax.ShapeDtypeStruct((bsz, tt, qw), BF16),
        grid=(bsz, n_all),
        in_specs=[
            pl.BlockSpec(memory_space=pltpu.SMEM),
            pl.BlockSpec((None, blk, qw), lambda b, t: (b, t, 0)),
            kv_spec(lambda b, t: (b, jnp.maximum(t - 1, 0), 0)),
            kv_spec(lambda b, t: (b, t, 0)),
            kv_spec(lambda b, t: (b, jnp.minimum(t + 1, n_all - 1), 0)),
            pl.BlockSpec((None, n_ctx_rows, kvw), lambda b, t: (b, 0, 0)),
        ],
        out_specs=pl.BlockSpec((None, blk, qw), lambda b, t: (b, t, 0)),
        compiler_params=_cparams("parallel", "parallel"),
        name="swa_core",
    )(sink, q, kv, kv, kv, kv)


def _segment_permutation(n):
    r = jnp.arange(n)
    p = (r[None, :] == ((r % 8) * (n // 8) + r // 8)[:, None]).astype(BF16)
    return p, p.T


def _rnn_pre_kernel(x_ref, mod_ref, g_ref, w_ref, u_ref, z_ref, *, n_ctx):
    del n_ctx
    h = _prenorm(x_ref[...], g_ref[...], mod_ref[...]).astype(BF16)
    rw = u_ref.shape[-1]
    u_ref[...] = _dot(h, w_ref[:, :rw]).astype(BF16)
    z_ref[...] = _dot(h, w_ref[:, rw:]).astype(BF16)


def _shift_rows(cur, other, up):
    sub = lax.broadcasted_iota(jnp.int32, cur.shape, 0)
    if up:
        return jnp.where(sub == 7, pltpu.roll(other, 7, 0), pltpu.roll(cur, 7, 0))
    return jnp.where(sub == 0, pltpu.roll(other, 1, 0), pltpu.roll(cur, 1, 0))


def _segment_scan(a, x, carry, reverse):
    n_g = a.shape[0] // 8
    order = range(n_g - 1, -1, -1) if reverse else range(n_g)
    hs, ps = [None] * n_g, [None] * n_g
    h = p = None
    for g in order:
        ag, xg = a[8 * g:8 * g + 8], x[8 * g:8 * g + 8]
        h = xg if h is None else ag * h + xg
        p = ag if p is None else ag * p
        hs[g], ps[g] = h, p
    sub = lax.broadcasted_iota(jnp.int32, h.shape, 0)
    for s in (1, 2, 4):
        shift = 8 - s if reverse else s
        inside = (sub < 8 - s) if reverse else (sub >= s)
        h = h + p * jnp.where(inside, pltpu.roll(h, shift, 0), 0.0)
        p = p * jnp.where(inside, pltpu.roll(p, shift, 0), 1.0)
    full = h + p * carry
    first, last = (7, 0) if reverse else (0, 7)
    seg_in = jnp.where(sub == first, carry, pltpu.roll(full, 7 if reverse else 1, 0))
    out = jnp.concatenate([hs[g] + ps[g] * seg_in for g in range(n_g)], axis=0)
    return out, jnp.broadcast_to(full[last:last + 1], full.shape)


def _rnn_core_kernel(upf_ref, ucf_ref, unf_ref, upb_ref, ucb_ref, unb_ref,
                     cw_ref, cb_ref, wg_ref, ba_ref, bx_ref, lam_ref,
                     perm_ref, unperm_ref, hf_ref, hb_ref, carry_ref, *, n_ctx, n_all):
    n = pl.program_id(1)

    @pl.when(n == 0)
    def _():
        carry_ref[...] = jnp.zeros_like(carry_ref)

    tiles = (n, _bwd_order(n, n_ctx, n_all))
    prev_ok = [jnp.where(jnp.logical_and(t != 0, t != n_ctx), 1.0, 0.0) for t in tiles]
    next_ok = [jnp.where(jnp.logical_and(t != n_ctx - 1, t != n_all - 1), 1.0, 0.0) for t in tiles]
    refs = ((upf_ref, ucf_ref, unf_ref, hf_ref), (upb_ref, ucb_ref, unb_ref, hb_ref))

    useg = [_dot(perm_ref[...], refs[d][1][...]) for d in range(2)]
    rows = useg[0].shape[0]
    convs, pres = {}, {}
    for d, (p_ref, _, n_ref, _) in enumerate(refs):
        before = p_ref[...].astype(F32)[HALO - 8:] * prev_ok[d]
        after = n_ref[...].astype(F32)[:8] * next_ok[d]
        before2 = pltpu.roll(before, 1, 0)
        for j in range(RNN_HEADS):
            lanes = slice(j * RNN_HD, (j + 1) * RNN_HD)
            u = useg[d][:, lanes]
            ext = jnp.concatenate([
                _shift_rows(u[rows - 16:rows - 8], before2[:, lanes], up=False),
                _shift_rows(u[rows - 8:], before[:, lanes], up=False),
                u,
                _shift_rows(u[:8], after[:, lanes], up=True)], axis=0)
            conv = cb_ref[:, lanes]
            for tap in range(CONV_W):
                conv = conv + ext[8 * tap:8 * tap + rows] * cw_ref[tap:tap + 1, lanes]
            convs[d, j] = conv
            pres[d, j] = _dot(conv.astype(BF16), wg_ref[d, j])

    half = RNN_HEADS // 2
    for j0 in (0, half):
        for d in range(2):
            outs = []
            for j in range(j0, j0 + half):
                pre, conv = pres[d, j], convs[d, j]
                r = _sigmoid(pre[:, :RNN_HD] + ba_ref[d, j])
                i = _sigmoid(pre[:, RNN_HD:] + bx_ref[d, j])
                log_a = -LRU_C * r * _softplus(-lam_ref[d, j])
                a = jnp.exp(log_a)
                y = -jnp.tanh(log_a) * (1.0 + a * a)
                x = jnp.where(y > 0.0, y * lax.rsqrt(y), 0.0) * i * conv
                hs, carry = _segment_scan(a, x, carry_ref[d, j], reverse=(d == 1))
                carry_ref[d, j] = carry
                outs.append(hs.astype(BF16))
            h = _dot(unperm_ref[...], jnp.concatenate(outs, axis=1))
            refs[d][3][:, j0 * RNN_HD:(j0 + half) * RNN_HD] = h.astype(BF16)


def _rnn_core(u, conv_w, conv_b, wg, b_a, b_x, lam, n_ctx_rows):
    bsz, tt, rw = u.shape
    tile = RNN_TILE
    n_all, n_ctx = tt // tile, n_ctx_rows // tile
    per = tile // HALO
    n_halo = tt // HALO
    fwd = lambda n: n
    bwd = lambda n: _bwd_order(n, n_ctx, n_all)

    def specs(order):
        return [
            pl.BlockSpec((None, HALO, rw), lambda b, n: (b, jnp.maximum(order(n) * per - 1, 0), 0)),
            pl.BlockSpec((None, tile, rw), lambda b, n: (b, order(n), 0)),
            pl.BlockSpec((None, HALO, rw), lambda b, n: (b, jnp.minimum((order(n) + 1) * per, n_halo - 1), 0)),
        ]

    out_spec = lambda order: pl.BlockSpec((None, tile, rw), lambda b, n: (b, order(n), 0))
    perm, unperm = _segment_permutation(tile)
    consts = (conv_w, conv_b, wg, b_a, b_x, lam, perm, unperm)
    return pl.pallas_call(
        functools.partial(_rnn_core_kernel, n_ctx=n_ctx, n_all=n_all),
        out_shape=[jax.ShapeDtypeStruct((bsz, tt, rw), BF16)] * 2,
        grid=(bsz, n_all),
        in_specs=specs(fwd) + specs(bwd) + [_full_spec(c.shape) for c in consts],
        out_specs=[out_spec(fwd), out_spec(bwd)],
        scratch_shapes=[pltpu.VMEM((2, RNN_HEADS, 8, RNN_HD), F32)],
        compiler_params=_cparams("arbitrary", "arbitrary"),
        name="rglru_core",
    )(u, u, u, u, u, u, *consts)


def _residual(u, x, mod_ref, gp_ref, w_ref):
    y = _dot(u.astype(BF16), w_ref[...])
    return x + mod_ref[2:3] * (_rms(y) * gp_ref[...])


def _gla_post_kernel(*refs, n_ctx):
    (mod_ref, gp_ref, w_ref, of_ref, ob_ref, z_ref, gh_ref, o_ref) = refs[-8:]
    o = of_ref[...].astype(F32) + ob_ref[...].astype(F32)
    hv = o.shape[-1] // GLA_HEADS
    o = jnp.concatenate([_rms(o[:, h * hv:(h + 1) * hv]) for h in range(GLA_HEADS)], axis=1)
    u = o * gh_ref[...] * _silu(z_ref[...].astype(F32))
    o_ref[...] = _residual(u, _token_tile(refs[:-8], n_ctx), mod_ref, gp_ref, w_ref)


def _swa_post_kernel(x_ref, mod_ref, gp_ref, w_ref, a_ref, z_ref, o_ref, *, n_ctx):
    del n_ctx
    u = a_ref[...].astype(F32) * _silu(z_ref[...].astype(F32))
    o_ref[...] = _residual(u, x_ref[...], mod_ref, gp_ref, w_ref)


def _rnn_post_kernel(x_ref, mod_ref, gp_ref, w_ref, hf_ref, hb_ref, z_ref, o_ref, *, n_ctx):
    del n_ctx
    u = (hf_ref[...].astype(F32) + hb_ref[...].astype(F32)) * _silu(z_ref[...].astype(F32))
    o_ref[...] = _residual(u, x_ref[...], mod_ref, gp_ref, w_ref)


def _stream_shape(xs):
    if isinstance(xs, tuple):
        bsz, n_ctx_rows, d = xs[0].shape
        return bsz, n_ctx_rows + xs[1].shape[1], d
    return xs.shape


def _post_call(body, name, acts, vecs, xs, mod, g_post, w_out, n_ctx_rows, latent_only):
    bsz, tt, d = _stream_shape(xs)
    n_ctx = n_ctx_rows // ROW_TILE
    off = n_ctx if latent_only else 0
    n_tiles = tt // ROW_TILE - off
    streams = xs if isinstance(xs, tuple) else (xs,)
    return pl.pallas_call(
        functools.partial(body, n_ctx=n_ctx - off),
        out_shape=jax.ShapeDtypeStruct((bsz, n_tiles * ROW_TILE, d), F32),
        grid=(bsz, n_tiles),
        in_specs=_token_specs(xs, n_ctx, off)
        + [_mod_spec(d, n_ctx, bsz, off), _full_spec(g_post.shape), _full_spec(w_out.shape)]
        + [_row_spec(a.shape[-1], off) for a in acts]
        + [_full_spec(v.shape) for v in vecs],
        out_specs=_row_spec(d),
        compiler_params=_cparams("parallel", "parallel"),
        name=name,
    )(*streams, mod, g_post, w_out, *acts, *vecs)


def _pre_call(body, name, xs, mod, g_pre, consts, out_widths, out_dtypes, n_ctx_rows, row_consts=()):
    bsz, tt, d = _stream_shape(xs)
    n_ctx = n_ctx_rows // ROW_TILE
    streams = xs if isinstance(xs, tuple) else (xs,)
    return pl.pallas_call(
        functools.partial(body, n_ctx=n_ctx),
        out_shape=[jax.ShapeDtypeStruct((bsz, tt, w), dt) for w, dt in zip(out_widths, out_dtypes)],
        grid=(bsz, tt // ROW_TILE),
        in_specs=_token_specs(xs, n_ctx)
        + [_mod_spec(d, n_ctx, bsz), _full_spec(g_pre.shape)]
        + [_full_spec(c.shape) for c in consts]
        + [pl.BlockSpec((ROW_TILE, r.shape[-1]), lambda b, t: (t, 0)) for r in row_consts],
        out_specs=[_row_spec(w) for w in out_widths],
        compiler_params=_cparams("parallel", "parallel"),
        name=name,
    )(*streams, mod, g_pre, *consts, *row_consts)


def _layer_gla(xs, mod, g_pre, g_post, w_in, w_g1, w_g2, b_g, g_head, w_out, n_ctx_rows, last):
    d = w_in.shape[0]
    rank, dk = w_g2.shape[1], w_g2.shape[2]
    dv = w_out.shape[0]
    wg1 = jnp.zeros((d, GLA_GATE_PAD), F32).at[:, :2 * rank].set(jnp.concatenate([w_g1[0], w_g1[1]], axis=1))
    wg2 = jnp.zeros((GLA_GATE_PAD, 2 * dk), F32)
    wg2 = wg2.at[:rank, :dk].set(w_g2[0]).at[rank:2 * rank, dk:].set(w_g2[1])
    qk, v, z, la = _pre_call(
        functools.partial(_gla_pre_kernel, dk=dk), "gla_pre", xs, mod, g_pre,
        (w_in.astype(BF16), wg1.astype(BF16), wg2.astype(BF16), b_g.reshape(1, 2 * dk)),
        (2 * dk, dv, dv, 2 * dk), (BF16, BF16, BF16, F32), n_ctx_rows)
    o_f, o_b = _gla_core(qk, v, la, n_ctx_rows)
    return _post_call(_gla_post_kernel, "gla_post", (o_f, o_b, z), (g_head.reshape(1, dv),),
                      xs, mod, g_post, w_out.astype(BF16), n_ctx_rows, last)


def _layer_swa(xs, mod, g_pre, g_post, w_in, sink, w_out, rope, n_ctx_rows, last):
    qw = w_out.shape[0]
    kw = (w_in.shape[1] - 2 * qw) // 2
    q, kv, z = _pre_call(_swa_pre_kernel, "swa_pre", xs, mod, g_pre, (w_in.astype(BF16),),
                         (qw, 2 * kw, qw), (BF16, BF16, BF16), n_ctx_rows, row_consts=rope)
    a = _swa_core(q, kv, sink, n_ctx_rows)
    return _post_call(_swa_post_kernel, "swa_post", (a, z), (), xs, mod, g_post, w_out.astype(BF16),
                      n_ctx_rows, last)


def _layer_rnn(xs, mod, g_pre, g_post, w_in, conv_w, conv_b, w_ra, b_ra, w_ri, b_ri, lam, w_out,
               n_ctx_rows, last):
    rw = w_out.shape[0]
    u, z = _pre_call(_rnn_pre_kernel, "rglru_pre", xs, mod, g_pre, (w_in.astype(BF16),),
                     (rw, rw), (BF16, BF16), n_ctx_rows)
    wg = jnp.concatenate([w_ra, w_ri], axis=-1).astype(BF16)
    vec = lambda a: a.reshape(2, RNN_HEADS, 1, RNN_HD)
    h_f, h_b = _rnn_core(u, conv_w, conv_b.reshape(1, rw), wg, vec(b_ra), vec(b_ri), vec(lam), n_ctx_rows)
    return _post_call(_rnn_post_kernel, "rglru_post", (h_f, h_b, z), (), xs, mod, g_post,
                      w_out.astype(BF16), n_ctx_rows, last)


def _rope_tables(t, n_ctx_rows):
    pairs = ATT_HEAD_DIM // 4
    pos = jnp.arange(t, dtype=jnp.int32)
    row = (pos // GRID_W).astype(F32)
    col = (pos % GRID_W).astype(F32)
    freqs = ROPE_BASE ** (-jnp.arange(pairs, dtype=F32) / pairs)
    ang_r, ang_c = row[:, None] * freqs, col[:, None] * freqs
    cos = jnp.concatenate([jnp.cos(ang_r)] * 2 + [jnp.cos(ang_c)] * 2, axis=1)
    sin = jnp.concatenate([-jnp.sin(ang_r), jnp.sin(ang_r), -jnp.sin(ang_c), jnp.sin(ang_c)], axis=1)
    cos = jnp.concatenate([jnp.ones((n_ctx_rows, ATT_HEAD_DIM), F32), cos], axis=0)
    sin = jnp.concatenate([jnp.zeros((n_ctx_rows, ATT_HEAD_DIM), F32), sin], axis=0)
    return jnp.tile(cos, (1, 2)), jnp.tile(sin, (1, 2))


def kernel(x, c, ctx, c_ctx, w_mod, b_mod, g_pre, g_post, a_w_in, a_w_g1, a_w_g2, a_b_g, a_g_head, a_w_out, b_w_in, b_sink, b_w_out, c_w_in, c_conv_w, c_conv_b, c_w_ra, c_b_ra, c_w_ri, c_b_ri, c_lam, c_w_out):
    bsz, t, d = x.shape
    n_ctx_rows = ctx.shape[1]
    depth = w_mod.shape[0]
    assert bsz < MOD_ROWS and n_ctx_rows % ROW_TILE == 0 and t % ROW_TILE == 0

    cvec = jnp.zeros((MOD_ROWS, d), F32).at[:bsz].set(c).at[bsz].set(c_ctx)
    mods = _modulation(cvec, w_mod, b_mod)
    rope = _rope_tables(t, n_ctx_rows)
    xs = (ctx, x)
    for i in range(depth):
        kind, j = i % N_MIXERS, i // N_MIXERS
        last = i == depth - 1
        gpre, gpost = g_pre[i].reshape(1, d), g_post[i].reshape(1, d)
        if kind == 0:
            xs = _layer_gla(xs, mods[i], gpre, gpost, a_w_in[j], a_w_g1[j], a_w_g2[j], a_b_g[j],
                            a_g_head[j], a_w_out[j], n_ctx_rows, last)
        elif kind == 1:
            xs = _layer_swa(xs, mods[i], gpre, gpost, b_w_in[j], b_sink[j], b_w_out[j], rope,
                            n_ctx_rows, last)
        else:
            xs = _layer_rnn(xs, mods[i], gpre, gpost, c_w_in[j], c_conv_w[j], c_conv_b[j], c_w_ra[j],
                            c_b_ra[j], c_w_ri[j], c_b_ri[j], c_lam[j], c_w_out[j], n_ctx_rows, last)
    return xs
```

```python
import functools

import jax
import jax.numpy as jnp
from jax import lax
from jax.experimental import pallas as pl
from jax.experimental.pallas import tpu as pltpu

F32 = jnp.float32
BF16 = jnp.bfloat16

EPS = 1e-6
GRID_W = 64
N_MIXERS = 3

GLA_HEADS = 4
GLA_TAU = 16.0
GLA_GATE_PAD = 128

ATT_HEAD_DIM = 64
ATT_KV_HEADS = 4
ATT_GROUP = 4
ATT_BLOCK = 128
ATT_ROWS = 32
ROPE_BASE = 10000.0
LOG2_E = 1.4426950408889634

RNN_HEADS = 10
RNN_HD = 128
CONV_W = 4
LRU_C = 8.0

ROW_TILE = 256
GLA_TILE = 256
GLA_CHUNK = 128
GLA_BLOCK = 64
GLA_SAFE_EXPONENT = 60.0
RNN_TILE = 256
HALO = 16
MOD_ROWS = 8
VMEM_LIMIT = 48 * 1024 * 1024


def _cparams(*sem):
    return pltpu.CompilerParams(dimension_semantics=sem, vmem_limit_bytes=VMEM_LIMIT)


def _sigmoid(x):
    return 0.5 * jnp.tanh(0.5 * x) + 0.5


def _silu(x):
    return x * _sigmoid(x)


def _softplus(x):
    return jnp.maximum(x, 0.0) + jnp.log1p(jnp.exp(-jnp.abs(x)))


def _log_sigmoid(x):
    return jnp.minimum(x, 0.0) - jnp.log(1.0 + jnp.exp(-jnp.abs(x)))


def _split3(a):
    hi = a.astype(BF16)
    r = a - hi.astype(F32)
    mid = r.astype(BF16)
    lo = (r - mid.astype(F32)).astype(BF16)
    return hi, mid, lo


def _dot(a, b):
    return jnp.dot(a, b, preferred_element_type=F32)


def _dot_nt(a, b):
    return lax.dot_general(a, b, (((1,), (1,)), ((), ())), preferred_element_type=F32)


def _dot_tn(a, b):
    return lax.dot_general(a, b, (((0,), (0,)), ((), ())), preferred_element_type=F32)


def _rms(x):
    return x * lax.rsqrt(jnp.mean(x * x, axis=-1, keepdims=True) + EPS)


def _mod_kernel(c_ref, w_ref, b_ref, o_ref):
    s = _silu(c_ref[...])
    s_hi, s_mid, s_lo = _split3(s)
    w_hi, w_mid, w_lo = _split3(w_ref[...])
    acc = _dot(s_hi, w_hi)
    acc += _dot(s_hi, w_mid) + _dot(s_mid, w_hi)
    acc += _dot(s_hi, w_lo) + _dot(s_mid, w_mid) + _dot(s_lo, w_hi)
    o_ref[...] = acc + b_ref[...]


def _modulation(cvec, w_mod, b_mod):
    depth, d, d3 = w_mod.shape
    out = pl.pallas_call(
        _mod_kernel,
        out_shape=jax.ShapeDtypeStruct((depth, MOD_ROWS, d3), F32),
        grid=(depth, d3 // d),
        in_specs=[
            pl.BlockSpec((MOD_ROWS, d), lambda i, j: (0, 0)),
            pl.BlockSpec((None, d, d), lambda i, j: (i, 0, j)),
            pl.BlockSpec((None, 1, d), lambda i, j: (i, 0, j)),
        ],
        out_specs=pl.BlockSpec((None, MOD_ROWS, d), lambda i, j: (i, 0, j)),
        compiler_params=_cparams("parallel", "parallel"),
        name="modulation",
    )(cvec, w_mod, b_mod.reshape(depth, 1, d3))
    return out.reshape(depth, MOD_ROWS, 3, d)


def _prenorm(x, g, mod):
    return _rms(x) * g * (1.0 + mod[1:2]) + mod[0:1]


def _mod_spec(d, n_ctx_tiles, ctx_row, offset=0):
    def index(b, t):
        return (jnp.where(t + offset < n_ctx_tiles, ctx_row, b), 0, 0)
    return pl.BlockSpec((None, 3, d), index)


def _row_spec(width, offset=0):
    return pl.BlockSpec((None, ROW_TILE, width), lambda b, t: (b, t + offset, 0))


def _token_specs(xs, n_ctx, offset=0):
    if not isinstance(xs, tuple):
        return [_row_spec(xs.shape[-1], offset)]
    d = xs[0].shape[-1]
    return [pl.BlockSpec((None, ROW_TILE, d), lambda b, t: (b, jnp.minimum(t + offset, n_ctx - 1), 0)),
            pl.BlockSpec((None, ROW_TILE, d), lambda b, t: (b, jnp.maximum(t + offset - n_ctx, 0), 0))]


def _token_tile(x_refs, n_ctx_tiles):
    if len(x_refs) == 1:
        return x_refs[0][...]
    return jnp.where(pl.program_id(1) < n_ctx_tiles, x_refs[0][...], x_refs[1][...])


def _full_spec(shape):
    zeros = (0,) * len(shape)
    return pl.BlockSpec(shape, lambda b, t: zeros)


def _gla_pre_kernel(*refs, dk, n_ctx):
    (mod_ref, g_ref, w_ref, wg1_ref, wg2_ref, bg_ref, qk_ref, v_ref, z_ref, la_ref, peak_ref) = refs[-11:]
    h = _prenorm(_token_tile(refs[:-11], n_ctx), g_ref[...], mod_ref[...]).astype(BF16)
    hk = dk // GLA_HEADS
    dv = v_ref.shape[-1]
    g1 = _dot(h, wg1_ref[...]).astype(BF16)
    gate = _dot(g1, wg2_ref[...]) + bg_ref[...]
    la = _log_sigmoid(gate) / GLA_TAU
    la_ref[...] = la
    peak = jnp.max(jnp.max(-la, axis=1, keepdims=True), axis=0, keepdims=True)
    peak_ref[...] = jnp.broadcast_to(peak, peak_ref.shape)
    q = _dot(h, w_ref[:, :dk]) * hk ** -0.5
    qk_ref[:, :dk] = q.astype(BF16)
    qk_ref[:, dk:] = _dot(h, w_ref[:, dk:2 * dk]).astype(BF16)
    v_ref[...] = _dot(h, w_ref[:, 2 * dk:2 * dk + dv]).astype(BF16)
    z_ref[...] = _dot(h, w_ref[:, 2 * dk + dv:]).astype(BF16)


def _gla_core_kernel(exact_ref, qkf_ref, vf_ref, laf_ref, qkb_ref, vb_ref, lab_ref,
                     of_ref, ob_ref, st_ref, row_ref, *, dk, n_ctx, n_all):
    b, n = pl.program_id(0), pl.program_id(1)

    @pl.when(n == 0)
    def _():
        st_ref[...] = jnp.zeros_like(st_ref)

    io = ((qkf_ref, vf_ref, laf_ref, of_ref), (qkb_ref, vb_ref, lab_ref, ob_ref))
    n_sub = qkf_ref.shape[0] // GLA_CHUNK

    def run(exact):
        chunks = [_gla_chunk_pair(io, (s * GLA_CHUNK, (n_sub - 1 - s) * GLA_CHUNK), st_ref, dk,
                                  row_ref if exact else None) for s in range(n_sub)]
        for _ in range(3):
            for chunk in chunks:
                next(chunk, None)

    out_of_range = exact_ref[b, n] + exact_ref[b, _bwd_order(n, n_ctx, n_all)]
    pl.when(out_of_range == 0)(lambda: run(False))
    pl.when(out_of_range != 0)(lambda: run(True))


def _gla_exact_intra(d, q, k, v, b, row_ref):
    c, hk = q.shape
    row_ref[:, :hk] = k
    row_ref[:, hk:2 * hk] = b
    row_ref[:, 2 * hk:] = v.astype(F32)
    t_idx = lax.broadcasted_iota(jnp.int32, (c, 1), 0)

    def keys(g, acc):
        group = row_ref[pl.ds(pl.multiple_of(g * 8, 8), 8), :]
        for r in range(8):
            s = g * 8 + r
            k_s, b_s, v_s = group[r:r + 1, :hk], group[r:r + 1, hk:2 * hk], group[r:r + 1, 2 * hk:]
            decay = jnp.exp(jnp.minimum(b - b_s, 0.0))
            col = jnp.sum(q * decay * k_s, axis=1, keepdims=True)
            causal = (t_idx >= s) if d == 0 else (t_idx <= s)
            acc = acc + jnp.where(causal, col, 0.0) * v_s
        return acc

    return lax.fori_loop(0, c // 8, keys, jnp.zeros((c, v.shape[1]), F32))


def _gla_chunk_pair(io, base, st_ref, dk, row_ref):
    c = GLA_CHUNK
    hk = dk // GLA_HEADS
    hv = io[0][1].shape[-1] // GLA_HEADS
    row = lax.broadcasted_iota(jnp.int32, (c, c), 0)
    col = lax.broadcasted_iota(jnp.int32, (c, c), 1)
    blk = GLA_BLOCK
    keeps = (col <= row, col >= row)
    lasts = (c - 1, 0)
    tile = [slice(r0, r0 + c) for r0 in base]
    cums = []
    for d, (_, _, la_ref, _) in enumerate(io):
        tri = jnp.where(keeps[d], 1.0, 0.0).astype(BF16)
        la_hi, la_mid, _ = _split3(la_ref[tile[d], d * dk:(d + 1) * dk])
        cums.append(_dot(tri, la_hi) + _dot(tri, la_mid))
    yield

    stage = {}
    for d, (qk_ref, v_ref, _, _) in enumerate(io):
        for h in range(GLA_HEADS):
            b = cums[d][:, h * hk:(h + 1) * hk]
            b_last = b[lasts[d]:lasts[d] + 1]
            q = qk_ref[tile[d], h * hk:(h + 1) * hk].astype(F32)
            k = qk_ref[tile[d], dk + h * hk:dk + (h + 1) * hk].astype(F32)
            v = v_ref[tile[d], h * hv:(h + 1) * hv]
            scores = []
            for i in (range(c // blk) if row_ref is None else ()):
                rows = slice(i * blk, (i + 1) * blk)
                keys = slice(0, (i + 1) * blk) if d == 0 else slice(i * blk, c)
                b_ref = b[i * blk + blk // 2:i * blk + blk // 2 + 1]
                qe = (q[rows] * jnp.exp(b[rows] - b_ref)).astype(BF16)
                ke = (k[keys] * jnp.exp(b_ref - b[keys])).astype(BF16)
                scores.append((rows, keys, _dot_nt(qe, ke)))
            qs = (q * jnp.exp(b)).astype(BF16)
            kd = (k * jnp.exp(b_last - b)).astype(BF16)
            st = st_ref[d, h]
            inter = _dot_nt(qs, st.astype(BF16))
            st_ref[d, h] = st * jnp.exp(b_last) + _dot_tn(v, kd)
            stage[d, h] = (scores, inter, v, (q, k, b))
    yield

    for d, (_, _, _, o_ref) in enumerate(io):
        for h in range(GLA_HEADS):
            scores, inter, v, (q, k, b) = stage[d, h]
            if row_ref is not None:
                o = _gla_exact_intra(d, q, k, v, b, row_ref) + inter
                o_ref[tile[d], h * hv:(h + 1) * hv] = o.astype(BF16)
            for rows, keys, s in scores:
                t_idx = lax.broadcasted_iota(jnp.int32, s.shape, 0) + rows.start
                s_idx = lax.broadcasted_iota(jnp.int32, s.shape, 1) + keys.start
                p = jnp.where((s_idx <= t_idx) if d == 0 else (s_idx >= t_idx), s, 0.0).astype(BF16)
                o = _dot(p, v[keys]) + inter[rows]
                o_ref[base[d] + rows.start:base[d] + rows.stop, h * hv:(h + 1) * hv] = o.astype(BF16)


def _bwd_order(n, n_ctx, n_all):
    return jnp.where(n < n_ctx, n_ctx - 1 - n, n_all - 1 - (n - n_ctx))


def _gla_core(qk, v, la, exact, n_ctx_rows):
    bsz, tt, dv = v.shape
    dk = qk.shape[-1] // 2
    c = GLA_TILE
    n_all, n_ctx = tt // c, n_ctx_rows // c
    hk, hv = dk // GLA_HEADS, dv // GLA_HEADS
    fwd = lambda b, n: (b, n, 0)
    bwd = lambda b, n: (b, _bwd_order(n, n_ctx, n_all), 0)
    spec = lambda w, idx: pl.BlockSpec((None, c, w), idx)
    return pl.pallas_call(
        functools.partial(_gla_core_kernel, dk=dk, n_ctx=n_ctx, n_all=n_all),
        out_shape=[jax.ShapeDtypeStruct((bsz, tt, dv), BF16)] * 2,
        grid=(bsz, n_all),
        in_specs=[pl.BlockSpec(memory_space=pltpu.SMEM),
                  spec(2 * dk, fwd), spec(dv, fwd), spec(2 * dk, fwd),
                  spec(2 * dk, bwd), spec(dv, bwd), spec(2 * dk, bwd)],
        out_specs=[spec(dv, fwd), spec(dv, bwd)],
        scratch_shapes=[pltpu.VMEM((2, GLA_HEADS, hv, hk), F32),
                        pltpu.VMEM((GLA_CHUNK, 2 * hk + hv), F32)],
        compiler_params=_cparams("arbitrary", "arbitrary"),
        name="gla_core",
    )(exact, qk, v, la, qk, v, la)


def _rope(x, cos, sin_signed):
    lane = lax.broadcasted_iota(jnp.int32, (x.shape[0], 128), 1)
    first = (lane % 32) < 16
    out = []
    for j in range(x.shape[1] // 128):
        xj = x[:, j * 128:(j + 1) * 128]
        partner = jnp.where(first, pltpu.roll(xj, 112, 1), pltpu.roll(xj, 16, 1))
        out.append(xj * cos + partner * sin_signed)
    return jnp.concatenate(out, axis=1)


def _swa_pre_kernel(*refs, n_ctx):
    (mod_ref, g_ref, w_ref, cos_ref, sin_ref, q_ref, kv_ref, z_ref) = refs[-8:]
    h = _prenorm(_token_tile(refs[:-8], n_ctx), g_ref[...], mod_ref[...]).astype(BF16)
    qw = q_ref.shape[-1]
    kw = kv_ref.shape[-1] // 2
    cos, sin = cos_ref[...], sin_ref[...]
    q = _rope(_dot(h, w_ref[:, :qw]), cos, sin) * (ATT_HEAD_DIM ** -0.5 * LOG2_E)
    q_ref[...] = q.astype(BF16)
    kv_ref[:, :kw] = _rope(_dot(h, w_ref[:, qw:qw + kw]), cos, sin).astype(BF16)
    kv_ref[:, kw:] = _dot(h, w_ref[:, qw + kw:qw + 2 * kw]).astype(BF16)
    z_ref[...] = _dot(h, w_ref[:, qw + 2 * kw:]).astype(BF16)


def _swa_core_kernel(sink_ref, q_ref, kvp_ref, kvc_ref, kvn_ref, kvx_ref, o_ref, *, n_ctx, n_all):
    t = pl.program_id(1)
    blk = ATT_BLOCK
    n_keys = 3 * blk + kvx_ref.shape[0]
    kw = kvc_ref.shape[-1] // 2
    n_loc = 3 * blk
    i = lax.broadcasted_iota(jnp.int32, (blk, n_loc), 0)
    j = lax.broadcasted_iota(jnp.int32, (blk, n_loc), 1)
    never = n_keys
    off_prev = jnp.where(t > n_ctx, 0, never)
    off_cur = jnp.where(t >= n_ctx, 0, never)
    off_next = jnp.where(jnp.logical_and(t >= n_ctx, t < n_all - 1), 0, never)
    valid = (((j >= i + off_prev) & (j < blk))
             | ((j >= blk + off_cur) & (j < 2 * blk))
             | ((j >= 2 * blk) & (j <= i + 2 * blk - off_next)))
    bias = jnp.where(valid, 0.0, -jnp.inf)
    logits = []
    for h in range(ATT_KV_HEADS):
        ks = slice(h * ATT_HEAD_DIM, (h + 1) * ATT_HEAD_DIM)
        k = jnp.concatenate([kvp_ref[:, ks], kvc_ref[:, ks], kvn_ref[:, ks], kvx_ref[:, ks]], axis=0)
        q = jnp.concatenate(
            [q_ref[:, (h * ATT_GROUP + g) * ATT_HEAD_DIM:(h * ATT_GROUP + g + 1) * ATT_HEAD_DIM]
             for g in range(ATT_GROUP)], axis=0)
        logits.append(_dot_nt(q, k))
    chunks = [(h, r0) for h in range(ATT_KV_HEADS) for r0 in range(0, ATT_GROUP * blk, ATT_ROWS)]

    def slabs(h, r0):
        out = []
        for c0 in range(0, n_keys, 128):
            s = logits[h][r0:r0 + ATT_ROWS, c0:c0 + 128]
            if c0 < n_loc:
                s = s + bias[r0 % blk:r0 % blk + ATT_ROWS, c0:c0 + 128]
            out.append(s)
        return out

    sinks = {(h, r0): sink_ref[h * ATT_GROUP + r0 // blk] * LOG2_E for h, r0 in chunks}
    maxes = {}
    for h, r0 in chunks:
        lane_max = functools.reduce(jnp.maximum, slabs(h, r0))
        maxes[h, r0] = jnp.maximum(jnp.max(lane_max, axis=-1, keepdims=True), sinks[h, r0])
    for h in range(ATT_KV_HEADS):
        vs = slice(kw + h * ATT_HEAD_DIM, kw + (h + 1) * ATT_HEAD_DIM)
        v = jnp.concatenate([kvp_ref[:, vs], kvc_ref[:, vs], kvn_ref[:, vs], kvx_ref[:, vs]], axis=0)
        probs, lane_sums = [], []
        for _, r0 in chunks[:len(chunks) // ATT_KV_HEADS]:
            p = [jnp.exp2(s - maxes[h, r0]) for s in slabs(h, r0)]
            probs.append(jnp.concatenate(p, axis=1).astype(BF16))
            lane_sums.append(functools.reduce(jnp.add, p))
        pv = _dot(jnp.concatenate(probs, axis=0), v)
        denom = jnp.concatenate(
            [jnp.sum(ls, axis=-1, keepdims=True) + jnp.exp2(sinks[h, r0] - maxes[h, r0])
             for ls, (_, r0) in zip(lane_sums, chunks)], axis=0)
        o = pv * (1.0 / denom)
        for g in range(ATT_GROUP):
            lo = (h * ATT_GROUP + g) * ATT_HEAD_DIM
            o_ref[:, lo:lo + ATT_HEAD_DIM] = o[g * blk:(g + 1) * blk].astype(BF16)


def _swa_core(q, kv, sink, n_ctx_rows):
    bsz, tt, qw = q.shape
    blk = ATT_BLOCK
    n_all, n_ctx = tt // blk, n_ctx_rows // blk
    kvw = kv.shape[-1]
    kv_spec = lambda idx: pl.BlockSpec((None, blk, kvw), idx)
    return pl.pallas_call(
        functools.partial(_swa_core_kernel, n_ctx=n_ctx, n_all=n_all),
        out_shape=jax.ShapeDtypeStruct((bsz, tt, qw), BF16),
        grid=(bsz, n_all),
        in_specs=[
            pl.BlockSpec(memory_space=pltpu.SMEM),
            pl.BlockSpec((None, blk, qw), lambda b, t: (b, t, 0)),
            kv_spec(lambda b, t: (b, jnp.maximum(t - 1, 0), 0)),
            kv_spec(lambda b, t: (b, t, 0)),
            kv_spec(lambda b, t: (b, jnp.minimum(t + 1, n_all - 1), 0)),
            pl.BlockSpec((None, n_ctx_rows, kvw), lambda b, t: (b, 0, 0)),
        ],
        out_specs=pl.BlockSpec((None, blk, qw), lambda b, t: (b, t, 0)),
        compiler_params=_cparams("parallel", "parallel"),
        name="swa_core",
    )(sink, q, kv, kv, kv, kv)


def _segment_permutation(n):
    r = jnp.arange(n)
    p = (r[None, :] == ((r % 8) * (n // 8) + r // 8)[:, None]).astype(BF16)
    return p, p.T


def _rnn_pre_kernel(x_ref, mod_ref, g_ref, w_ref, u_ref, z_ref, *, n_ctx):
    del n_ctx
    h = _prenorm(x_ref[...], g_ref[...], mod_ref[...]).astype(BF16)
    rw = u_ref.shape[-1]
    u_ref[...] = _dot(h, w_ref[:, :rw]).astype(BF16)
    z_ref[...] = _dot(h, w_ref[:, rw:]).astype(BF16)


def _shift_rows(cur, other, up):
    sub = lax.broadcasted_iota(jnp.int32, cur.shape, 0)
    if up:
        return jnp.where(sub == 7, pltpu.roll(other, 7, 0), pltpu.roll(cur, 7, 0))
    return jnp.where(sub == 0, pltpu.roll(other, 1, 0), pltpu.roll(cur, 1, 0))


def _segment_scan(a, x, carry, reverse):
    n_g = a.shape[0] // 8
    order = range(n_g - 1, -1, -1) if reverse else range(n_g)
    hs, ps = [None] * n_g, [None] * n_g
    h = p = None
    for g in order:
        ag, xg = a[8 * g:8 * g + 8], x[8 * g:8 * g + 8]
        h = xg if h is None else ag * h + xg
        p = ag if p is None else ag * p
        hs[g], ps[g] = h, p
    sub = lax.broadcasted_iota(jnp.int32, h.shape, 0)
    for s in (1, 2, 4):
        shift = 8 - s if reverse else s
        inside = (sub < 8 - s) if reverse else (sub >= s)
        h = h + p * jnp.where(inside, pltpu.roll(h, shift, 0), 0.0)
        p = p * jnp.where(inside, pltpu.roll(p, shift, 0), 1.0)
    full = h + p * carry
    first, last = (7, 0) if reverse else (0, 7)
    seg_in = jnp.where(sub == first, carry, pltpu.roll(full, 7 if reverse else 1, 0))
    out = jnp.concatenate([hs[g] + ps[g] * seg_in for g in range(n_g)], axis=0)
    return out, jnp.broadcast_to(full[last:last + 1], full.shape)


def _rnn_core_kernel(upf_ref, ucf_ref, unf_ref, upb_ref, ucb_ref, unb_ref,
                     cw_ref, cb_ref, wg_ref, ba_ref, bx_ref, lam_ref,
                     perm_ref, unperm_ref, hf_ref, hb_ref, carry_ref, *, n_ctx, n_all):
    n = pl.program_id(1)

    @pl.when(n == 0)
    def _():
        carry_ref[...] = jnp.zeros_like(carry_ref)

    tiles = (n, _bwd_order(n, n_ctx, n_all))
    prev_ok = [jnp.where(jnp.logical_and(t != 0, t != n_ctx), 1.0, 0.0) for t in tiles]
    next_ok = [jnp.where(jnp.logical_and(t != n_ctx - 1, t != n_all - 1), 1.0, 0.0) for t in tiles]
    refs = ((upf_ref, ucf_ref, unf_ref, hf_ref), (upb_ref, ucb_ref, unb_ref, hb_ref))

    useg = [_dot(perm_ref[...], refs[d][1][...]) for d in range(2)]
    rows = useg[0].shape[0]
    convs, pres = {}, {}
    for d, (p_ref, _, n_ref, _) in enumerate(refs):
        before = p_ref[...].astype(F32)[HALO - 8:] * prev_ok[d]
        after = n_ref[...].astype(F32)[:8] * next_ok[d]
        before2 = pltpu.roll(before, 1, 0)
        for j in range(RNN_HEADS):
            lanes = slice(j * RNN_HD, (j + 1) * RNN_HD)
            u = useg[d][:, lanes]
            ext = jnp.concatenate([
                _shift_rows(u[rows - 16:rows - 8], before2[:, lanes], up=False),
                _shift_rows(u[rows - 8:], before[:, lanes], up=False),
                u,
                _shift_rows(u[:8], after[:, lanes], up=True)], axis=0)
            conv = cb_ref[:, lanes]
            for tap in range(CONV_W):
                conv = conv + ext[8 * tap:8 * tap + rows] * cw_ref[tap:tap + 1, lanes]
            convs[d, j] = conv
            pres[d, j] = _dot(conv.astype(BF16), wg_ref[d, j])

    half = RNN_HEADS // 2
    for j0 in (0, half):
        for d in range(2):
            outs = []
            for j in range(j0, j0 + half):
                pre, conv = pres[d, j], convs[d, j]
                u = jnp.tanh(pre[:, :RNN_HD] + ba_ref[d, j]) + 1.0
                i = 0.5 * jnp.tanh(pre[:, RNN_HD:] + bx_ref[d, j]) + 0.5
                rate = (0.5 * LRU_C) * _softplus(-lam_ref[d, j])
                a = jnp.exp2((-LOG2_E * rate) * u)
                y = jnp.tanh(rate * u) * (1.0 + a * a)
                x = jnp.where(y > 0.0, y * lax.rsqrt(y), 0.0) * i * conv
                hs, carry = _segment_scan(a, x, carry_ref[d, j], reverse=(d == 1))
                carry_ref[d, j] = carry
                outs.append(hs.astype(BF16))
            h = _dot(unperm_ref[...], jnp.concatenate(outs, axis=1))
            refs[d][3][:, j0 * RNN_HD:(j0 + half) * RNN_HD] = h.astype(BF16)


def _rnn_core(u, conv_w, conv_b, wg, b_a, b_x, lam, n_ctx_rows):
    bsz, tt, rw = u.shape
    tile = RNN_TILE
    n_all, n_ctx = tt // tile, n_ctx_rows // tile
    per = tile // HALO
    n_halo = tt // HALO
    fwd = lambda n: n
    bwd = lambda n: _bwd_order(n, n_ctx, n_all)

    def specs(order):
        return [
            pl.BlockSpec((None, HALO, rw), lambda b, n: (b, jnp.maximum(order(n) * per - 1, 0), 0)),
            pl.BlockSpec((None, tile, rw), lambda b, n: (b, order(n), 0)),
            pl.BlockSpec((None, HALO, rw), lambda b, n: (b, jnp.minimum((order(n) + 1) * per, n_halo - 1), 0)),
        ]

    out_spec = lambda order: pl.BlockSpec((None, tile, rw), lambda b, n: (b, order(n), 0))
    perm, unperm = _segment_permutation(tile)
    consts = (conv_w, conv_b, wg, b_a, b_x, lam, perm, unperm)
    return pl.pallas_call(
        functools.partial(_rnn_core_kernel, n_ctx=n_ctx, n_all=n_all),
        out_shape=[jax.ShapeDtypeStruct((bsz, tt, rw), BF16)] * 2,
        grid=(bsz, n_all),
        in_specs=specs(fwd) + specs(bwd) + [_full_spec(c.shape) for c in consts],
        out_specs=[out_spec(fwd), out_spec(bwd)],
        scratch_shapes=[pltpu.VMEM((2, RNN_HEADS, 8, RNN_HD), F32)],
        compiler_params=_cparams("arbitrary", "arbitrary"),
        name="rglru_core",
    )(u, u, u, u, u, u, *consts)


def _residual(u, x, mod_ref, gp_ref, w_ref):
    y = _dot(u.astype(BF16), w_ref[...])
    return x + mod_ref[2:3] * (_rms(y) * gp_ref[...])


def _gla_post_kernel(*refs, n_ctx):
    (mod_ref, gp_ref, w_ref, of_ref, ob_ref, z_ref, gh_ref, o_ref) = refs[-8:]
    o = of_ref[...].astype(F32) + ob_ref[...].astype(F32)
    hv = o.shape[-1] // GLA_HEADS
    o = jnp.concatenate([_rms(o[:, h * hv:(h + 1) * hv]) for h in range(GLA_HEADS)], axis=1)
    u = o * gh_ref[...] * _silu(z_ref[...].astype(F32))
    o_ref[...] = _residual(u, _token_tile(refs[:-8], n_ctx), mod_ref, gp_ref, w_ref)


def _swa_post_kernel(x_ref, mod_ref, gp_ref, w_ref, a_ref, z_ref, o_ref, *, n_ctx):
    del n_ctx
    u = a_ref[...].astype(F32) * _silu(z_ref[...].astype(F32))
    o_ref[...] = _residual(u, x_ref[...], mod_ref, gp_ref, w_ref)


def _rnn_post_kernel(x_ref, mod_ref, gp_ref, w_ref, hf_ref, hb_ref, z_ref, o_ref, *, n_ctx):
    del n_ctx
    u = (hf_ref[...].astype(F32) + hb_ref[...].astype(F32)) * _silu(z_ref[...].astype(F32))
    o_ref[...] = _residual(u, x_ref[...], mod_ref, gp_ref, w_ref)


def _stream_shape(xs):
    if isinstance(xs, tuple):
        bsz, n_ctx_rows, d = xs[0].shape
        return bsz, n_ctx_rows + xs[1].shape[1], d
    return xs.shape


def _post_call(body, name, acts, vecs, xs, mod, g_post, w_out, n_ctx_rows, latent_only):
    bsz, tt, d = _stream_shape(xs)
    n_ctx = n_ctx_rows // ROW_TILE
    off = n_ctx if latent_only else 0
    n_tiles = tt // ROW_TILE - off
    streams = xs if isinstance(xs, tuple) else (xs,)
    return pl.pallas_call(
        functools.partial(body, n_ctx=n_ctx - off),
        out_shape=jax.ShapeDtypeStruct((bsz, n_tiles * ROW_TILE, d), F32),
        grid=(bsz, n_tiles),
        in_specs=_token_specs(xs, n_ctx, off)
        + [_mod_spec(d, n_ctx, bsz, off), _full_spec(g_post.shape), _full_spec(w_out.shape)]
        + [_row_spec(a.shape[-1], off) for a in acts]
        + [_full_spec(v.shape) for v in vecs],
        out_specs=_row_spec(d),
        compiler_params=_cparams("parallel", "parallel"),
        name=name,
    )(*streams, mod, g_post, w_out, *acts, *vecs)


def _pre_call(body, name, xs, mod, g_pre, consts, out_widths, out_dtypes, n_ctx_rows, row_consts=(),
              tile_stats=0):
    bsz, tt, d = _stream_shape(xs)
    n_ctx = n_ctx_rows // ROW_TILE
    streams = xs if isinstance(xs, tuple) else (xs,)
    stat_shape = jax.ShapeDtypeStruct((bsz, tt // ROW_TILE, 8, 128), F32)
    stat_spec = pl.BlockSpec((None, None, 8, 128), lambda b, t: (b, t, 0, 0))
    return pl.pallas_call(
        functools.partial(body, n_ctx=n_ctx),
        out_shape=[jax.ShapeDtypeStruct((bsz, tt, w), dt) for w, dt in zip(out_widths, out_dtypes)]
        + [stat_shape] * tile_stats,
        grid=(bsz, tt // ROW_TILE),
        in_specs=_token_specs(xs, n_ctx)
        + [_mod_spec(d, n_ctx, bsz), _full_spec(g_pre.shape)]
        + [_full_spec(c.shape) for c in consts]
        + [pl.BlockSpec((ROW_TILE, r.shape[-1]), lambda b, t: (t, 0)) for r in row_consts],
        out_specs=[_row_spec(w) for w in out_widths] + [stat_spec] * tile_stats,
        compiler_params=_cparams("parallel", "parallel"),
        name=name,
    )(*streams, mod, g_pre, *consts, *row_consts)


def _layer_gla(xs, mod, g_pre, g_post, w_in, w_g1, w_g2, b_g, g_head, w_out, n_ctx_rows, last):
    d = w_in.shape[0]
    rank, dk = w_g2.shape[1], w_g2.shape[2]
    dv = w_out.shape[0]
    wg1 = jnp.zeros((d, GLA_GATE_PAD), F32).at[:, :2 * rank].set(jnp.concatenate([w_g1[0], w_g1[1]], axis=1))
    wg2 = jnp.zeros((GLA_GATE_PAD, 2 * dk), F32)
    wg2 = wg2.at[:rank, :dk].set(w_g2[0]).at[rank:2 * rank, dk:].set(w_g2[1])
    qk, v, z, la, peak = _pre_call(
        functools.partial(_gla_pre_kernel, dk=dk), "gla_pre", xs, mod, g_pre,
        (w_in.astype(BF16), wg1.astype(BF16), wg2.astype(BF16), b_g.reshape(1, 2 * dk)),
        (2 * dk, dv, dv, 2 * dk), (BF16, BF16, BF16, F32), n_ctx_rows, tile_stats=1)
    exact = (peak[:, :, 0, 0] * (GLA_BLOCK // 2) > GLA_SAFE_EXPONENT).astype(jnp.int32)
    o_f, o_b = _gla_core(qk, v, la, exact, n_ctx_rows)
    return _post_call(_gla_post_kernel, "gla_post", (o_f, o_b, z), (g_head.reshape(1, dv),),
                      xs, mod, g_post, w_out.astype(BF16), n_ctx_rows, last)


def _layer_swa(xs, mod, g_pre, g_post, w_in, sink, w_out, rope, n_ctx_rows, last):
    qw = w_out.shape[0]
    kw = (w_in.shape[1] - 2 * qw) // 2
    q, kv, z = _pre_call(_swa_pre_kernel, "swa_pre", xs, mod, g_pre, (w_in.astype(BF16),),
                         (qw, 2 * kw, qw), (BF16, BF16, BF16), n_ctx_rows, row_consts=rope)
    a = _swa_core(q, kv, sink, n_ctx_rows)
    return _post_call(_swa_post_kernel, "swa_post", (a, z), (), xs, mod, g_post, w_out.astype(BF16),
                      n_ctx_rows, last)


def _layer_rnn(xs, mod, g_pre, g_post, w_in, conv_w, conv_b, w_ra, b_ra, w_ri, b_ri, lam, w_out,
               n_ctx_rows, last):
    rw = w_out.shape[0]
    u, z = _pre_call(_rnn_pre_kernel, "rglru_pre", xs, mod, g_pre, (w_in.astype(BF16),),
                     (rw, rw), (BF16, BF16), n_ctx_rows)
    wg = (0.5 * jnp.concatenate([w_ra, w_ri], axis=-1)).astype(BF16)
    vec = lambda a: a.reshape(2, RNN_HEADS, 1, RNN_HD)
    h_f, h_b = _rnn_core(u, conv_w, conv_b.reshape(1, rw), wg, vec(0.5 * b_ra), vec(0.5 * b_ri), vec(lam),
                         n_ctx_rows)
    return _post_call(_rnn_post_kernel, "rglru_post", (h_f, h_b, z), (), xs, mod, g_post,
                      w_out.astype(BF16), n_ctx_rows, last)


def _rope_tables(t, n_ctx_rows):
    pairs = ATT_HEAD_DIM // 4
    pos = jnp.arange(t, dtype=jnp.int32)
    row = (pos // GRID_W).astype(F32)
    col = (pos % GRID_W).astype(F32)
    freqs = ROPE_BASE ** (-jnp.arange(pairs, dtype=F32) / pairs)
    ang_r, ang_c = row[:, None] * freqs, col[:, None] * freqs
    cos = jnp.concatenate([jnp.cos(ang_r)] * 2 + [jnp.cos(ang_c)] * 2, axis=1)
    sin = jnp.concatenate([-jnp.sin(ang_r), jnp.sin(ang_r), -jnp.sin(ang_c), jnp.sin(ang_c)], axis=1)
    cos = jnp.concatenate([jnp.ones((n_ctx_rows, ATT_HEAD_DIM), F32), cos], axis=0)
    sin = jnp.concatenate([jnp.zeros((n_ctx_rows, ATT_HEAD_DIM), F32), sin], axis=0)
    return jnp.tile(cos, (1, 2)), jnp.tile(sin, (1, 2))


def kernel(x, c, ctx, c_ctx, w_mod, b_mod, g_pre, g_post, a_w_in, a_w_g1, a_w_g2, a_b_g, a_g_head, a_w_out, b_w_in, b_sink, b_w_out, c_w_in, c_conv_w, c_conv_b, c_w_ra, c_b_ra, c_w_ri, c_b_ri, c_lam, c_w_out):
    bsz, t, d = x.shape
    n_ctx_rows = ctx.shape[1]
    depth = w_mod.shape[0]
    assert bsz < MOD_ROWS and n_ctx_rows % ROW_TILE == 0 and t % ROW_TILE == 0

    cvec = jnp.zeros((MOD_ROWS, d), F32).at[:bsz].set(c).at[bsz].set(c_ctx)
    mods = _modulation(cvec, w_mod, b_mod)
    rope = _rope_tables(t, n_ctx_rows)
    xs = (ctx, x)
    for i in range(depth):
        kind, j = i % N_MIXERS, i // N_MIXERS
        last = i == depth - 1
        gpre, gpost = g_pre[i].reshape(1, d), g_post[i].reshape(1, d)
        if kind == 0:
            xs = _layer_gla(xs, mods[i], gpre, gpost, a_w_in[j], a_w_g1[j], a_w_g2[j], a_b_g[j],
                            a_g_head[j], a_w_out[j], n_ctx_rows, last)
        elif kind == 1:
            xs = _layer_swa(xs, mods[i], gpre, gpost, b_w_in[j], b_sink[j], b_w_out[j], rope,
                            n_ctx_rows, last)
        else:
            xs = _layer_rnn(xs, mods[i], gpre, gpost, c_w_in[j], c_conv_w[j], c_conv_b[j], c_w_ra[j],
                            c_b_ra[j], c_w_ri[j], c_b_ri[j], c_lam[j], c_w_out[j], n_ctx_rows, last)
    return xs
```

```python
import functools
from typing import Callable, NamedTuple

import jax
import jax.numpy as jnp
from jax import lax
from jax.experimental import pallas as pl
from jax.experimental.pallas import tpu as pltpu

F32 = jnp.float32
BF16 = jnp.bfloat16

EPS = 1e-6
GRID_W = 64
N_MIXERS = 3

GLA_HEADS = 4
GLA_TAU = 16.0
GLA_GATE_PAD = 128

ATT_HEAD_DIM = 64
ATT_KV_HEADS = 4
ATT_GROUP = 4
ATT_BLOCK = 128
ATT_ROWS = 32
ROPE_BASE = 10000.0
LOG2_E = 1.4426950408889634

RNN_HEADS = 10
RNN_HD = 128
CONV_W = 4
LRU_C = 8.0

ROW_TILE = 256
GLA_TILE = 256
GLA_CHUNK = 128
GLA_BLOCK = 64
GLA_SAFE_EXPONENT = 60.0
RNN_TILE = 256
HALO = 16
MOD_ROWS = 8
VMEM_LIMIT = 48 * 1024 * 1024


def _cparams(*sem):
    return pltpu.CompilerParams(dimension_semantics=sem, vmem_limit_bytes=VMEM_LIMIT)


def _sigmoid(x):
    return 0.5 * jnp.tanh(0.5 * x) + 0.5


def _silu(x):
    return x * _sigmoid(x)


def _softplus(x):
    return jnp.maximum(x, 0.0) + jnp.log1p(jnp.exp(-jnp.abs(x)))


def _log_sigmoid(x):
    return jnp.minimum(x, 0.0) - jnp.log(1.0 + jnp.exp(-jnp.abs(x)))


def _split3(a):
    hi = a.astype(BF16)
    r = a - hi.astype(F32)
    mid = r.astype(BF16)
    lo = (r - mid.astype(F32)).astype(BF16)
    return hi, mid, lo


def _dot(a, b):
    return jnp.dot(a, b, preferred_element_type=F32)


def _dot_nt(a, b):
    return lax.dot_general(a, b, (((1,), (1,)), ((), ())), preferred_element_type=F32)


def _dot_tn(a, b):
    return lax.dot_general(a, b, (((0,), (0,)), ((), ())), preferred_element_type=F32)


def _rms(x):
    return x * lax.rsqrt(jnp.mean(x * x, axis=-1, keepdims=True) + EPS)


def _mod_kernel(c_ref, w_ref, b_ref, o_ref):
    s = _silu(c_ref[...])
    s_hi, s_mid, s_lo = _split3(s)
    w_hi, w_mid, w_lo = _split3(w_ref[...])
    acc = _dot(s_hi, w_hi)
    acc += _dot(s_hi, w_mid) + _dot(s_mid, w_hi)
    acc += _dot(s_hi, w_lo) + _dot(s_mid, w_mid) + _dot(s_lo, w_hi)
    o_ref[...] = acc + b_ref[...]


def _modulation(cvec, w_mod, b_mod):
    depth, d, d3 = w_mod.shape
    out = pl.pallas_call(
        _mod_kernel,
        out_shape=jax.ShapeDtypeStruct((depth, MOD_ROWS, d3), F32),
        grid=(depth, d3 // d),
        in_specs=[
            pl.BlockSpec((MOD_ROWS, d), lambda i, j: (0, 0)),
            pl.BlockSpec((None, d, d), lambda i, j: (i, 0, j)),
            pl.BlockSpec((None, 1, d), lambda i, j: (i, 0, j)),
        ],
        out_specs=pl.BlockSpec((None, MOD_ROWS, d), lambda i, j: (i, 0, j)),
        compiler_params=_cparams("parallel", "parallel"),
        name="modulation",
    )(cvec, w_mod, b_mod.reshape(depth, 1, d3))
    return out.reshape(depth, MOD_ROWS, 3, d)


def _prenorm(x, g, mod):
    return _rms(x) * g * (1.0 + mod[1:2]) + mod[0:1]


def _mod_spec(d, n_ctx_tiles, ctx_row, offset=0):
    def index(b, t):
        return (jnp.where(t + offset < n_ctx_tiles, ctx_row, b), 0, 0)
    return pl.BlockSpec((None, 3, d), index)


def _row_spec(width, offset=0):
    return pl.BlockSpec((None, ROW_TILE, width), lambda b, t: (b, t + offset, 0))


def _token_specs(xs, n_ctx, offset=0):
    if not isinstance(xs, tuple):
        return [_row_spec(xs.shape[-1], offset)]
    d = xs[0].shape[-1]
    return [pl.BlockSpec((None, ROW_TILE, d), lambda b, t: (b, jnp.minimum(t + offset, n_ctx - 1), 0)),
            pl.BlockSpec((None, ROW_TILE, d), lambda b, t: (b, jnp.maximum(t + offset - n_ctx, 0), 0))]


def _token_tile(x_refs, n_ctx_tiles):
    if len(x_refs) == 1:
        return x_refs[0][...]
    return jnp.where(pl.program_id(1) < n_ctx_tiles, x_refs[0][...], x_refs[1][...])


def _full_spec(shape):
    zeros = (0,) * len(shape)
    return pl.BlockSpec(shape, lambda b, t: zeros)


def _gla_pre_part(x, mod_ref, g_ref, w_ref, wg1_ref, wg2_ref, bg_ref, qk_ref, v_ref, z_ref, la_ref, peak_ref,
                  *, dk):
    h = _prenorm(x, g_ref[...], mod_ref[...]).astype(BF16)
    hk = dk // GLA_HEADS
    dv = v_ref.shape[-1]
    g1 = _dot(h, wg1_ref[...]).astype(BF16)
    gate = _dot(g1, wg2_ref[...]) + bg_ref[...]
    la = _log_sigmoid(gate) / GLA_TAU
    la_ref[...] = la
    peak = jnp.max(jnp.max(-la, axis=1, keepdims=True), axis=0, keepdims=True)
    peak_ref[...] = jnp.broadcast_to(peak, peak_ref.shape)
    q = _dot(h, w_ref[:, :dk]) * hk ** -0.5
    qk_ref[:, :dk] = q.astype(BF16)
    qk_ref[:, dk:] = _dot(h, w_ref[:, dk:2 * dk]).astype(BF16)
    v_ref[...] = _dot(h, w_ref[:, 2 * dk:2 * dk + dv]).astype(BF16)
    z_ref[...] = _dot(h, w_ref[:, 2 * dk + dv:]).astype(BF16)


def _gla_core_kernel(exact_ref, qkf_ref, vf_ref, laf_ref, qkb_ref, vb_ref, lab_ref,
                     of_ref, ob_ref, st_ref, row_ref, *, dk, n_ctx, n_all):
    b, n = pl.program_id(0), pl.program_id(1)

    @pl.when(n == 0)
    def _():
        st_ref[...] = jnp.zeros_like(st_ref)

    io = ((qkf_ref, vf_ref, laf_ref, of_ref), (qkb_ref, vb_ref, lab_ref, ob_ref))
    n_sub = qkf_ref.shape[0] // GLA_CHUNK

    def run(exact):
        chunks = [_gla_chunk_pair(io, (s * GLA_CHUNK, (n_sub - 1 - s) * GLA_CHUNK), st_ref, dk,
                                  row_ref if exact else None) for s in range(n_sub)]
        for _ in range(3):
            for chunk in chunks:
                next(chunk, None)

    out_of_range = exact_ref[b, n] + exact_ref[b, _bwd_order(n, n_ctx, n_all)]
    pl.when(out_of_range == 0)(lambda: run(False))
    pl.when(out_of_range != 0)(lambda: run(True))


def _gla_exact_intra(d, q, k, v, b, row_ref):
    c, hk = q.shape
    row_ref[:, :hk] = k
    row_ref[:, hk:2 * hk] = b
    row_ref[:, 2 * hk:] = v.astype(F32)
    t_idx = lax.broadcasted_iota(jnp.int32, (c, 1), 0)

    def keys(g, acc):
        group = row_ref[pl.ds(pl.multiple_of(g * 8, 8), 8), :]
        for r in range(8):
            s = g * 8 + r
            k_s, b_s, v_s = group[r:r + 1, :hk], group[r:r + 1, hk:2 * hk], group[r:r + 1, 2 * hk:]
            decay = jnp.exp(jnp.minimum(b - b_s, 0.0))
            col = jnp.sum(q * decay * k_s, axis=1, keepdims=True)
            causal = (t_idx >= s) if d == 0 else (t_idx <= s)
            acc = acc + jnp.where(causal, col, 0.0) * v_s
        return acc

    return lax.fori_loop(0, c // 8, keys, jnp.zeros((c, v.shape[1]), F32))


def _gla_chunk_pair(io, base, st_ref, dk, row_ref):
    c = GLA_CHUNK
    hk = dk // GLA_HEADS
    hv = io[0][1].shape[-1] // GLA_HEADS
    row = lax.broadcasted_iota(jnp.int32, (c, c), 0)
    col = lax.broadcasted_iota(jnp.int32, (c, c), 1)
    blk = GLA_BLOCK
    keeps = (col <= row, col >= row)
    lasts = (c - 1, 0)
    tile = [slice(r0, r0 + c) for r0 in base]
    cums = []
    for d, (_, _, la_ref, _) in enumerate(io):
        tri = jnp.where(keeps[d], 1.0, 0.0).astype(BF16)
        la_hi, la_mid, _ = _split3(la_ref[tile[d], d * dk:(d + 1) * dk])
        cums.append(_dot(tri, la_hi) + _dot(tri, la_mid))
    yield

    stage = {}
    for d, (qk_ref, v_ref, _, _) in enumerate(io):
        for h in range(GLA_HEADS):
            b = cums[d][:, h * hk:(h + 1) * hk]
            b_last = b[lasts[d]:lasts[d] + 1]
            q = qk_ref[tile[d], h * hk:(h + 1) * hk].astype(F32)
            k = qk_ref[tile[d], dk + h * hk:dk + (h + 1) * hk].astype(F32)
            v = v_ref[tile[d], h * hv:(h + 1) * hv]
            scores = []
            for i in (range(c // blk) if row_ref is None else ()):
                rows = slice(i * blk, (i + 1) * blk)
                keys = slice(0, (i + 1) * blk) if d == 0 else slice(i * blk, c)
                b_ref = b[i * blk + blk // 2:i * blk + blk // 2 + 1]
                qe = (q[rows] * jnp.exp(b[rows] - b_ref)).astype(BF16)
                ke = (k[keys] * jnp.exp(b_ref - b[keys])).astype(BF16)
                scores.append((rows, keys, _dot_nt(qe, ke)))
            qs = (q * jnp.exp(b)).astype(BF16)
            kd = (k * jnp.exp(b_last - b)).astype(BF16)
            st = st_ref[d, h]
            inter = _dot_nt(qs, st.astype(BF16))
            st_ref[d, h] = st * jnp.exp(b_last) + _dot_tn(v, kd)
            stage[d, h] = (scores, inter, v, (q, k, b))
    yield

    for d, (_, _, _, o_ref) in enumerate(io):
        for h in range(GLA_HEADS):
            scores, inter, v, (q, k, b) = stage[d, h]
            if row_ref is not None:
                o = _gla_exact_intra(d, q, k, v, b, row_ref) + inter
                o_ref[tile[d], h * hv:(h + 1) * hv] = o.astype(BF16)
            for rows, keys, s in scores:
                t_idx = lax.broadcasted_iota(jnp.int32, s.shape, 0) + rows.start
                s_idx = lax.broadcasted_iota(jnp.int32, s.shape, 1) + keys.start
                p = jnp.where((s_idx <= t_idx) if d == 0 else (s_idx >= t_idx), s, 0.0).astype(BF16)
                o = _dot(p, v[keys]) + inter[rows]
                o_ref[base[d] + rows.start:base[d] + rows.stop, h * hv:(h + 1) * hv] = o.astype(BF16)


def _bwd_order(n, n_ctx, n_all):
    return jnp.where(n < n_ctx, n_ctx - 1 - n, n_all - 1 - (n - n_ctx))


def _gla_core(qk, v, la, exact, n_ctx_rows):
    bsz, tt, dv = v.shape
    dk = qk.shape[-1] // 2
    c = GLA_TILE
    n_all, n_ctx = tt // c, n_ctx_rows // c
    hk, hv = dk // GLA_HEADS, dv // GLA_HEADS
    fwd = lambda b, n: (b, n, 0)
    bwd = lambda b, n: (b, _bwd_order(n, n_ctx, n_all), 0)
    spec = lambda w, idx: pl.BlockSpec((None, c, w), idx)
    return pl.pallas_call(
        functools.partial(_gla_core_kernel, dk=dk, n_ctx=n_ctx, n_all=n_all),
        out_shape=[jax.ShapeDtypeStruct((bsz, tt, dv), BF16)] * 2,
        grid=(bsz, n_all),
        in_specs=[pl.BlockSpec(memory_space=pltpu.SMEM),
                  spec(2 * dk, fwd), spec(dv, fwd), spec(2 * dk, fwd),
                  spec(2 * dk, bwd), spec(dv, bwd), spec(2 * dk, bwd)],
        out_specs=[spec(dv, fwd), spec(dv, bwd)],
        scratch_shapes=[pltpu.VMEM((2, GLA_HEADS, hv, hk), F32),
                        pltpu.VMEM((GLA_CHUNK, 2 * hk + hv), F32)],
        compiler_params=_cparams("arbitrary", "arbitrary"),
        name="gla_core",
    )(exact, qk, v, la, qk, v, la)


def _rope(x, cos, sin_signed):
    lane = lax.broadcasted_iota(jnp.int32, (x.shape[0], 128), 1)
    first = (lane % 32) < 16
    out = []
    for j in range(x.shape[1] // 128):
        xj = x[:, j * 128:(j + 1) * 128]
        partner = jnp.where(first, pltpu.roll(xj, 112, 1), pltpu.roll(xj, 16, 1))
        out.append(xj * cos + partner * sin_signed)
    return jnp.concatenate(out, axis=1)


def _swa_pre_part(x, mod_ref, g_ref, w_ref, cos_ref, sin_ref, q_ref, kv_ref, z_ref):
    h = _prenorm(x, g_ref[...], mod_ref[...]).astype(BF16)
    qw = q_ref.shape[-1]
    kw = kv_ref.shape[-1] // 2
    cos, sin = cos_ref[...], sin_ref[...]
    q = _rope(_dot(h, w_ref[:, :qw]), cos, sin) * (ATT_HEAD_DIM ** -0.5 * LOG2_E)
    q_ref[...] = q.astype(BF16)
    kv_ref[:, :kw] = _rope(_dot(h, w_ref[:, qw:qw + kw]), cos, sin).astype(BF16)
    kv_ref[:, kw:] = _dot(h, w_ref[:, qw + kw:qw + 2 * kw]).astype(BF16)
    z_ref[...] = _dot(h, w_ref[:, qw + 2 * kw:]).astype(BF16)


def _swa_core_kernel(sink_ref, q_ref, kvp_ref, kvc_ref, kvn_ref, kvx_ref, o_ref, *, n_ctx, n_all):
    t = pl.program_id(1)
    blk = ATT_BLOCK
    n_keys = 3 * blk + kvx_ref.shape[0]
    kw = kvc_ref.shape[-1] // 2
    n_loc = 3 * blk
    i = lax.broadcasted_iota(jnp.int32, (blk, n_loc), 0)
    j = lax.broadcasted_iota(jnp.int32, (blk, n_loc), 1)
    never = n_keys
    off_prev = jnp.where(t > n_ctx, 0, never)
    off_cur = jnp.where(t >= n_ctx, 0, never)
    off_next = jnp.where(jnp.logical_and(t >= n_ctx, t < n_all - 1), 0, never)
    valid = (((j >= i + off_prev) & (j < blk))
             | ((j >= blk + off_cur) & (j < 2 * blk))
             | ((j >= 2 * blk) & (j <= i + 2 * blk - off_next)))
    bias = jnp.where(valid, 0.0, -jnp.inf)
    logits = []
    for h in range(ATT_KV_HEADS):
        ks = slice(h * ATT_HEAD_DIM, (h + 1) * ATT_HEAD_DIM)
        k = jnp.concatenate([kvp_ref[:, ks], kvc_ref[:, ks], kvn_ref[:, ks], kvx_ref[:, ks]], axis=0)
        q = jnp.concatenate(
            [q_ref[:, (h * ATT_GROUP + g) * ATT_HEAD_DIM:(h * ATT_GROUP + g + 1) * ATT_HEAD_DIM]
             for g in range(ATT_GROUP)], axis=0)
        logits.append(_dot_nt(q, k))
    chunks = [(h, r0) for h in range(ATT_KV_HEADS) for r0 in range(0, ATT_GROUP * blk, ATT_ROWS)]

    def slabs(h, r0):
        out = []
        for c0 in range(0, n_keys, 128):
            s = logits[h][r0:r0 + ATT_ROWS, c0:c0 + 128]
            if c0 < n_loc:
                s = s + bias[r0 % blk:r0 % blk + ATT_ROWS, c0:c0 + 128]
            out.append(s)
        return out

    sinks = {(h, r0): sink_ref[h * ATT_GROUP + r0 // blk] * LOG2_E for h, r0 in chunks}
    maxes = {}
    for h, r0 in chunks:
        lane_max = functools.reduce(jnp.maximum, slabs(h, r0))
        maxes[h, r0] = jnp.maximum(jnp.max(lane_max, axis=-1, keepdims=True), sinks[h, r0])
    for h in range(ATT_KV_HEADS):
        vs = slice(kw + h * ATT_HEAD_DIM, kw + (h + 1) * ATT_HEAD_DIM)
        v = jnp.concatenate([kvp_ref[:, vs], kvc_ref[:, vs], kvn_ref[:, vs], kvx_ref[:, vs]], axis=0)
        probs, lane_sums = [], []
        for _, r0 in chunks[:len(chunks) // ATT_KV_HEADS]:
            p = [jnp.exp2(s - maxes[h, r0]) for s in slabs(h, r0)]
            probs.append(jnp.concatenate(p, axis=1).astype(BF16))
            lane_sums.append(functools.reduce(jnp.add, p))
        pv = _dot(jnp.concatenate(probs, axis=0), v)
        denom = jnp.concatenate(
            [jnp.sum(ls, axis=-1, keepdims=True) + jnp.exp2(sinks[h, r0] - maxes[h, r0])
             for ls, (_, r0) in zip(lane_sums, chunks)], axis=0)
        o = pv * (1.0 / denom)
        for g in range(ATT_GROUP):
            lo = (h * ATT_GROUP + g) * ATT_HEAD_DIM
            o_ref[:, lo:lo + ATT_HEAD_DIM] = o[g * blk:(g + 1) * blk].astype(BF16)


def _swa_core(q, kv, sink, n_ctx_rows):
    bsz, tt, qw = q.shape
    blk = ATT_BLOCK
    n_all, n_ctx = tt // blk, n_ctx_rows // blk
    kvw = kv.shape[-1]
    kv_spec = lambda idx: pl.BlockSpec((None, blk, kvw), idx)
    return pl.pallas_call(
        functools.partial(_swa_core_kernel, n_ctx=n_ctx, n_all=n_all),
        out_shape=jax.ShapeDtypeStruct((bsz, tt, qw), BF16),
        grid=(bsz, n_all),
        in_specs=[
            pl.BlockSpec(memory_space=pltpu.SMEM),
            pl.BlockSpec((None, blk, qw), lambda b, t: (b, t, 0)),
            kv_spec(lambda b, t: (b, jnp.maximum(t - 1, 0), 0)),
            kv_spec(lambda b, t: (b, t, 0)),
            kv_spec(lambda b, t: (b, jnp.minimum(t + 1, n_all - 1), 0)),
            pl.BlockSpec((None, n_ctx_rows, kvw), lambda b, t: (b, 0, 0)),
        ],
        out_specs=pl.BlockSpec((None, blk, qw), lambda b, t: (b, t, 0)),
        compiler_params=_cparams("parallel", "parallel"),
        name="swa_core",
    )(sink, q, kv, kv, kv, kv)


def _segment_permutation(n):
    r = jnp.arange(n)
    p = (r[None, :] == ((r % 8) * (n // 8) + r // 8)[:, None]).astype(BF16)
    return p, p.T


def _rnn_pre_part(x, mod_ref, g_ref, w_ref, u_ref, z_ref):
    h = _prenorm(x, g_ref[...], mod_ref[...]).astype(BF16)
    rw = u_ref.shape[-1]
    u_ref[...] = _dot(h, w_ref[:, :rw]).astype(BF16)
    z_ref[...] = _dot(h, w_ref[:, rw:]).astype(BF16)


def _shift_rows(cur, other, up):
    sub = lax.broadcasted_iota(jnp.int32, cur.shape, 0)
    if up:
        return jnp.where(sub == 7, pltpu.roll(other, 7, 0), pltpu.roll(cur, 7, 0))
    return jnp.where(sub == 0, pltpu.roll(other, 1, 0), pltpu.roll(cur, 1, 0))


def _segment_scan(a, x, carry, reverse):
    n_g = a.shape[0] // 8
    order = range(n_g - 1, -1, -1) if reverse else range(n_g)
    hs, ps = [None] * n_g, [None] * n_g
    h = p = None
    for g in order:
        ag, xg = a[8 * g:8 * g + 8], x[8 * g:8 * g + 8]
        h = xg if h is None else ag * h + xg
        p = ag if p is None else ag * p
        hs[g], ps[g] = h, p
    sub = lax.broadcasted_iota(jnp.int32, h.shape, 0)
    for s in (1, 2, 4):
        shift = 8 - s if reverse else s
        inside = (sub < 8 - s) if reverse else (sub >= s)
        h = h + p * jnp.where(inside, pltpu.roll(h, shift, 0), 0.0)
        p = p * jnp.where(inside, pltpu.roll(p, shift, 0), 1.0)
    full = h + p * carry
    first, last = (7, 0) if reverse else (0, 7)
    seg_in = jnp.where(sub == first, carry, pltpu.roll(full, 7 if reverse else 1, 0))
    out = jnp.concatenate([hs[g] + ps[g] * seg_in for g in range(n_g)], axis=0)
    return out, jnp.broadcast_to(full[last:last + 1], full.shape)


def _rnn_core_kernel(upf_ref, ucf_ref, unf_ref, upb_ref, ucb_ref, unb_ref,
                     cw_ref, cb_ref, wg_ref, ba_ref, bx_ref, lam_ref,
                     perm_ref, unperm_ref, hf_ref, hb_ref, carry_ref, *, n_ctx, n_all):
    n = pl.program_id(1)

    @pl.when(n == 0)
    def _():
        carry_ref[...] = jnp.zeros_like(carry_ref)

    tiles = (n, _bwd_order(n, n_ctx, n_all))
    prev_ok = [jnp.where(jnp.logical_and(t != 0, t != n_ctx), 1.0, 0.0) for t in tiles]
    next_ok = [jnp.where(jnp.logical_and(t != n_ctx - 1, t != n_all - 1), 1.0, 0.0) for t in tiles]
    refs = ((upf_ref, ucf_ref, unf_ref, hf_ref), (upb_ref, ucb_ref, unb_ref, hb_ref))

    useg = [_dot(perm_ref[...], refs[d][1][...]) for d in range(2)]
    rows = useg[0].shape[0]
    convs, pres = {}, {}
    for d, (p_ref, _, n_ref, _) in enumerate(refs):
        before = p_ref[...].astype(F32)[HALO - 8:] * prev_ok[d]
        after = n_ref[...].astype(F32)[:8] * next_ok[d]
        before2 = pltpu.roll(before, 1, 0)
        for j in range(RNN_HEADS):
            lanes = slice(j * RNN_HD, (j + 1) * RNN_HD)
            u = useg[d][:, lanes]
            ext = jnp.concatenate([
                _shift_rows(u[rows - 16:rows - 8], before2[:, lanes], up=False),
                _shift_rows(u[rows - 8:], before[:, lanes], up=False),
                u,
                _shift_rows(u[:8], after[:, lanes], up=True)], axis=0)
            conv = cb_ref[:, lanes]
            for tap in range(CONV_W):
                conv = conv + ext[8 * tap:8 * tap + rows] * cw_ref[tap:tap + 1, lanes]
            convs[d, j] = conv
            pres[d, j] = _dot(conv.astype(BF16), wg_ref[d, j])

    half = RNN_HEADS // 2
    for j0 in (0, half):
        for d in range(2):
            outs = []
            for j in range(j0, j0 + half):
                pre, conv = pres[d, j], convs[d, j]
                u = jnp.tanh(pre[:, :RNN_HD] + ba_ref[d, j]) + 1.0
                i = 0.5 * jnp.tanh(pre[:, RNN_HD:] + bx_ref[d, j]) + 0.5
                rate = (0.5 * LRU_C) * _softplus(-lam_ref[d, j])
                a = jnp.exp2((-LOG2_E * rate) * u)
                y = jnp.tanh(rate * u) * (1.0 + a * a)
                x = jnp.where(y > 0.0, y * lax.rsqrt(y), 0.0) * i * conv
                hs, carry = _segment_scan(a, x, carry_ref[d, j], reverse=(d == 1))
                carry_ref[d, j] = carry
                outs.append(hs.astype(BF16))
            h = _dot(unperm_ref[...], jnp.concatenate(outs, axis=1))
            refs[d][3][:, j0 * RNN_HD:(j0 + half) * RNN_HD] = h.astype(BF16)


def _rnn_core(u, conv_w, conv_b, wg, b_a, b_x, lam, n_ctx_rows):
    bsz, tt, rw = u.shape
    tile = RNN_TILE
    n_all, n_ctx = tt // tile, n_ctx_rows // tile
    per = tile // HALO
    n_halo = tt // HALO
    fwd = lambda n: n
    bwd = lambda n: _bwd_order(n, n_ctx, n_all)

    def specs(order):
        return [
            pl.BlockSpec((None, HALO, rw), lambda b, n: (b, jnp.maximum(order(n) * per - 1, 0), 0)),
            pl.BlockSpec((None, tile, rw), lambda b, n: (b, order(n), 0)),
            pl.BlockSpec((None, HALO, rw), lambda b, n: (b, jnp.minimum((order(n) + 1) * per, n_halo - 1), 0)),
        ]

    out_spec = lambda order: pl.BlockSpec((None, tile, rw), lambda b, n: (b, order(n), 0))
    perm, unperm = _segment_permutation(tile)
    consts = (conv_w, conv_b, wg, b_a, b_x, lam, perm, unperm)
    return pl.pallas_call(
        functools.partial(_rnn_core_kernel, n_ctx=n_ctx, n_all=n_all),
        out_shape=[jax.ShapeDtypeStruct((bsz, tt, rw), BF16)] * 2,
        grid=(bsz, n_all),
        in_specs=specs(fwd) + specs(bwd) + [_full_spec(c.shape) for c in consts],
        out_specs=[out_spec(fwd), out_spec(bwd)],
        scratch_shapes=[pltpu.VMEM((2, RNN_HEADS, 8, RNN_HD), F32)],
        compiler_params=_cparams("arbitrary", "arbitrary"),
        name="rglru_core",
    )(u, u, u, u, u, u, *consts)


def _residual(u, x, mod_ref, gp_ref, w_ref):
    y = _dot(u.astype(BF16), w_ref[...])
    return x + mod_ref[2:3] * (_rms(y) * gp_ref[...])


def _gla_post_part(x, mod_ref, gp_ref, w_ref, of_ref, ob_ref, z_ref, gh_ref):
    o = of_ref[...].astype(F32) + ob_ref[...].astype(F32)
    hv = o.shape[-1] // GLA_HEADS
    o = jnp.concatenate([_rms(o[:, h * hv:(h + 1) * hv]) for h in range(GLA_HEADS)], axis=1)
    u = o * gh_ref[...] * _silu(z_ref[...].astype(F32))
    return _residual(u, x, mod_ref, gp_ref, w_ref)


def _swa_post_part(x, mod_ref, gp_ref, w_ref, a_ref, z_ref):
    u = a_ref[...].astype(F32) * _silu(z_ref[...].astype(F32))
    return _residual(u, x, mod_ref, gp_ref, w_ref)


def _rnn_post_part(x, mod_ref, gp_ref, w_ref, hf_ref, hb_ref, z_ref):
    u = (hf_ref[...].astype(F32) + hb_ref[...].astype(F32)) * _silu(z_ref[...].astype(F32))
    return _residual(u, x, mod_ref, gp_ref, w_ref)


class _Pre(NamedTuple):
    name: str
    part: Callable
    mod: jax.Array
    g_pre: jax.Array
    consts: tuple
    row_consts: tuple
    widths: tuple
    dtypes: tuple
    stats: int


class _Post(NamedTuple):
    name: str
    part: Callable
    mod: jax.Array
    g_post: jax.Array
    w_out: jax.Array
    acts: tuple
    vecs: tuple


def _pre_kernel(*refs, part, n_x, n_ctx):
    part(_token_tile(refs[:n_x], n_ctx), *refs[n_x:])


def _post_kernel(*refs, part, n_x, n_ctx):
    refs[-1][...] = part(_token_tile(refs[:n_x], n_ctx), *refs[n_x:-1])


def _mid_kernel(*refs, post_part, n_post, pre_part, n_pre, n_x, n_ctx):
    post_in, pre_in = refs[n_x:n_x + n_post], refs[n_x + n_post:n_x + n_post + n_pre]
    xo_ref, pre_out = refs[n_x + n_post + n_pre], refs[n_x + n_post + n_pre + 1:]
    x = post_part(_token_tile(refs[:n_x], n_ctx), *post_in)
    xo_ref[...] = x
    pre_part(x, *pre_in, *pre_out)


def _stream_shape(xs):
    if isinstance(xs, tuple):
        bsz, n_ctx_rows, d = xs[0].shape
        return bsz, n_ctx_rows + xs[1].shape[1], d
    return xs.shape


def _pre_io(pre, bsz, tt, d, n_ctx):
    operands = (pre.mod, pre.g_pre, *pre.consts, *pre.row_consts)
    in_specs = ([_mod_spec(d, n_ctx, bsz), _full_spec(pre.g_pre.shape)]
                + [_full_spec(c.shape) for c in pre.consts]
                + [pl.BlockSpec((ROW_TILE, r.shape[-1]), lambda b, t: (t, 0)) for r in pre.row_consts])
    out_shape = ([jax.ShapeDtypeStruct((bsz, tt, w), dt) for w, dt in zip(pre.widths, pre.dtypes)]
                 + [jax.ShapeDtypeStruct((bsz, tt // ROW_TILE, 8, 128), F32)] * pre.stats)
    out_specs = ([_row_spec(w) for w in pre.widths]
                 + [pl.BlockSpec((None, None, 8, 128), lambda b, t: (b, t, 0, 0))] * pre.stats)
    return operands, in_specs, out_shape, out_specs


def _post_io(post, bsz, d, n_ctx, off):
    operands = (post.mod, post.g_post, post.w_out, *post.acts, *post.vecs)
    in_specs = ([_mod_spec(d, n_ctx, bsz, off), _full_spec(post.g_post.shape), _full_spec(post.w_out.shape)]
                + [_row_spec(a.shape[-1], off) for a in post.acts]
                + [_full_spec(v.shape) for v in post.vecs])
    return operands, in_specs


def _pre_call(pre, xs, n_ctx_rows):
    bsz, tt, d = _stream_shape(xs)
    n_ctx = n_ctx_rows // ROW_TILE
    streams = xs if isinstance(xs, tuple) else (xs,)
    operands, in_specs, out_shape, out_specs = _pre_io(pre, bsz, tt, d, n_ctx)
    return pl.pallas_call(
        functools.partial(_pre_kernel, part=pre.part, n_x=len(streams), n_ctx=n_ctx),
        out_shape=out_shape,
        grid=(bsz, tt // ROW_TILE),
        in_specs=_token_specs(xs, n_ctx) + in_specs,
        out_specs=out_specs,
        compiler_params=_cparams("parallel", "parallel"),
        name=pre.name,
    )(*streams, *operands)


def _post_call(post, xs, n_ctx_rows, latent_only):
    bsz, tt, d = _stream_shape(xs)
    n_ctx = n_ctx_rows // ROW_TILE
    off = n_ctx if latent_only else 0
    n_tiles = tt // ROW_TILE - off
    streams = xs if isinstance(xs, tuple) else (xs,)
    operands, in_specs = _post_io(post, bsz, d, n_ctx, off)
    return pl.pallas_call(
        functools.partial(_post_kernel, part=post.part, n_x=len(streams), n_ctx=n_ctx - off),
        out_shape=jax.ShapeDtypeStruct((bsz, n_tiles * ROW_TILE, d), F32),
        grid=(bsz, n_tiles),
        in_specs=_token_specs(xs, n_ctx, off) + in_specs,
        out_specs=_row_spec(d),
        compiler_params=_cparams("parallel", "parallel"),
        name=post.name,
    )(*streams, *operands)


def _mid_call(post, pre, xs, n_ctx_rows):
    bsz, tt, d = _stream_shape(xs)
    n_ctx = n_ctx_rows // ROW_TILE
    streams = xs if isinstance(xs, tuple) else (xs,)
    post_ops, post_specs = _post_io(post, bsz, d, n_ctx, 0)
    pre_ops, pre_specs, pre_shape, pre_out_specs = _pre_io(pre, bsz, tt, d, n_ctx)
    outs = pl.pallas_call(
        functools.partial(_mid_kernel, post_part=post.part, n_post=len(post_ops),
                          pre_part=pre.part, n_pre=len(pre_ops), n_x=len(streams), n_ctx=n_ctx),
        out_shape=[jax.ShapeDtypeStruct((bsz, tt, d), F32)] + pre_shape,
        grid=(bsz, tt // ROW_TILE),
        in_specs=_token_specs(xs, n_ctx) + post_specs + pre_specs,
        out_specs=[_row_spec(d)] + pre_out_specs,
        compiler_params=_cparams("parallel", "parallel"),
        name=post.name + "_" + pre.name,
    )(*streams, *post_ops, *pre_ops)
    return outs[0], outs[1:]


def _gla_layer(mod, g_pre, g_post, w_in, w_g1, w_g2, b_g, g_head, w_out, n_ctx_rows):
    d = w_in.shape[0]
    rank, dk = w_g2.shape[1], w_g2.shape[2]
    dv = w_out.shape[0]
    wg1 = jnp.zeros((d, GLA_GATE_PAD), F32).at[:, :2 * rank].set(jnp.concatenate([w_g1[0], w_g1[1]], axis=1))
    wg2 = jnp.zeros((GLA_GATE_PAD, 2 * dk), F32)
    wg2 = wg2.at[:rank, :dk].set(w_g2[0]).at[rank:2 * rank, dk:].set(w_g2[1])
    pre = _Pre("gla_pre", functools.partial(_gla_pre_part, dk=dk), mod, g_pre,
               (w_in.astype(BF16), wg1.astype(BF16), wg2.astype(BF16), b_g.reshape(1, 2 * dk)), (),
               (2 * dk, dv, dv, 2 * dk), (BF16, BF16, BF16, F32), 1)

    def mixer(outs):
        qk, v, z, la, peak = outs
        exact = (peak[:, :, 0, 0] * (GLA_BLOCK // 2) > GLA_SAFE_EXPONENT).astype(jnp.int32)
        o_f, o_b = _gla_core(qk, v, la, exact, n_ctx_rows)
        return _Post("gla_post", _gla_post_part, mod, g_post, w_out.astype(BF16), (o_f, o_b, z),
                     (g_head.reshape(1, dv),))

    return pre, mixer


def _swa_layer(mod, g_pre, g_post, w_in, sink, w_out, rope, n_ctx_rows):
    qw = w_out.shape[0]
    kw = (w_in.shape[1] - 2 * qw) // 2
    pre = _Pre("swa_pre", _swa_pre_part, mod, g_pre, (w_in.astype(BF16),), rope,
               (qw, 2 * kw, qw), (BF16, BF16, BF16), 0)

    def mixer(outs):
        q, kv, z = outs
        a = _swa_core(q, kv, sink, n_ctx_rows)
        return _Post("swa_post", _swa_post_part, mod, g_post, w_out.astype(BF16), (a, z), ())

    return pre, mixer


def _rnn_layer(mod, g_pre, g_post, w_in, conv_w, conv_b, w_ra, b_ra, w_ri, b_ri, lam, w_out, n_ctx_rows):
    rw = w_out.shape[0]
    pre = _Pre("rglru_pre", _rnn_pre_part, mod, g_pre, (w_in.astype(BF16),), (), (rw, rw), (BF16, BF16), 0)
    wg = (0.5 * jnp.concatenate([w_ra, w_ri], axis=-1)).astype(BF16)
    vec = lambda a: a.reshape(2, RNN_HEADS, 1, RNN_HD)

    def mixer(outs):
        u, z = outs
        h_f, h_b = _rnn_core(u, conv_w, conv_b.reshape(1, rw), wg, vec(0.5 * b_ra), vec(0.5 * b_ri), vec(lam),
                             n_ctx_rows)
        return _Post("rglru_post", _rnn_post_part, mod, g_post, w_out.astype(BF16), (h_f, h_b, z), ())

    return pre, mixer


def _rope_tables(t, n_ctx_rows):
    pairs = ATT_HEAD_DIM // 4
    pos = jnp.arange(t, dtype=jnp.int32)
    row = (pos // GRID_W).astype(F32)
    col = (pos % GRID_W).astype(F32)
    freqs = ROPE_BASE ** (-jnp.arange(pairs, dtype=F32) / pairs)
    ang_r, ang_c = row[:, None] * freqs, col[:, None] * freqs
    cos = jnp.concatenate([jnp.cos(ang_r)] * 2 + [jnp.cos(ang_c)] * 2, axis=1)
    sin = jnp.concatenate([-jnp.sin(ang_r), jnp.sin(ang_r), -jnp.sin(ang_c), jnp.sin(ang_c)], axis=1)
    cos = jnp.concatenate([jnp.ones((n_ctx_rows, ATT_HEAD_DIM), F32), cos], axis=0)
    sin = jnp.concatenate([jnp.zeros((n_ctx_rows, ATT_HEAD_DIM), F32), sin], axis=0)
    return jnp.tile(cos, (1, 2)), jnp.tile(sin, (1, 2))


def kernel(x, c, ctx, c_ctx, w_mod, b_mod, g_pre, g_post, a_w_in, a_w_g1, a_w_g2, a_b_g, a_g_head, a_w_out, b_w_in, b_sink, b_w_out, c_w_in, c_conv_w, c_conv_b, c_w_ra, c_b_ra, c_w_ri, c_b_ri, c_lam, c_w_out):
    bsz, t, d = x.shape
    n_ctx_rows = ctx.shape[1]
    depth = w_mod.shape[0]
    assert bsz < MOD_ROWS and n_ctx_rows % ROW_TILE == 0 and t % ROW_TILE == 0

    cvec = jnp.zeros((MOD_ROWS, d), F32).at[:bsz].set(c).at[bsz].set(c_ctx)
    mods = _modulation(cvec, w_mod, b_mod)
    rope = _rope_tables(t, n_ctx_rows)
    layers = []
    for i in range(depth):
        kind, j = i % N_MIXERS, i // N_MIXERS
        gpre, gpost = g_pre[i].reshape(1, d), g_post[i].reshape(1, d)
        if kind == 0:
            layers.append(_gla_layer(mods[i], gpre, gpost, a_w_in[j], a_w_g1[j], a_w_g2[j], a_b_g[j],
                                     a_g_head[j], a_w_out[j], n_ctx_rows))
        elif kind == 1:
            layers.append(_swa_layer(mods[i], gpre, gpost, b_w_in[j], b_sink[j], b_w_out[j], rope, n_ctx_rows))
        else:
            layers.append(_rnn_layer(mods[i], gpre, gpost, c_w_in[j], c_conv_w[j], c_conv_b[j], c_w_ra[j],
                                     c_b_ra[j], c_w_ri[j], c_b_ri[j], c_lam[j], c_w_out[j], n_ctx_rows))

    xs = (ctx, x)
    outs = _pre_call(layers[0][0], xs, n_ctx_rows)
    for i in range(depth - 1):
        xs, outs = _mid_call(layers[i][1](outs), layers[i + 1][0], xs, n_ctx_rows)
    return _post_call(layers[-1][1](outs), xs, n_ctx_rows, latent_only=True)
```

```python
import functools
from typing import Callable, NamedTuple

import jax
import jax.numpy as jnp
from jax import lax
from jax.experimental import pallas as pl
from jax.experimental.pallas import tpu as pltpu

F32 = jnp.float32
BF16 = jnp.bfloat16

EPS = 1e-6
GRID_W = 64
N_MIXERS = 3

GLA_HEADS = 4
GLA_TAU = 16.0
GLA_GATE_PAD = 128

ATT_HEAD_DIM = 64
ATT_KV_HEADS = 4
ATT_GROUP = 4
ATT_BLOCK = 128
ATT_ROWS = 32
ROPE_BASE = 10000.0
LOG2_E = 1.4426950408889634

RNN_HEADS = 10
RNN_HD = 128
CONV_W = 4
LRU_C = 8.0

ROW_TILE = 256
GLA_TILE = 256
GLA_CHUNK = 128
GLA_BLOCK = 64
GLA_SAFE_EXPONENT = 60.0
RNN_TILE = 256
HALO = 16
MOD_ROWS = 8
VMEM_LIMIT = 48 * 1024 * 1024


def _cparams(*sem):
    return pltpu.CompilerParams(dimension_semantics=sem, vmem_limit_bytes=VMEM_LIMIT)


def _sigmoid(x):
    return 0.5 * jnp.tanh(0.5 * x) + 0.5


def _silu(x):
    return x * _sigmoid(x)


def _softplus(x):
    return jnp.maximum(x, 0.0) + jnp.log1p(jnp.exp(-jnp.abs(x)))


def _log_sigmoid(x):
    return jnp.minimum(x, 0.0) - jnp.log(1.0 + jnp.exp(-jnp.abs(x)))


def _split3(a):
    hi = a.astype(BF16)
    r = a - hi.astype(F32)
    mid = r.astype(BF16)
    lo = (r - mid.astype(F32)).astype(BF16)
    return hi, mid, lo


def _dot(a, b):
    return jnp.dot(a, b, preferred_element_type=F32)


def _dot_nt(a, b):
    return lax.dot_general(a, b, (((1,), (1,)), ((), ())), preferred_element_type=F32)


def _dot_tn(a, b):
    return lax.dot_general(a, b, (((0,), (0,)), ((), ())), preferred_element_type=F32)


def _rms(x):
    return x * lax.rsqrt(jnp.mean(x * x, axis=-1, keepdims=True) + EPS)


def _mod_kernel(c_ref, w_ref, b_ref, o_ref):
    s = _silu(c_ref[...])
    s_hi, s_mid, s_lo = _split3(s)
    w_hi, w_mid, w_lo = _split3(w_ref[...])
    acc = _dot(s_hi, w_hi)
    acc += _dot(s_hi, w_mid) + _dot(s_mid, w_hi)
    acc += _dot(s_hi, w_lo) + _dot(s_mid, w_mid) + _dot(s_lo, w_hi)
    o_ref[...] = acc + b_ref[...]


def _modulation(cvec, w_mod, b_mod):
    depth, d, d3 = w_mod.shape
    out = pl.pallas_call(
        _mod_kernel,
        out_shape=jax.ShapeDtypeStruct((depth, MOD_ROWS, d3), F32),
        grid=(depth, d3 // d),
        in_specs=[
            pl.BlockSpec((MOD_ROWS, d), lambda i, j: (0, 0)),
            pl.BlockSpec((None, d, d), lambda i, j: (i, 0, j)),
            pl.BlockSpec((None, 1, d), lambda i, j: (i, 0, j)),
        ],
        out_specs=pl.BlockSpec((None, MOD_ROWS, d), lambda i, j: (i, 0, j)),
        compiler_params=_cparams("parallel", "parallel"),
        name="modulation",
    )(cvec, w_mod, b_mod.reshape(depth, 1, d3))
    return out.reshape(depth, MOD_ROWS, 3, d)


def _prenorm(x, g, mod):
    return _rms(x) * g * (1.0 + mod[1:2]) + mod[0:1]


def _mod_spec(d, n_ctx_tiles, ctx_row, offset=0):
    def index(b, t):
        return (jnp.where(t + offset < n_ctx_tiles, ctx_row, b), 0, 0)
    return pl.BlockSpec((None, 3, d), index)


def _row_spec(width, offset=0):
    return pl.BlockSpec((None, ROW_TILE, width), lambda b, t: (b, t + offset, 0))


def _token_specs(xs, n_ctx, offset=0):
    if not isinstance(xs, tuple):
        return [_row_spec(xs.shape[-1], offset)]
    d = xs[0].shape[-1]
    return [pl.BlockSpec((None, ROW_TILE, d), lambda b, t: (b, jnp.minimum(t + offset, n_ctx - 1), 0)),
            pl.BlockSpec((None, ROW_TILE, d), lambda b, t: (b, jnp.maximum(t + offset - n_ctx, 0), 0))]


def _token_tile(x_refs, n_ctx_tiles):
    if len(x_refs) == 1:
        return x_refs[0][...]
    return jnp.where(pl.program_id(1) < n_ctx_tiles, x_refs[0][...], x_refs[1][...])


def _full_spec(shape):
    zeros = (0,) * len(shape)
    return pl.BlockSpec(shape, lambda b, t: zeros)


def _gla_pre_part(x, mod_ref, g_ref, w_ref, wg1_ref, wg2_ref, bg_ref, qk_ref, v_ref, z_ref, la_ref, peak_ref,
                  *, dk):
    h = _prenorm(x, g_ref[...], mod_ref[...]).astype(BF16)
    hk = dk // GLA_HEADS
    dv = v_ref.shape[-1]
    g1 = _dot(h, wg1_ref[...]).astype(BF16)
    gate = _dot(g1, wg2_ref[...]) + bg_ref[...]
    la = _log_sigmoid(gate) / GLA_TAU
    la_ref[...] = la.astype(BF16)
    peak = jnp.max(jnp.max(-la, axis=1, keepdims=True), axis=0, keepdims=True)
    peak_ref[...] = jnp.broadcast_to(peak, peak_ref.shape)
    q = _dot(h, w_ref[:, :dk]) * hk ** -0.5
    qk_ref[:, :dk] = q.astype(BF16)
    qk_ref[:, dk:] = _dot(h, w_ref[:, dk:2 * dk]).astype(BF16)
    v_ref[...] = _dot(h, w_ref[:, 2 * dk:2 * dk + dv]).astype(BF16)
    z_ref[...] = _dot(h, w_ref[:, 2 * dk + dv:]).astype(BF16)


def _gla_core_kernel(exact_ref, qkf_ref, vf_ref, laf_ref, qkb_ref, vb_ref, lab_ref,
                     of_ref, ob_ref, st_ref, row_ref, *, dk, n_ctx, n_all):
    b, n = pl.program_id(0), pl.program_id(1)

    @pl.when(n == 0)
    def _():
        st_ref[...] = jnp.zeros_like(st_ref)

    io = ((qkf_ref, vf_ref, laf_ref, of_ref), (qkb_ref, vb_ref, lab_ref, ob_ref))
    n_sub = qkf_ref.shape[0] // GLA_CHUNK

    def run(exact):
        chunks = [_gla_chunk_pair(io, (s * GLA_CHUNK, (n_sub - 1 - s) * GLA_CHUNK), st_ref, dk,
                                  row_ref if exact else None) for s in range(n_sub)]
        for _ in range(3):
            for chunk in chunks:
                next(chunk, None)

    out_of_range = exact_ref[b, n] + exact_ref[b, _bwd_order(n, n_ctx, n_all)]
    pl.when(out_of_range == 0)(lambda: run(False))
    pl.when(out_of_range != 0)(lambda: run(True))


def _gla_exact_intra(d, q, k, v, b, row_ref):
    c, hk = q.shape
    row_ref[:, :hk] = k
    row_ref[:, hk:2 * hk] = b
    row_ref[:, 2 * hk:] = v.astype(F32)
    t_idx = lax.broadcasted_iota(jnp.int32, (c, 1), 0)

    def keys(g, acc):
        group = row_ref[pl.ds(pl.multiple_of(g * 8, 8), 8), :]
        for r in range(8):
            s = g * 8 + r
            k_s, b_s, v_s = group[r:r + 1, :hk], group[r:r + 1, hk:2 * hk], group[r:r + 1, 2 * hk:]
            decay = jnp.exp(jnp.minimum(b - b_s, 0.0))
            col = jnp.sum(q * decay * k_s, axis=1, keepdims=True)
            causal = (t_idx >= s) if d == 0 else (t_idx <= s)
            acc = acc + jnp.where(causal, col, 0.0) * v_s
        return acc

    return lax.fori_loop(0, c // 8, keys, jnp.zeros((c, v.shape[1]), F32))


def _gla_chunk_pair(io, base, st_ref, dk, row_ref):
    c = GLA_CHUNK
    hk = dk // GLA_HEADS
    hv = io[0][1].shape[-1] // GLA_HEADS
    row = lax.broadcasted_iota(jnp.int32, (c, c), 0)
    col = lax.broadcasted_iota(jnp.int32, (c, c), 1)
    blk = GLA_BLOCK
    keeps = (col <= row, col >= row)
    lasts = (c - 1, 0)
    tile = [slice(r0, r0 + c) for r0 in base]
    cums = []
    for d, (_, _, la_ref, _) in enumerate(io):
        tri = jnp.where(keeps[d], 1.0, 0.0).astype(BF16)
        cums.append(_dot(tri, la_ref[tile[d], d * dk:(d + 1) * dk]))
    yield

    stage = {}
    for d, (qk_ref, v_ref, _, _) in enumerate(io):
        for h in range(GLA_HEADS):
            b = cums[d][:, h * hk:(h + 1) * hk]
            b_last = b[lasts[d]:lasts[d] + 1]
            q = qk_ref[tile[d], h * hk:(h + 1) * hk].astype(F32)
            k = qk_ref[tile[d], dk + h * hk:dk + (h + 1) * hk].astype(F32)
            v = v_ref[tile[d], h * hv:(h + 1) * hv]
            scores = []
            for i in (range(c // blk) if row_ref is None else ()):
                rows = slice(i * blk, (i + 1) * blk)
                keys = slice(0, (i + 1) * blk) if d == 0 else slice(i * blk, c)
                b_ref = b[i * blk + blk // 2:i * blk + blk // 2 + 1]
                qe = (q[rows] * jnp.exp(b[rows] - b_ref)).astype(BF16)
                ke = (k[keys] * jnp.exp(b_ref - b[keys])).astype(BF16)
                scores.append((rows, keys, _dot_nt(qe, ke)))
            qs = (q * jnp.exp(b)).astype(BF16)
            kd = (k * jnp.exp(b_last - b)).astype(BF16)
            st = st_ref[d, h]
            inter = _dot_nt(qs, st.astype(BF16))
            st_ref[d, h] = st * jnp.exp(b_last) + _dot_tn(v, kd)
            stage[d, h] = (scores, inter, v, (q, k, b))
    yield

    for d, (_, _, _, o_ref) in enumerate(io):
        for h in range(GLA_HEADS):
            scores, inter, v, (q, k, b) = stage[d, h]
            if row_ref is not None:
                o = _gla_exact_intra(d, q, k, v, b, row_ref) + inter
                o_ref[tile[d], h * hv:(h + 1) * hv] = o.astype(BF16)
            for rows, keys, s in scores:
                t_idx = lax.broadcasted_iota(jnp.int32, s.shape, 0) + rows.start
                s_idx = lax.broadcasted_iota(jnp.int32, s.shape, 1) + keys.start
                p = jnp.where((s_idx <= t_idx) if d == 0 else (s_idx >= t_idx), s, 0.0).astype(BF16)
                o = _dot(p, v[keys]) + inter[rows]
                o_ref[base[d] + rows.start:base[d] + rows.stop, h * hv:(h + 1) * hv] = o.astype(BF16)


def _bwd_order(n, n_ctx, n_all):
    return jnp.where(n < n_ctx, n_ctx - 1 - n, n_all - 1 - (n - n_ctx))


def _gla_core(qk, v, la, exact, n_ctx_rows):
    bsz, tt, dv = v.shape
    dk = qk.shape[-1] // 2
    c = GLA_TILE
    n_all, n_ctx = tt // c, n_ctx_rows // c
    hk, hv = dk // GLA_HEADS, dv // GLA_HEADS
    fwd = lambda b, n: (b, n, 0)
    bwd = lambda b, n: (b, _bwd_order(n, n_ctx, n_all), 0)
    spec = lambda w, idx: pl.BlockSpec((None, c, w), idx)
    return pl.pallas_call(
        functools.partial(_gla_core_kernel, dk=dk, n_ctx=n_ctx, n_all=n_all),
        out_shape=[jax.ShapeDtypeStruct((bsz, tt, dv), BF16)] * 2,
        grid=(bsz, n_all),
        in_specs=[pl.BlockSpec(memory_space=pltpu.SMEM),
                  spec(2 * dk, fwd), spec(dv, fwd), spec(2 * dk, fwd),
                  spec(2 * dk, bwd), spec(dv, bwd), spec(2 * dk, bwd)],
        out_specs=[spec(dv, fwd), spec(dv, bwd)],
        scratch_shapes=[pltpu.VMEM((2, GLA_HEADS, hv, hk), F32),
                        pltpu.VMEM((GLA_CHUNK, 2 * hk + hv), F32)],
        compiler_params=_cparams("arbitrary", "arbitrary"),
        name="gla_core",
    )(exact, qk, v, la, qk, v, la)


def _rope(x, cos, sin_signed):
    lane = lax.broadcasted_iota(jnp.int32, (x.shape[0], 128), 1)
    first = (lane % 32) < 16
    out = []
    for j in range(x.shape[1] // 128):
        xj = x[:, j * 128:(j + 1) * 128]
        partner = jnp.where(first, pltpu.roll(xj, 112, 1), pltpu.roll(xj, 16, 1))
        out.append(xj * cos + partner * sin_signed)
    return jnp.concatenate(out, axis=1)


def _swa_pre_part(x, mod_ref, g_ref, w_ref, cos_ref, sin_ref, q_ref, kv_ref, z_ref):
    h = _prenorm(x, g_ref[...], mod_ref[...]).astype(BF16)
    qw = q_ref.shape[-1]
    kw = kv_ref.shape[-1] // 2
    cos, sin = cos_ref[...], sin_ref[...]
    q = _rope(_dot(h, w_ref[:, :qw]), cos, sin) * (ATT_HEAD_DIM ** -0.5 * LOG2_E)
    q_ref[...] = q.astype(BF16)
    kv_ref[:, :kw] = _rope(_dot(h, w_ref[:, qw:qw + kw]), cos, sin).astype(BF16)
    kv_ref[:, kw:] = _dot(h, w_ref[:, qw + kw:qw + 2 * kw]).astype(BF16)
    z_ref[...] = _dot(h, w_ref[:, qw + 2 * kw:]).astype(BF16)


def _swa_core_kernel(sink_ref, q_ref, kvp_ref, kvc_ref, kvn_ref, kvx_ref, o_ref, *, n_ctx, n_all):
    t = pl.program_id(1)
    blk = ATT_BLOCK
    n_keys = 3 * blk + kvx_ref.shape[0]
    kw = kvc_ref.shape[-1] // 2
    n_loc = 3 * blk
    i = lax.broadcasted_iota(jnp.int32, (blk, n_loc), 0)
    j = lax.broadcasted_iota(jnp.int32, (blk, n_loc), 1)
    never = n_keys
    off_prev = jnp.where(t > n_ctx, 0, never)
    off_cur = jnp.where(t >= n_ctx, 0, never)
    off_next = jnp.where(jnp.logical_and(t >= n_ctx, t < n_all - 1), 0, never)
    valid = (((j >= i + off_prev) & (j < blk))
             | ((j >= blk + off_cur) & (j < 2 * blk))
             | ((j >= 2 * blk) & (j <= i + 2 * blk - off_next)))
    bias = jnp.where(valid, 0.0, -jnp.inf)
    logits = []
    for h in range(ATT_KV_HEADS):
        ks = slice(h * ATT_HEAD_DIM, (h + 1) * ATT_HEAD_DIM)
        k = jnp.concatenate([kvp_ref[:, ks], kvc_ref[:, ks], kvn_ref[:, ks], kvx_ref[:, ks]], axis=0)
        q = jnp.concatenate(
            [q_ref[:, (h * ATT_GROUP + g) * ATT_HEAD_DIM:(h * ATT_GROUP + g + 1) * ATT_HEAD_DIM]
             for g in range(ATT_GROUP)], axis=0)
        logits.append(_dot_nt(q, k))
    chunks = [(h, r0) for h in range(ATT_KV_HEADS) for r0 in range(0, ATT_GROUP * blk, ATT_ROWS)]

    def slabs(h, r0):
        out = []
        for c0 in range(0, n_keys, 128):
            s = logits[h][r0:r0 + ATT_ROWS, c0:c0 + 128]
            if c0 < n_loc:
                s = s + bias[r0 % blk:r0 % blk + ATT_ROWS, c0:c0 + 128]
            out.append(s)
        return out

    sinks = {(h, r0): sink_ref[h * ATT_GROUP + r0 // blk] * LOG2_E for h, r0 in chunks}
    maxes = {}
    for h, r0 in chunks:
        lane_max = functools.reduce(jnp.maximum, slabs(h, r0))
        maxes[h, r0] = jnp.maximum(jnp.max(lane_max, axis=-1, keepdims=True), sinks[h, r0])
    head_lane = lax.broadcasted_iota(jnp.int32, (n_keys, 2 * ATT_HEAD_DIM), 1) // ATT_HEAD_DIM
    for h in range(ATT_KV_HEADS):
        vs = slice(kw + (h // 2) * 2 * ATT_HEAD_DIM, kw + (h // 2 + 1) * 2 * ATT_HEAD_DIM)
        v = jnp.concatenate([kvp_ref[:, vs], kvc_ref[:, vs], kvn_ref[:, vs], kvx_ref[:, vs]], axis=0)
        v = jnp.where(head_lane == h % 2, v, jnp.ones_like(v))
        probs, sink_terms = [], []
        for _, r0 in chunks[:len(chunks) // ATT_KV_HEADS]:
            p = [jnp.exp2(s - maxes[h, r0]) for s in slabs(h, r0)]
            probs.append(jnp.concatenate(p, axis=1).astype(BF16))
            sink_terms.append(jnp.exp2(sinks[h, r0] - maxes[h, r0]))
        pv = _dot(jnp.concatenate(probs, axis=0), v)
        own, other = (h % 2) * ATT_HEAD_DIM, (1 - h % 2) * ATT_HEAD_DIM
        denom = pv[:, other:other + 1] + jnp.concatenate(sink_terms, axis=0)
        o = pv[:, own:own + ATT_HEAD_DIM] * (1.0 / denom)
        for g in range(ATT_GROUP):
            lo = (h * ATT_GROUP + g) * ATT_HEAD_DIM
            o_ref[:, lo:lo + ATT_HEAD_DIM] = o[g * blk:(g + 1) * blk].astype(BF16)


def _swa_core(q, kv, sink, n_ctx_rows):
    bsz, tt, qw = q.shape
    blk = ATT_BLOCK
    n_all, n_ctx = tt // blk, n_ctx_rows // blk
    kvw = kv.shape[-1]
    kv_spec = lambda idx: pl.BlockSpec((None, blk, kvw), idx)
    return pl.pallas_call(
        functools.partial(_swa_core_kernel, n_ctx=n_ctx, n_all=n_all),
        out_shape=jax.ShapeDtypeStruct((bsz, tt, qw), BF16),
        grid=(bsz, n_all),
        in_specs=[
            pl.BlockSpec(memory_space=pltpu.SMEM),
            pl.BlockSpec((None, blk, qw), lambda b, t: (b, t, 0)),
            kv_spec(lambda b, t: (b, jnp.maximum(t - 1, 0), 0)),
            kv_spec(lambda b, t: (b, t, 0)),
            kv_spec(lambda b, t: (b, jnp.minimum(t + 1, n_all - 1), 0)),
            pl.BlockSpec((None, n_ctx_rows, kvw), lambda b, t: (b, 0, 0)),
        ],
        out_specs=pl.BlockSpec((None, blk, qw), lambda b, t: (b, t, 0)),
        compiler_params=_cparams("parallel", "parallel"),
        name="swa_core",
    )(sink, q, kv, kv, kv, kv)


def _segment_permutation(n):
    r = jnp.arange(n)
    p = (r[None, :] == ((r % 8) * (n // 8) + r // 8)[:, None]).astype(BF16)
    return p, p.T


def _rnn_pre_part(x, mod_ref, g_ref, w_ref, u_ref, z_ref):
    h = _prenorm(x, g_ref[...], mod_ref[...]).astype(BF16)
    rw = u_ref.shape[-1]
    u_ref[...] = _dot(h, w_ref[:, :rw]).astype(BF16)
    z_ref[...] = _dot(h, w_ref[:, rw:]).astype(BF16)


def _shift_rows(cur, other, up):
    sub = lax.broadcasted_iota(jnp.int32, cur.shape, 0)
    if up:
        return jnp.where(sub == 7, pltpu.roll(other, 7, 0), pltpu.roll(cur, 7, 0))
    return jnp.where(sub == 0, pltpu.roll(other, 1, 0), pltpu.roll(cur, 1, 0))


def _segment_scan(a, x, carry, reverse):
    n_g = a.shape[0] // 8
    order = range(n_g - 1, -1, -1) if reverse else range(n_g)
    hs, ps = [None] * n_g, [None] * n_g
    h = p = None
    for g in order:
        ag, xg = a[8 * g:8 * g + 8], x[8 * g:8 * g + 8]
        h = xg if h is None else ag * h + xg
        p = ag if p is None else ag * p
        hs[g], ps[g] = h, p
    sub = lax.broadcasted_iota(jnp.int32, h.shape, 0)
    for s in (1, 2, 4):
        shift = 8 - s if reverse else s
        inside = (sub < 8 - s) if reverse else (sub >= s)
        h = h + p * jnp.where(inside, pltpu.roll(h, shift, 0), 0.0)
        p = p * jnp.where(inside, pltpu.roll(p, shift, 0), 1.0)
    full = h + p * carry
    first, last = (7, 0) if reverse else (0, 7)
    seg_in = jnp.where(sub == first, carry, pltpu.roll(full, 7 if reverse else 1, 0))
    out = jnp.concatenate([hs[g] + ps[g] * seg_in for g in range(n_g)], axis=0)
    return out, jnp.broadcast_to(full[last:last + 1], full.shape)


def _rnn_core_kernel(upf_ref, ucf_ref, unf_ref, upb_ref, ucb_ref, unb_ref,
                     cw_ref, cb_ref, wg_ref, ba_ref, bx_ref, lam_ref,
                     perm_ref, unperm_ref, hf_ref, hb_ref, carry_ref, *, n_ctx, n_all):
    n = pl.program_id(1)

    @pl.when(n == 0)
    def _():
        carry_ref[...] = jnp.zeros_like(carry_ref)

    tiles = (n, _bwd_order(n, n_ctx, n_all))
    prev_ok = [jnp.where(jnp.logical_and(t != 0, t != n_ctx), 1.0, 0.0) for t in tiles]
    next_ok = [jnp.where(jnp.logical_and(t != n_ctx - 1, t != n_all - 1), 1.0, 0.0) for t in tiles]
    refs = ((upf_ref, ucf_ref, unf_ref, hf_ref), (upb_ref, ucb_ref, unb_ref, hb_ref))

    useg = [_dot(perm_ref[...], refs[d][1][...]) for d in range(2)]
    rows = useg[0].shape[0]
    convs, pres = {}, {}
    for d, (p_ref, _, n_ref, _) in enumerate(refs):
        before = p_ref[...].astype(F32)[HALO - 8:] * prev_ok[d]
        after = n_ref[...].astype(F32)[:8] * next_ok[d]
        before2 = pltpu.roll(before, 1, 0)
        for j in range(RNN_HEADS):
            lanes = slice(j * RNN_HD, (j + 1) * RNN_HD)
            u = useg[d][:, lanes]
            ext = jnp.concatenate([
                _shift_rows(u[rows - 16:rows - 8], before2[:, lanes], up=False),
                _shift_rows(u[rows - 8:], before[:, lanes], up=False),
                u,
                _shift_rows(u[:8], after[:, lanes], up=True)], axis=0)
            conv = cb_ref[:, lanes]
            for tap in range(CONV_W):
                conv = conv + ext[8 * tap:8 * tap + rows] * cw_ref[tap:tap + 1, lanes]
            convs[d, j] = conv
            pres[d, j] = _dot(conv.astype(BF16), wg_ref[d, j])

    half = RNN_HEADS // 2
    for j0 in (0, half):
        for d in range(2):
            outs = []
            for j in range(j0, j0 + half):
                pre, conv = pres[d, j], convs[d, j]
                u = jnp.tanh(pre[:, :RNN_HD] + ba_ref[d, j]) + 1.0
                i = 0.5 * jnp.tanh(pre[:, RNN_HD:] + bx_ref[d, j]) + 0.5
                rate = (0.5 * LRU_C) * _softplus(-lam_ref[d, j])
                a = jnp.exp2((-LOG2_E * rate) * u)
                y = jnp.tanh(rate * u) * (1.0 + a * a)
                x = jnp.where(y > 0.0, y * lax.rsqrt(y), 0.0) * i * conv
                hs, carry = _segment_scan(a, x, carry_ref[d, j], reverse=(d == 1))
                carry_ref[d, j] = carry
                outs.append(hs.astype(BF16))
            h = _dot(unperm_ref[...], jnp.concatenate(outs, axis=1))
            refs[d][3][:, j0 * RNN_HD:(j0 + half) * RNN_HD] = h.astype(BF16)


def _rnn_core(u, conv_w, conv_b, wg, b_a, b_x, lam, n_ctx_rows):
    bsz, tt, rw = u.shape
    tile = RNN_TILE
    n_all, n_ctx = tt // tile, n_ctx_rows // tile
    per = tile // HALO
    n_halo = tt // HALO
    fwd = lambda n: n
    bwd = lambda n: _bwd_order(n, n_ctx, n_all)

    def specs(order):
        return [
            pl.BlockSpec((None, HALO, rw), lambda b, n: (b, jnp.maximum(order(n) * per - 1, 0), 0)),
            pl.BlockSpec((None, tile, rw), lambda b, n: (b, order(n), 0)),
            pl.BlockSpec((None, HALO, rw), lambda b, n: (b, jnp.minimum((order(n) + 1) * per, n_halo - 1), 0)),
        ]

    out_spec = lambda order: pl.BlockSpec((None, tile, rw), lambda b, n: (b, order(n), 0))
    perm, unperm = _segment_permutation(tile)
    consts = (conv_w, conv_b, wg, b_a, b_x, lam, perm, unperm)
    return pl.pallas_call(
        functools.partial(_rnn_core_kernel, n_ctx=n_ctx, n_all=n_all),
        out_shape=[jax.ShapeDtypeStruct((bsz, tt, rw), BF16)] * 2,
        grid=(bsz, n_all),
        in_specs=specs(fwd) + specs(bwd) + [_full_spec(c.shape) for c in consts],
        out_specs=[out_spec(fwd), out_spec(bwd)],
        scratch_shapes=[pltpu.VMEM((2, RNN_HEADS, 8, RNN_HD), F32)],
        compiler_params=_cparams("arbitrary", "arbitrary"),
        name="rglru_core",
    )(u, u, u, u, u, u, *consts)


def _residual(u, x, mod_ref, gp_ref, w_ref):
    y = _dot(u.astype(BF16), w_ref[...])
    return x + mod_ref[2:3] * (_rms(y) * gp_ref[...])


def _gla_post_part(x, mod_ref, gp_ref, w_ref, of_ref, ob_ref, z_ref, gh_ref, rows=slice(None)):
    o = of_ref[rows].astype(F32) + ob_ref[rows].astype(F32)
    hv = o.shape[-1] // GLA_HEADS
    o = jnp.concatenate([_rms(o[:, h * hv:(h + 1) * hv]) for h in range(GLA_HEADS)], axis=1)
    u = o * gh_ref[...] * _silu(z_ref[rows].astype(F32))
    return _residual(u, x, mod_ref, gp_ref, w_ref)


def _swa_post_part(x, mod_ref, gp_ref, w_ref, a_ref, z_ref, rows=slice(None)):
    u = a_ref[rows].astype(F32) * _silu(z_ref[rows].astype(F32))
    return _residual(u, x, mod_ref, gp_ref, w_ref)


def _rnn_post_part(x, mod_ref, gp_ref, w_ref, hf_ref, hb_ref, z_ref, rows=slice(None)):
    u = (hf_ref[rows].astype(F32) + hb_ref[rows].astype(F32)) * _silu(z_ref[rows].astype(F32))
    return _residual(u, x, mod_ref, gp_ref, w_ref)


class _Pre(NamedTuple):
    name: str
    part: Callable
    mod: jax.Array
    g_pre: jax.Array
    consts: tuple
    row_consts: tuple
    widths: tuple
    dtypes: tuple
    stats: int


class _Post(NamedTuple):
    name: str
    part: Callable
    mod: jax.Array
    g_post: jax.Array
    w_out: jax.Array
    acts: tuple
    vecs: tuple


def _pre_kernel(*refs, part, n_x, n_ctx):
    part(_token_tile(refs[:n_x], n_ctx), *refs[n_x:])


def _post_kernel(*refs, part, n_x, n_ctx):
    refs[-1][...] = part(_token_tile(refs[:n_x], n_ctx), *refs[n_x:-1])


def _mid_kernel(*refs, post_part, n_post, pre_part, n_pre, n_x, n_ctx):
    post_in, pre_in = refs[n_x:n_x + n_post], refs[n_x + n_post:n_x + n_post + n_pre]
    xo_ref, pre_out = refs[n_x + n_post + n_pre], refs[n_x + n_post + n_pre + 1:]
    x_in = _token_tile(refs[:n_x], n_ctx)
    half = x_in.shape[0] // 2
    x = jnp.concatenate([post_part(x_in[rows], *post_in, rows=rows)
                         for rows in (slice(0, half), slice(half, 2 * half))], axis=0)
    xo_ref[...] = x
    pre_part(x, *pre_in, *pre_out)


def _stream_shape(xs):
    if isinstance(xs, tuple):
        bsz, n_ctx_rows, d = xs[0].shape
        return bsz, n_ctx_rows + xs[1].shape[1], d
    return xs.shape


def _pre_io(pre, bsz, tt, d, n_ctx):
    operands = (pre.mod, pre.g_pre, *pre.consts, *pre.row_consts)
    in_specs = ([_mod_spec(d, n_ctx, bsz), _full_spec(pre.g_pre.shape)]
                + [_full_spec(c.shape) for c in pre.consts]
                + [pl.BlockSpec((ROW_TILE, r.shape[-1]), lambda b, t: (t, 0)) for r in pre.row_consts])
    out_shape = ([jax.ShapeDtypeStruct((bsz, tt, w), dt) for w, dt in zip(pre.widths, pre.dtypes)]
                 + [jax.ShapeDtypeStruct((bsz, tt // ROW_TILE, 8, 128), F32)] * pre.stats)
    out_specs = ([_row_spec(w) for w in pre.widths]
                 + [pl.BlockSpec((None, None, 8, 128), lambda b, t: (b, t, 0, 0))] * pre.stats)
    return operands, in_specs, out_shape, out_specs


def _post_io(post, bsz, d, n_ctx, off):
    operands = (post.mod, post.g_post, post.w_out, *post.acts, *post.vecs)
    in_specs = ([_mod_spec(d, n_ctx, bsz, off), _full_spec(post.g_post.shape), _full_spec(post.w_out.shape)]
                + [_row_spec(a.shape[-1], off) for a in post.acts]
                + [_full_spec(v.shape) for v in post.vecs])
    return operands, in_specs


def _pre_call(pre, xs, n_ctx_rows):
    bsz, tt, d = _stream_shape(xs)
    n_ctx = n_ctx_rows // ROW_TILE
    streams = xs if isinstance(xs, tuple) else (xs,)
    operands, in_specs, out_shape, out_specs = _pre_io(pre, bsz, tt, d, n_ctx)
    return pl.pallas_call(
        functools.partial(_pre_kernel, part=pre.part, n_x=len(streams), n_ctx=n_ctx),
        out_shape=out_shape,
        grid=(bsz, tt // ROW_TILE),
        in_specs=_token_specs(xs, n_ctx) + in_specs,
        out_specs=out_specs,
        compiler_params=_cparams("parallel", "parallel"),
        name=pre.name,
    )(*streams, *operands)


def _post_call(post, xs, n_ctx_rows, latent_only):
    bsz, tt, d = _stream_shape(xs)
    n_ctx = n_ctx_rows // ROW_TILE
    off = n_ctx if latent_only else 0
    n_tiles = tt // ROW_TILE - off
    streams = xs if isinstance(xs, tuple) else (xs,)
    operands, in_specs = _post_io(post, bsz, d, n_ctx, off)
    return pl.pallas_call(
        functools.partial(_post_kernel, part=post.part, n_x=len(streams), n_ctx=n_ctx - off),
        out_shape=jax.ShapeDtypeStruct((bsz, n_tiles * ROW_TILE, d), F32),
        grid=(bsz, n_tiles),
        in_specs=_token_specs(xs, n_ctx, off) + in_specs,
        out_specs=_row_spec(d),
        compiler_params=_cparams("parallel", "parallel"),
        name=post.name,
    )(*streams, *operands)


def _mid_call(post, pre, xs, n_ctx_rows):
    bsz, tt, d = _stream_shape(xs)
    n_ctx = n_ctx_rows // ROW_TILE
    streams = xs if isinstance(xs, tuple) else (xs,)
    post_ops, post_specs = _post_io(post, bsz, d, n_ctx, 0)
    pre_ops, pre_specs, pre_shape, pre_out_specs = _pre_io(pre, bsz, tt, d, n_ctx)
    outs = pl.pallas_call(
        functools.partial(_mid_kernel, post_part=post.part, n_post=len(post_ops),
                          pre_part=pre.part, n_pre=len(pre_ops), n_x=len(streams), n_ctx=n_ctx),
        out_shape=[jax.ShapeDtypeStruct((bsz, tt, d), F32)] + pre_shape,
        grid=(bsz, tt // ROW_TILE),
        in_specs=_token_specs(xs, n_ctx) + post_specs + pre_specs,
        out_specs=[_row_spec(d)] + pre_out_specs,
        compiler_params=_cparams("parallel", "parallel"),
        name=post.name + "_" + pre.name,
    )(*streams, *post_ops, *pre_ops)
    return outs[0], outs[1:]


def _gla_layer(mod, g_pre, g_post, w_in, w_g1, w_g2, b_g, g_head, w_out, n_ctx_rows):
    d = w_in.shape[0]
    rank, dk = w_g2.shape[1], w_g2.shape[2]
    dv = w_out.shape[0]
    wg1 = jnp.zeros((d, GLA_GATE_PAD), F32).at[:, :2 * rank].set(jnp.concatenate([w_g1[0], w_g1[1]], axis=1))
    wg2 = jnp.zeros((GLA_GATE_PAD, 2 * dk), F32)
    wg2 = wg2.at[:rank, :dk].set(w_g2[0]).at[rank:2 * rank, dk:].set(w_g2[1])
    pre = _Pre("gla_pre", functools.partial(_gla_pre_part, dk=dk), mod, g_pre,
               (w_in.astype(BF16), wg1.astype(BF16), wg2.astype(BF16), b_g.reshape(1, 2 * dk)), (),
               (2 * dk, dv, dv, 2 * dk), (BF16, BF16, BF16, BF16), 1)

    def mixer(outs):
        qk, v, z, la, peak = outs
        exact = (peak[:, :, 0, 0] * (GLA_BLOCK // 2) > GLA_SAFE_EXPONENT).astype(jnp.int32)
        o_f, o_b = _gla_core(qk, v, la, exact, n_ctx_rows)
        return _Post("gla_post", _gla_post_part, mod, g_post, w_out.astype(BF16), (o_f, o_b, z),
                     (g_head.reshape(1, dv),))

    return pre, mixer


def _swa_layer(mod, g_pre, g_post, w_in, sink, w_out, rope, n_ctx_rows):
    qw = w_out.shape[0]
    kw = (w_in.shape[1] - 2 * qw) // 2
    pre = _Pre("swa_pre", _swa_pre_part, mod, g_pre, (w_in.astype(BF16),), rope,
               (qw, 2 * kw, qw), (BF16, BF16, BF16), 0)

    def mixer(outs):
        q, kv, z = outs
        a = _swa_core(q, kv, sink, n_ctx_rows)
        return _Post("swa_post", _swa_post_part, mod, g_post, w_out.astype(BF16), (a, z), ())

    return pre, mixer


def _rnn_layer(mod, g_pre, g_post, w_in, conv_w, conv_b, w_ra, b_ra, w_ri, b_ri, lam, w_out, n_ctx_rows):
    rw = w_out.shape[0]
    pre = _Pre("rglru_pre", _rnn_pre_part, mod, g_pre, (w_in.astype(BF16),), (), (rw, rw), (BF16, BF16), 0)
    wg = (0.5 * jnp.concatenate([w_ra, w_ri], axis=-1)).astype(BF16)
    vec = lambda a: a.reshape(2, RNN_HEADS, 1, RNN_HD)

    def mixer(outs):
        u, z = outs
        h_f, h_b = _rnn_core(u, conv_w, conv_b.reshape(1, rw), wg, vec(0.5 * b_ra), vec(0.5 * b_ri), vec(lam),
                             n_ctx_rows)
        return _Post("rglru_post", _rnn_post_part, mod, g_post, w_out.astype(BF16), (h_f, h_b, z), ())

    return pre, mixer


def _rope_tables(t, n_ctx_rows):
    pairs = ATT_HEAD_DIM // 4
    pos = jnp.arange(t, dtype=jnp.int32)
    row = (pos // GRID_W).astype(F32)
    col = (pos % GRID_W).astype(F32)
    freqs = ROPE_BASE ** (-jnp.arange(pairs, dtype=F32) / pairs)
    ang_r, ang_c = row[:, None] * freqs, col[:, None] * freqs
    cos = jnp.concatenate([jnp.cos(ang_r)] * 2 + [jnp.cos(ang_c)] * 2, axis=1)
    sin = jnp.concatenate([-jnp.sin(ang_r), jnp.sin(ang_r), -jnp.sin(ang_c), jnp.sin(ang_c)], axis=1)
    cos = jnp.concatenate([jnp.ones((n_ctx_rows, ATT_HEAD_DIM), F32), cos], axis=0)
    sin = jnp.concatenate([jnp.zeros((n_ctx_rows, ATT_HEAD_DIM), F32), sin], axis=0)
    return jnp.tile(cos, (1, 2)), jnp.tile(sin, (1, 2))


def kernel(x, c, ctx, c_ctx, w_mod, b_mod, g_pre, g_post, a_w_in, a_w_g1, a_w_g2, a_b_g, a_g_head, a_w_out, b_w_in, b_sink, b_w_out, c_w_in, c_conv_w, c_conv_b, c_w_ra, c_b_ra, c_w_ri, c_b_ri, c_lam, c_w_out):
    bsz, t, d = x.shape
    n_ctx_rows = ctx.shape[1]
    depth = w_mod.shape[0]
    assert bsz < MOD_ROWS and n_ctx_rows % ROW_TILE == 0 and t % ROW_TILE == 0

    cvec = jnp.zeros((MOD_ROWS, d), F32).at[:bsz].set(c).at[bsz].set(c_ctx)
    mods = _modulation(cvec, w_mod, b_mod)
    rope = _rope_tables(t, n_ctx_rows)
    layers = []
    for i in range(depth):
        kind, j = i % N_MIXERS, i // N_MIXERS
        gpre, gpost = g_pre[i].reshape(1, d), g_post[i].reshape(1, d)
        if kind == 0:
            layers.append(_gla_layer(mods[i], gpre, gpost, a_w_in[j], a_w_g1[j], a_w_g2[j], a_b_g[j],
                                     a_g_head[j], a_w_out[j], n_ctx_rows))
        elif kind == 1:
            layers.append(_swa_layer(mods[i], gpre, gpost, b_w_in[j], b_sink[j], b_w_out[j], rope, n_ctx_rows))
        else:
            layers.append(_rnn_layer(mods[i], gpre, gpost, c_w_in[j], c_conv_w[j], c_conv_b[j], c_w_ra[j],
                                     c_b_ra[j], c_w_ri[j], c_b_ri[j], c_lam[j], c_w_out[j], n_ctx_rows))

    xs = (ctx, x)
    outs = _pre_call(layers[0][0], xs, n_ctx_rows)
    for i in range(depth - 1):
        xs, outs = _mid_call(layers[i][1](outs), layers[i + 1][0], xs, n_ctx_rows)
    return _post_call(layers[-1][1](outs), xs, n_ctx_rows, latent_only=True)
```

```python
import functools
from typing import Callable, NamedTuple

import jax
import jax.numpy as jnp
from jax import lax
from jax.experimental import pallas as pl
from jax.experimental.pallas import tpu as pltpu

F32 = jnp.float32
BF16 = jnp.bfloat16

EPS = 1e-6
GRID_W = 64
N_MIXERS = 3

GLA_HEADS = 4
GLA_TAU = 16.0
GLA_GATE_PAD = 128

ATT_HEAD_DIM = 64
ATT_KV_HEADS = 4
ATT_GROUP = 4
ATT_BLOCK = 128
ATT_ROWS = 32
ROPE_BASE = 10000.0
LOG2_E = 1.4426950408889634

RNN_HEADS = 10
RNN_HD = 128
CONV_W = 4
LRU_C = 8.0

ROW_TILE = 256
GLA_TILE = 256
GLA_CHUNK = 128
GLA_BLOCK = 64
GLA_SAFE_EXPONENT = 60.0
RNN_TILE = 256
HALO = 16
MOD_ROWS = 8
VMEM_LIMIT = 48 * 1024 * 1024


def _cparams(*sem):
    return pltpu.CompilerParams(dimension_semantics=sem, vmem_limit_bytes=VMEM_LIMIT)


def _sigmoid(x):
    return 0.5 * jnp.tanh(0.5 * x) + 0.5


def _silu(x):
    return x * _sigmoid(x)


def _softplus(x):
    return jnp.maximum(x, 0.0) + jnp.log1p(jnp.exp(-jnp.abs(x)))


def _log_sigmoid(x):
    return jnp.minimum(x, 0.0) - jnp.log(1.0 + jnp.exp(-jnp.abs(x)))


def _split3(a):
    hi = a.astype(BF16)
    r = a - hi.astype(F32)
    mid = r.astype(BF16)
    lo = (r - mid.astype(F32)).astype(BF16)
    return hi, mid, lo


def _dot(a, b):
    return jnp.dot(a, b, preferred_element_type=F32)


def _dot_nt(a, b):
    return lax.dot_general(a, b, (((1,), (1,)), ((), ())), preferred_element_type=F32)


def _dot_tn(a, b):
    return lax.dot_general(a, b, (((0,), (0,)), ((), ())), preferred_element_type=F32)


def _rms(x):
    return x * lax.rsqrt(jnp.mean(x * x, axis=-1, keepdims=True) + EPS)


def _mod_kernel(c_ref, w_ref, b_ref, o_ref):
    s = _silu(c_ref[...])
    s_hi, s_mid, s_lo = _split3(s)
    w_hi, w_mid, w_lo = _split3(w_ref[...])
    acc = _dot(s_hi, w_hi)
    acc += _dot(s_hi, w_mid) + _dot(s_mid, w_hi)
    acc += _dot(s_hi, w_lo) + _dot(s_mid, w_mid) + _dot(s_lo, w_hi)
    o_ref[...] = acc + b_ref[...]


def _modulation(cvec, w_mod, b_mod):
    depth, d, d3 = w_mod.shape
    out = pl.pallas_call(
        _mod_kernel,
        out_shape=jax.ShapeDtypeStruct((depth, MOD_ROWS, d3), F32),
        grid=(depth, d3 // d),
        in_specs=[
            pl.BlockSpec((MOD_ROWS, d), lambda i, j: (0, 0)),
            pl.BlockSpec((None, d, d), lambda i, j: (i, 0, j)),
            pl.BlockSpec((None, 1, d), lambda i, j: (i, 0, j)),
        ],
        out_specs=pl.BlockSpec((None, MOD_ROWS, d), lambda i, j: (i, 0, j)),
        compiler_params=_cparams("parallel", "parallel"),
        name="modulation",
    )(cvec, w_mod, b_mod.reshape(depth, 1, d3))
    return out.reshape(depth, MOD_ROWS, 3, d)


def _prenorm(x, g, mod):
    return _rms(x) * g * (1.0 + mod[1:2]) + mod[0:1]


def _mod_spec(d, n_ctx_tiles, ctx_row, offset=0):
    def index(b, t):
        return (jnp.where(t + offset < n_ctx_tiles, ctx_row, b), 0, 0)
    return pl.BlockSpec((None, 3, d), index)


def _row_spec(width, offset=0):
    return pl.BlockSpec((None, ROW_TILE, width), lambda b, t: (b, t + offset, 0))


def _token_specs(xs, n_ctx, offset=0):
    if not isinstance(xs, tuple):
        return [_row_spec(xs.shape[-1], offset)]
    d = xs[0].shape[-1]
    return [pl.BlockSpec((None, ROW_TILE, d), lambda b, t: (b, jnp.minimum(t + offset, n_ctx - 1), 0)),
            pl.BlockSpec((None, ROW_TILE, d), lambda b, t: (b, jnp.maximum(t + offset - n_ctx, 0), 0))]


def _token_tile(x_refs, n_ctx_tiles):
    if len(x_refs) == 1:
        return x_refs[0][...]
    return jnp.where(pl.program_id(1) < n_ctx_tiles, x_refs[0][...], x_refs[1][...])


def _full_spec(shape):
    zeros = (0,) * len(shape)
    return pl.BlockSpec(shape, lambda b, t: zeros)


def _gla_pre_part(x, mod_ref, g_ref, w_ref, wg1_ref, wg2_ref, bg_ref, qk_ref, v_ref, z_ref, la_ref, peak_ref,
                  *, dk):
    h = _prenorm(x, g_ref[...], mod_ref[...]).astype(BF16)
    hk = dk // GLA_HEADS
    dv = v_ref.shape[-1]
    g1 = _dot(h, wg1_ref[...]).astype(BF16)
    gate = _dot(g1, wg2_ref[...]) + bg_ref[...]
    la = _log_sigmoid(gate) / GLA_TAU
    la_ref[...] = la.astype(BF16)
    peak = jnp.max(jnp.max(-la, axis=1, keepdims=True), axis=0, keepdims=True)
    peak_ref[...] = jnp.broadcast_to(peak, peak_ref.shape)
    q = _dot(h, w_ref[:, :dk]) * hk ** -0.5
    qk_ref[:, :dk] = q.astype(BF16)
    qk_ref[:, dk:] = _dot(h, w_ref[:, dk:2 * dk]).astype(BF16)
    v_ref[...] = _dot(h, w_ref[:, 2 * dk:2 * dk + dv]).astype(BF16)
    z_ref[...] = _dot(h, w_ref[:, 2 * dk + dv:]).astype(BF16)


def _gla_core_kernel(exact_ref, qkf_ref, vf_ref, laf_ref, qkb_ref, vb_ref, lab_ref,
                     of_ref, ob_ref, st_ref, row_ref, *, dk, n_ctx, n_all):
    b, n = pl.program_id(0), pl.program_id(1)

    @pl.when(n == 0)
    def _():
        st_ref[...] = jnp.zeros_like(st_ref)

    io = ((qkf_ref, vf_ref, laf_ref, of_ref), (qkb_ref, vb_ref, lab_ref, ob_ref))
    n_sub = qkf_ref.shape[0] // GLA_CHUNK

    def run(exact):
        chunks = [_gla_chunk_pair(io, (s * GLA_CHUNK, (n_sub - 1 - s) * GLA_CHUNK), st_ref, dk,
                                  row_ref if exact else None) for s in range(n_sub)]
        for _ in range(3):
            for chunk in chunks:
                next(chunk, None)

    out_of_range = exact_ref[b, n] + exact_ref[b, _bwd_order(n, n_ctx, n_all)]
    pl.when(out_of_range == 0)(lambda: run(False))
    pl.when(out_of_range != 0)(lambda: run(True))


def _gla_exact_intra(d, q, k, v, b, row_ref):
    c, hk = q.shape
    row_ref[:, :hk] = k
    row_ref[:, hk:2 * hk] = b
    row_ref[:, 2 * hk:] = v.astype(F32)
    t_idx = lax.broadcasted_iota(jnp.int32, (c, 1), 0)

    def keys(g, acc):
        group = row_ref[pl.ds(pl.multiple_of(g * 8, 8), 8), :]
        for r in range(8):
            s = g * 8 + r
            k_s, b_s, v_s = group[r:r + 1, :hk], group[r:r + 1, hk:2 * hk], group[r:r + 1, 2 * hk:]
            decay = jnp.exp(jnp.minimum(b - b_s, 0.0))
            col = jnp.sum(q * decay * k_s, axis=1, keepdims=True)
            causal = (t_idx >= s) if d == 0 else (t_idx <= s)
            acc = acc + jnp.where(causal, col, 0.0) * v_s
        return acc

    return lax.fori_loop(0, c // 8, keys, jnp.zeros((c, v.shape[1]), F32))


def _gla_chunk_pair(io, base, st_ref, dk, row_ref):
    c = GLA_CHUNK
    hk = dk // GLA_HEADS
    hv = io[0][1].shape[-1] // GLA_HEADS
    row = lax.broadcasted_iota(jnp.int32, (c, c), 0)
    col = lax.broadcasted_iota(jnp.int32, (c, c), 1)
    blk = GLA_BLOCK
    keeps = (col <= row, col >= row)
    lasts = (c - 1, 0)
    tile = [slice(r0, r0 + c) for r0 in base]
    cums = []
    for d, (_, _, la_ref, _) in enumerate(io):
        tri = jnp.where(keeps[d], 1.0, 0.0).astype(BF16)
        cums.append(_dot(tri, la_ref[tile[d], d * dk:(d + 1) * dk]))
    yield

    stage = {}
    for d, (qk_ref, v_ref, _, _) in enumerate(io):
        for h in range(GLA_HEADS):
            b = cums[d][:, h * hk:(h + 1) * hk]
            b_last = b[lasts[d]:lasts[d] + 1]
            q = qk_ref[tile[d], h * hk:(h + 1) * hk].astype(F32)
            k = qk_ref[tile[d], dk + h * hk:dk + (h + 1) * hk].astype(F32)
            v = v_ref[tile[d], h * hv:(h + 1) * hv]
            scores = []
            for i in (range(c // blk) if row_ref is None else ()):
                rows = slice(i * blk, (i + 1) * blk)
                keys = slice(0, (i + 1) * blk) if d == 0 else slice(i * blk, c)
                b_ref = b[i * blk + blk // 2:i * blk + blk // 2 + 1]
                qe = (q[rows] * jnp.exp(b[rows] - b_ref)).astype(BF16)
                ke = (k[keys] * jnp.exp(b_ref - b[keys])).astype(BF16)
                scores.append((rows, keys, _dot_nt(qe, ke)))
            qs = (q * jnp.exp(b)).astype(BF16)
            kd = (k * jnp.exp(b_last - b)).astype(BF16)
            st = st_ref[d, h]
            inter = _dot_nt(qs, st.astype(BF16))
            st_ref[d, h] = st * jnp.exp(b_last) + _dot_tn(v, kd)
            stage[d, h] = (scores, inter, v, (q, k, b))
    yield

    for d, (_, _, _, o_ref) in enumerate(io):
        for h in range(GLA_HEADS):
            scores, inter, v, (q, k, b) = stage[d, h]
            if row_ref is not None:
                o = _gla_exact_intra(d, q, k, v, b, row_ref) + inter
                o_ref[tile[d], h * hv:(h + 1) * hv] = o.astype(BF16)
            for rows, keys, s in scores:
                t_idx = lax.broadcasted_iota(jnp.int32, s.shape, 0) + rows.start
                s_idx = lax.broadcasted_iota(jnp.int32, s.shape, 1) + keys.start
                p = jnp.where((s_idx <= t_idx) if d == 0 else (s_idx >= t_idx), s, 0.0).astype(BF16)
                o = _dot(p, v[keys]) + inter[rows]
                o_ref[base[d] + rows.start:base[d] + rows.stop, h * hv:(h + 1) * hv] = o.astype(BF16)


def _bwd_order(n, n_ctx, n_all):
    return jnp.where(n < n_ctx, n_ctx - 1 - n, n_all - 1 - (n - n_ctx))


def _gla_core(qk, v, la, exact, n_ctx_rows):
    bsz, tt, dv = v.shape
    dk = qk.shape[-1] // 2
    c = GLA_TILE
    n_all, n_ctx = tt // c, n_ctx_rows // c
    hk, hv = dk // GLA_HEADS, dv // GLA_HEADS
    fwd = lambda b, n: (b, n, 0)
    bwd = lambda b, n: (b, _bwd_order(n, n_ctx, n_all), 0)
    spec = lambda w, idx: pl.BlockSpec((None, c, w), idx)
    return pl.pallas_call(
        functools.partial(_gla_core_kernel, dk=dk, n_ctx=n_ctx, n_all=n_all),
        out_shape=[jax.ShapeDtypeStruct((bsz, tt, dv), BF16)] * 2,
        grid=(bsz, n_all),
        in_specs=[pl.BlockSpec(memory_space=pltpu.SMEM),
                  spec(2 * dk, fwd), spec(dv, fwd), spec(2 * dk, fwd),
                  spec(2 * dk, bwd), spec(dv, bwd), spec(2 * dk, bwd)],
        out_specs=[spec(dv, fwd), spec(dv, bwd)],
        scratch_shapes=[pltpu.VMEM((2, GLA_HEADS, hv, hk), F32),
                        pltpu.VMEM((GLA_CHUNK, 2 * hk + hv), F32)],
        compiler_params=_cparams("arbitrary", "arbitrary"),
        name="gla_core",
    )(exact, qk, v, la, qk, v, la)


def _rope(x, cos, sin_signed):
    lane = lax.broadcasted_iota(jnp.int32, (x.shape[0], 128), 1)
    first = (lane % 32) < 16
    out = []
    for j in range(x.shape[1] // 128):
        xj = x[:, j * 128:(j + 1) * 128]
        partner = jnp.where(first, pltpu.roll(xj, 112, 1), pltpu.roll(xj, 16, 1))
        out.append(xj * cos + partner * sin_signed)
    return jnp.concatenate(out, axis=1)


def _swa_pre_part(x, mod_ref, g_ref, w_ref, cos_ref, sin_ref, q_ref, kv_ref, z_ref):
    h = _prenorm(x, g_ref[...], mod_ref[...]).astype(BF16)
    qw = q_ref.shape[-1]
    kw = kv_ref.shape[-1] // 2
    cos, sin = cos_ref[...], sin_ref[...]
    q = _rope(_dot(h, w_ref[:, :qw]), cos, sin) * (ATT_HEAD_DIM ** -0.5 * LOG2_E)
    q_ref[...] = q.astype(BF16)
    kv_ref[:, :kw] = _rope(_dot(h, w_ref[:, qw:qw + kw]), cos, sin).astype(BF16)
    kv_ref[:, kw:] = _dot(h, w_ref[:, qw + kw:qw + 2 * kw]).astype(BF16)
    z_ref[...] = _dot(h, w_ref[:, qw + 2 * kw:]).astype(BF16)


def _swa_core_kernel(sink_ref, q_ref, kvp_ref, kvc_ref, kvn_ref, kvx_ref, o_ref, *, n_ctx, n_all):
    t = pl.program_id(1)
    blk = ATT_BLOCK
    n_keys = 3 * blk + kvx_ref.shape[0]
    kw = kvc_ref.shape[-1] // 2
    n_loc = 3 * blk
    i = lax.broadcasted_iota(jnp.int32, (blk, n_loc), 0)
    j = lax.broadcasted_iota(jnp.int32, (blk, n_loc), 1)
    never = n_keys
    off_prev = jnp.where(t > n_ctx, 0, never)
    off_cur = jnp.where(t >= n_ctx, 0, never)
    off_next = jnp.where(jnp.logical_and(t >= n_ctx, t < n_all - 1), 0, never)
    valid = (((j >= i + off_prev) & (j < blk))
             | ((j >= blk + off_cur) & (j < 2 * blk))
             | ((j >= 2 * blk) & (j <= i + 2 * blk - off_next)))
    bias = jnp.where(valid, 0.0, -jnp.inf)
    logits = []
    for h in range(ATT_KV_HEADS):
        ks = slice(h * ATT_HEAD_DIM, (h + 1) * ATT_HEAD_DIM)
        k = jnp.concatenate([kvp_ref[:, ks], kvc_ref[:, ks], kvn_ref[:, ks], kvx_ref[:, ks]], axis=0)
        q = jnp.concatenate(
            [q_ref[:, (h * ATT_GROUP + g) * ATT_HEAD_DIM:(h * ATT_GROUP + g + 1) * ATT_HEAD_DIM]
             for g in range(ATT_GROUP)], axis=0)
        logits.append(_dot_nt(q, k))
    chunks = [(h, r0) for h in range(ATT_KV_HEADS) for r0 in range(0, ATT_GROUP * blk, ATT_ROWS)]

    def slabs(h, r0):
        out = []
        for c0 in range(0, n_keys, 128):
            s = logits[h][r0:r0 + ATT_ROWS, c0:c0 + 128]
            if c0 < n_loc:
                s = s + bias[r0 % blk:r0 % blk + ATT_ROWS, c0:c0 + 128]
            out.append(s)
        return out

    sinks = {(h, r0): sink_ref[h * ATT_GROUP + r0 // blk] * LOG2_E for h, r0 in chunks}
    maxes = {}
    for h, r0 in chunks:
        lane_max = functools.reduce(jnp.maximum, slabs(h, r0))
        maxes[h, r0] = jnp.maximum(jnp.max(lane_max, axis=-1, keepdims=True), sinks[h, r0])
    head_lane = lax.broadcasted_iota(jnp.int32, (n_keys, 2 * ATT_HEAD_DIM), 1) // ATT_HEAD_DIM
    for h in range(ATT_KV_HEADS):
        vs = slice(kw + (h // 2) * 2 * ATT_HEAD_DIM, kw + (h // 2 + 1) * 2 * ATT_HEAD_DIM)
        v = jnp.concatenate([kvp_ref[:, vs], kvc_ref[:, vs], kvn_ref[:, vs], kvx_ref[:, vs]], axis=0)
        v = jnp.where(head_lane == h % 2, v, jnp.ones_like(v))
        probs, sink_terms = [], []
        for _, r0 in chunks[:len(chunks) // ATT_KV_HEADS]:
            p = [jnp.exp2(s - maxes[h, r0]) for s in slabs(h, r0)]
            probs.append(jnp.concatenate(p, axis=1).astype(BF16))
            sink_terms.append(jnp.exp2(sinks[h, r0] - maxes[h, r0]))
        pv = _dot(jnp.concatenate(probs, axis=0), v)
        own, other = (h % 2) * ATT_HEAD_DIM, (1 - h % 2) * ATT_HEAD_DIM
        denom = pv[:, other:other + 1] + jnp.concatenate(sink_terms, axis=0)
        o = pv[:, own:own + ATT_HEAD_DIM] * (1.0 / denom)
        for g in range(ATT_GROUP):
            lo = (h * ATT_GROUP + g) * ATT_HEAD_DIM
            o_ref[:, lo:lo + ATT_HEAD_DIM] = o[g * blk:(g + 1) * blk].astype(BF16)


def _swa_core(q, kv, sink, n_ctx_rows):
    bsz, tt, qw = q.shape
    blk = ATT_BLOCK
    n_all, n_ctx = tt // blk, n_ctx_rows // blk
    kvw = kv.shape[-1]
    kv_spec = lambda idx: pl.BlockSpec((None, blk, kvw), idx)
    return pl.pallas_call(
        functools.partial(_swa_core_kernel, n_ctx=n_ctx, n_all=n_all),
        out_shape=jax.ShapeDtypeStruct((bsz, tt, qw), BF16),
        grid=(bsz, n_all),
        in_specs=[
            pl.BlockSpec(memory_space=pltpu.SMEM),
            pl.BlockSpec((None, blk, qw), lambda b, t: (b, t, 0)),
            kv_spec(lambda b, t: (b, jnp.maximum(t - 1, 0), 0)),
            kv_spec(lambda b, t: (b, t, 0)),
            kv_spec(lambda b, t: (b, jnp.minimum(t + 1, n_all - 1), 0)),
            pl.BlockSpec((None, n_ctx_rows, kvw), lambda b, t: (b, 0, 0)),
        ],
        out_specs=pl.BlockSpec((None, blk, qw), lambda b, t: (b, t, 0)),
        compiler_params=_cparams("parallel", "parallel"),
        name="swa_core",
    )(sink, q, kv, kv, kv, kv)


def _segment_permutation(n):
    r = jnp.arange(n)
    p = (r[None, :] == ((r % 8) * (n // 8) + r // 8)[:, None]).astype(BF16)
    return p, p.T


def _rnn_pre_part(x, mod_ref, g_ref, w_ref, u_ref, z_ref):
    h = _prenorm(x, g_ref[...], mod_ref[...]).astype(BF16)
    rw = u_ref.shape[-1]
    u_ref[...] = _dot(h, w_ref[:, :rw]).astype(BF16)
    z_ref[...] = _dot(h, w_ref[:, rw:]).astype(BF16)


def _shift_rows(cur, other, up):
    sub = lax.broadcasted_iota(jnp.int32, cur.shape, 0)
    if up:
        return jnp.where(sub == 7, pltpu.roll(other, 7, 0), pltpu.roll(cur, 7, 0))
    return jnp.where(sub == 0, pltpu.roll(other, 1, 0), pltpu.roll(cur, 1, 0))


def _segment_scan(a, x, carry, reverse):
    n_g = a.shape[0] // 8
    order = range(n_g - 1, -1, -1) if reverse else range(n_g)
    hs, ps = [None] * n_g, [None] * n_g
    h = p = None
    for g in order:
        ag, xg = a[8 * g:8 * g + 8], x[8 * g:8 * g + 8]
        h = xg if h is None else ag * h + xg
        p = ag if p is None else ag * p
        hs[g], ps[g] = h, p
    sub = lax.broadcasted_iota(jnp.int32, h.shape, 0)
    for s in (1, 2, 4):
        shift = 8 - s if reverse else s
        inside = (sub < 8 - s) if reverse else (sub >= s)
        h = h + p * jnp.where(inside, pltpu.roll(h, shift, 0), 0.0)
        p = p * jnp.where(inside, pltpu.roll(p, shift, 0), 1.0)
    full = h + p * carry
    first, last = (7, 0) if reverse else (0, 7)
    seg_in = jnp.where(sub == first, carry, pltpu.roll(full, 7 if reverse else 1, 0))
    out = jnp.concatenate([hs[g] + ps[g] * seg_in for g in range(n_g)], axis=0)
    return out, jnp.broadcast_to(full[last:last + 1], full.shape)


def _rnn_core_kernel(upf_ref, ucf_ref, unf_ref, upb_ref, ucb_ref, unb_ref,
                     cw_ref, cb_ref, wg_ref, ba_ref, bx_ref, lam_ref,
                     perm_ref, unperm_ref, hf_ref, hb_ref, carry_ref, *, n_ctx, n_all):
    n = pl.program_id(1)

    @pl.when(n == 0)
    def _():
        carry_ref[...] = jnp.zeros_like(carry_ref)

    tiles = (n, _bwd_order(n, n_ctx, n_all))
    prev_ok = [jnp.where(jnp.logical_and(t != 0, t != n_ctx), 1.0, 0.0) for t in tiles]
    next_ok = [jnp.where(jnp.logical_and(t != n_ctx - 1, t != n_all - 1), 1.0, 0.0) for t in tiles]
    refs = ((upf_ref, ucf_ref, unf_ref, hf_ref), (upb_ref, ucb_ref, unb_ref, hb_ref))

    useg = [_dot(perm_ref[...], refs[d][1][...]) for d in range(2)]
    rows = useg[0].shape[0]
    convs, pres = {}, {}
    for d, (p_ref, _, n_ref, _) in enumerate(refs):
        before = p_ref[...].astype(F32)[HALO - 8:] * prev_ok[d]
        after = n_ref[...].astype(F32)[:8] * next_ok[d]
        before2 = pltpu.roll(before, 1, 0)
        for j in range(RNN_HEADS):
            lanes = slice(j * RNN_HD, (j + 1) * RNN_HD)
            u = useg[d][:, lanes]
            ext = jnp.concatenate([
                _shift_rows(u[rows - 16:rows - 8], before2[:, lanes], up=False),
                _shift_rows(u[rows - 8:], before[:, lanes], up=False),
                u,
                _shift_rows(u[:8], after[:, lanes], up=True)], axis=0)
            conv = cb_ref[:, lanes]
            for tap in range(CONV_W):
                conv = conv + ext[8 * tap:8 * tap + rows] * cw_ref[tap:tap + 1, lanes]
            convs[d, j] = conv
            pres[d, j] = _dot(conv.astype(BF16), wg_ref[d, j])

    half = RNN_HEADS // 2
    for j0 in (0, half):
        for d in range(2):
            outs = []
            for j in range(j0, j0 + half):
                pre, conv = pres[d, j], convs[d, j]
                u = jnp.tanh(pre[:, :RNN_HD] + ba_ref[d, j]) + 1.0
                i = jnp.tanh(pre[:, RNN_HD:] + bx_ref[d, j]) + 1.0
                rate = (0.5 * LRU_C) * _softplus(-lam_ref[d, j])
                a = jnp.exp2((-LOG2_E * rate) * u)
                y = 1.0 - a * a
                x = jnp.where(y > 0.0, y * lax.rsqrt(y), 0.0) * i * conv
                hs, carry = _segment_scan(a, x, carry_ref[d, j], reverse=(d == 1))
                carry_ref[d, j] = carry
                outs.append(hs.astype(BF16))
            h = _dot(unperm_ref[...], jnp.concatenate(outs, axis=1))
            refs[d][3][:, j0 * RNN_HD:(j0 + half) * RNN_HD] = h.astype(BF16)


def _rnn_core(u, conv_w, conv_b, wg, b_a, b_x, lam, n_ctx_rows):
    bsz, tt, rw = u.shape
    tile = RNN_TILE
    n_all, n_ctx = tt // tile, n_ctx_rows // tile
    per = tile // HALO
    n_halo = tt // HALO
    fwd = lambda n: n
    bwd = lambda n: _bwd_order(n, n_ctx, n_all)

    def specs(order):
        return [
            pl.BlockSpec((None, HALO, rw), lambda b, n: (b, jnp.maximum(order(n) * per - 1, 0), 0)),
            pl.BlockSpec((None, tile, rw), lambda b, n: (b, order(n), 0)),
            pl.BlockSpec((None, HALO, rw), lambda b, n: (b, jnp.minimum((order(n) + 1) * per, n_halo - 1), 0)),
        ]

    out_spec = lambda order: pl.BlockSpec((None, tile, rw), lambda b, n: (b, order(n), 0))
    perm, unperm = _segment_permutation(tile)
    consts = (conv_w, conv_b, wg, b_a, b_x, lam, perm, unperm)
    return pl.pallas_call(
        functools.partial(_rnn_core_kernel, n_ctx=n_ctx, n_all=n_all),
        out_shape=[jax.ShapeDtypeStruct((bsz, tt, rw), BF16)] * 2,
        grid=(bsz, n_all),
        in_specs=specs(fwd) + specs(bwd) + [_full_spec(c.shape) for c in consts],
        out_specs=[out_spec(fwd), out_spec(bwd)],
        scratch_shapes=[pltpu.VMEM((2, RNN_HEADS, 8, RNN_HD), F32)],
        compiler_params=_cparams("arbitrary", "arbitrary"),
        name="rglru_core",
    )(u, u, u, u, u, u, *consts)


def _residual(u, x, mod_ref, gp_ref, w_ref):
    y = _dot(u.astype(BF16), w_ref[...])
    return x + mod_ref[2:3] * (_rms(y) * gp_ref[...])


def _gla_post_part(x, mod_ref, gp_ref, w_ref, of_ref, ob_ref, z_ref, gh_ref, rows=slice(None)):
    o = of_ref[rows].astype(F32) + ob_ref[rows].astype(F32)
    hv = o.shape[-1] // GLA_HEADS
    o = jnp.concatenate([_rms(o[:, h * hv:(h + 1) * hv]) for h in range(GLA_HEADS)], axis=1)
    u = o * gh_ref[...] * _silu(z_ref[rows].astype(F32))
    return _residual(u, x, mod_ref, gp_ref, w_ref)


def _swa_post_part(x, mod_ref, gp_ref, w_ref, a_ref, z_ref, rows=slice(None)):
    u = a_ref[rows].astype(F32) * _silu(z_ref[rows].astype(F32))
    return _residual(u, x, mod_ref, gp_ref, w_ref)


def _rnn_post_part(x, mod_ref, gp_ref, w_ref, hf_ref, hb_ref, z_ref, rows=slice(None)):
    u = (hf_ref[rows].astype(F32) + hb_ref[rows].astype(F32)) * _silu(z_ref[rows].astype(F32))
    return _residual(u, x, mod_ref, gp_ref, w_ref)


class _Pre(NamedTuple):
    name: str
    part: Callable
    mod: jax.Array
    g_pre: jax.Array
    consts: tuple
    row_consts: tuple
    widths: tuple
    dtypes: tuple
    stats: int


class _Post(NamedTuple):
    name: str
    part: Callable
    mod: jax.Array
    g_post: jax.Array
    w_out: jax.Array
    acts: tuple
    vecs: tuple


def _pre_kernel(*refs, part, n_x, n_ctx):
    part(_token_tile(refs[:n_x], n_ctx), *refs[n_x:])


def _post_kernel(*refs, part, n_x, n_ctx):
    refs[-1][...] = part(_token_tile(refs[:n_x], n_ctx), *refs[n_x:-1])


def _mid_kernel(*refs, post_part, n_post, pre_part, n_pre, n_x, n_ctx):
    post_in, pre_in = refs[n_x:n_x + n_post], refs[n_x + n_post:n_x + n_post + n_pre]
    xo_ref, pre_out = refs[n_x + n_post + n_pre], refs[n_x + n_post + n_pre + 1:]
    x_in = _token_tile(refs[:n_x], n_ctx)
    half = x_in.shape[0] // 2
    x = jnp.concatenate([post_part(x_in[rows], *post_in, rows=rows)
                         for rows in (slice(0, half), slice(half, 2 * half))], axis=0)
    xo_ref[...] = x
    pre_part(x, *pre_in, *pre_out)


def _stream_shape(xs):
    if isinstance(xs, tuple):
        bsz, n_ctx_rows, d = xs[0].shape
        return bsz, n_ctx_rows + xs[1].shape[1], d
    return xs.shape


def _pre_io(pre, bsz, tt, d, n_ctx):
    operands = (pre.mod, pre.g_pre, *pre.consts, *pre.row_consts)
    in_specs = ([_mod_spec(d, n_ctx, bsz), _full_spec(pre.g_pre.shape)]
                + [_full_spec(c.shape) for c in pre.consts]
                + [pl.BlockSpec((ROW_TILE, r.shape[-1]), lambda b, t: (t, 0)) for r in pre.row_consts])
    out_shape = ([jax.ShapeDtypeStruct((bsz, tt, w), dt) for w, dt in zip(pre.widths, pre.dtypes)]
                 + [jax.ShapeDtypeStruct((bsz, tt // ROW_TILE, 8, 128), F32)] * pre.stats)
    out_specs = ([_row_spec(w) for w in pre.widths]
                 + [pl.BlockSpec((None, None, 8, 128), lambda b, t: (b, t, 0, 0))] * pre.stats)
    return operands, in_specs, out_shape, out_specs


def _post_io(post, bsz, d, n_ctx, off):
    operands = (post.mod, post.g_post, post.w_out, *post.acts, *post.vecs)
    in_specs = ([_mod_spec(d, n_ctx, bsz, off), _full_spec(post.g_post.shape), _full_spec(post.w_out.shape)]
                + [_row_spec(a.shape[-1], off) for a in post.acts]
                + [_full_spec(v.shape) for v in post.vecs])
    return operands, in_specs


def _pre_call(pre, xs, n_ctx_rows):
    bsz, tt, d = _stream_shape(xs)
    n_ctx = n_ctx_rows // ROW_TILE
    streams = xs if isinstance(xs, tuple) else (xs,)
    operands, in_specs, out_shape, out_specs = _pre_io(pre, bsz, tt, d, n_ctx)
    return pl.pallas_call(
        functools.partial(_pre_kernel, part=pre.part, n_x=len(streams), n_ctx=n_ctx),
        out_shape=out_shape,
        grid=(bsz, tt // ROW_TILE),
        in_specs=_token_specs(xs, n_ctx) + in_specs,
        out_specs=out_specs,
        compiler_params=_cparams("parallel", "parallel"),
        name=pre.name,
    )(*streams, *operands)


def _post_call(post, xs, n_ctx_rows, latent_only):
    bsz, tt, d = _stream_shape(xs)
    n_ctx = n_ctx_rows // ROW_TILE
    off = n_ctx if latent_only else 0
    n_tiles = tt // ROW_TILE - off
    streams = xs if isinstance(xs, tuple) else (xs,)
    operands, in_specs = _post_io(post, bsz, d, n_ctx, off)
    return pl.pallas_call(
        functools.partial(_post_kernel, part=post.part, n_x=len(streams), n_ctx=n_ctx - off),
        out_shape=jax.ShapeDtypeStruct((bsz, n_tiles * ROW_TILE, d), F32),
        grid=(bsz, n_tiles),
        in_specs=_token_specs(xs, n_ctx, off) + in_specs,
        out_specs=_row_spec(d),
        compiler_params=_cparams("parallel", "parallel"),
        name=post.name,
    )(*streams, *operands)


def _mid_call(post, pre, xs, n_ctx_rows):
    bsz, tt, d = _stream_shape(xs)
    n_ctx = n_ctx_rows // ROW_TILE
    streams = xs if isinstance(xs, tuple) else (xs,)
    post_ops, post_specs = _post_io(post, bsz, d, n_ctx, 0)
    pre_ops, pre_specs, pre_shape, pre_out_specs = _pre_io(pre, bsz, tt, d, n_ctx)
    outs = pl.pallas_call(
        functools.partial(_mid_kernel, post_part=post.part, n_post=len(post_ops),
                          pre_part=pre.part, n_pre=len(pre_ops), n_x=len(streams), n_ctx=n_ctx),
        out_shape=[jax.ShapeDtypeStruct((bsz, tt, d), F32)] + pre_shape,
        grid=(bsz, tt // ROW_TILE),
        in_specs=_token_specs(xs, n_ctx) + post_specs + pre_specs,
        out_specs=[_row_spec(d)] + pre_out_specs,
        compiler_params=_cparams("parallel", "parallel"),
        name=post.name + "_" + pre.name,
    )(*streams, *post_ops, *pre_ops)
    return outs[0], outs[1:]


def _gla_layer(mod, g_pre, g_post, w_in, w_g1, w_g2, b_g, g_head, w_out, n_ctx_rows):
    d = w_in.shape[0]
    rank, dk = w_g2.shape[1], w_g2.shape[2]
    dv = w_out.shape[0]
    wg1 = jnp.zeros((d, GLA_GATE_PAD), F32).at[:, :2 * rank].set(jnp.concatenate([w_g1[0], w_g1[1]], axis=1))
    wg2 = jnp.zeros((GLA_GATE_PAD, 2 * dk), F32)
    wg2 = wg2.at[:rank, :dk].set(w_g2[0]).at[rank:2 * rank, dk:].set(w_g2[1])
    pre = _Pre("gla_pre", functools.partial(_gla_pre_part, dk=dk), mod, g_pre,
               (w_in.astype(BF16), wg1.astype(BF16), wg2.astype(BF16), b_g.reshape(1, 2 * dk)), (),
               (2 * dk, dv, dv, 2 * dk), (BF16, BF16, BF16, BF16), 1)

    def mixer(outs):
        qk, v, z, la, peak = outs
        exact = (peak[:, :, 0, 0] * (GLA_BLOCK // 2) > GLA_SAFE_EXPONENT).astype(jnp.int32)
        o_f, o_b = _gla_core(qk, v, la, exact, n_ctx_rows)
        return _Post("gla_post", _gla_post_part, mod, g_post, w_out.astype(BF16), (o_f, o_b, z),
                     (g_head.reshape(1, dv),))

    return pre, mixer


def _swa_layer(mod, g_pre, g_post, w_in, sink, w_out, rope, n_ctx_rows):
    qw = w_out.shape[0]
    kw = (w_in.shape[1] - 2 * qw) // 2
    pre = _Pre("swa_pre", _swa_pre_part, mod, g_pre, (w_in.astype(BF16),), rope,
               (qw, 2 * kw, qw), (BF16, BF16, BF16), 0)

    def mixer(outs):
        q, kv, z = outs
        a = _swa_core(q, kv, sink, n_ctx_rows)
        return _Post("swa_post", _swa_post_part, mod, g_post, w_out.astype(BF16), (a, z), ())

    return pre, mixer


def _rnn_layer(mod, g_pre, g_post, w_in, conv_w, conv_b, w_ra, b_ra, w_ri, b_ri, lam, w_out, n_ctx_rows):
    rw = w_out.shape[0]
    pre = _Pre("rglru_pre", _rnn_pre_part, mod, g_pre, (w_in.astype(BF16),), (), (rw, rw), (BF16, BF16), 0)
    wg = jnp.concatenate([w_ra, w_ri], axis=-1).astype(BF16)
    vec = lambda a: a.reshape(2, RNN_HEADS, 1, RNN_HD)

    def mixer(outs):
        u, z = outs
        h_f, h_b = _rnn_core(u, 0.5 * conv_w, 0.5 * conv_b.reshape(1, rw), wg, vec(0.5 * b_ra), vec(0.5 * b_ri),
                             vec(lam), n_ctx_rows)
        return _Post("rglru_post", _rnn_post_part, mod, g_post, w_out.astype(BF16), (h_f, h_b, z), ())

    return pre, mixer


def _rope_tables(t, n_ctx_rows):
    pairs = ATT_HEAD_DIM // 4
    pos = jnp.arange(t, dtype=jnp.int32)
    row = (pos // GRID_W).astype(F32)
    col = (pos % GRID_W).astype(F32)
    freqs = ROPE_BASE ** (-jnp.arange(pairs, dtype=F32) / pairs)
    ang_r, ang_c = row[:, None] * freqs, col[:, None] * freqs
    cos = jnp.concatenate([jnp.cos(ang_r)] * 2 + [jnp.cos(ang_c)] * 2, axis=1)
    sin = jnp.concatenate([-jnp.sin(ang_r), jnp.sin(ang_r), -jnp.sin(ang_c), jnp.sin(ang_c)], axis=1)
    cos = jnp.concatenate([jnp.ones((n_ctx_rows, ATT_HEAD_DIM), F32), cos], axis=0)
    sin = jnp.concatenate([jnp.zeros((n_ctx_rows, ATT_HEAD_DIM), F32), sin], axis=0)
    return jnp.tile(cos, (1, 2)), jnp.tile(sin, (1, 2))


def kernel(x, c, ctx, c_ctx, w_mod, b_mod, g_pre, g_post, a_w_in, a_w_g1, a_w_g2, a_b_g, a_g_head, a_w_out, b_w_in, b_sink, b_w_out, c_w_in, c_conv_w, c_conv_b, c_w_ra, c_b_ra, c_w_ri, c_b_ri, c_lam, c_w_out):
    bsz, t, d = x.shape
    n_ctx_rows = ctx.shape[1]
    depth = w_mod.shape[0]
    assert bsz < MOD_ROWS and n_ctx_rows % ROW_TILE == 0 and t % ROW_TILE == 0

    cvec = jnp.zeros((MOD_ROWS, d), F32).at[:bsz].set(c).at[bsz].set(c_ctx)
    mods = _modulation(cvec, w_mod, b_mod)
    rope = _rope_tables(t, n_ctx_rows)
    layers = []
    for i in range(depth):
        kind, j = i % N_MIXERS, i // N_MIXERS
        gpre, gpost = g_pre[i].reshape(1, d), g_post[i].reshape(1, d)
        if kind == 0:
            layers.append(_gla_layer(mods[i], gpre, gpost, a_w_in[j], a_w_g1[j], a_w_g2[j], a_b_g[j],
                                     a_g_head[j], a_w_out[j], n_ctx_rows))
        elif kind == 1:
            layers.append(_swa_layer(mods[i], gpre, gpost, b_w_in[j], b_sink[j], b_w_out[j], rope, n_ctx_rows))
        else:
            layers.append(_rnn_layer(mods[i], gpre, gpost, c_w_in[j], c_conv_w[j], c_conv_b[j], c_w_ra[j],
                                     c_b_ra[j], c_w_ri[j], c_b_ri[j], c_lam[j], c_w_out[j], n_ctx_rows))

    xs = (ctx, x)
    outs = _pre_call(layers[0][0], xs, n_ctx_rows)
    for i in range(depth - 1):
        xs, outs = _mid_call(layers[i][1](outs), layers[i + 1][0], xs, n_ctx_rows)
    return _post_call(layers[-1][1](outs), xs, n_ctx_rows, latent_only=True)
```

```python
import functools
from typing import Callable, NamedTuple

import jax
import jax.numpy as jnp
from jax import lax
from jax.experimental import pallas as pl
from jax.experimental.pallas import tpu as pltpu

F32 = jnp.float32
BF16 = jnp.bfloat16

EPS = 1e-6
GRID_W = 64
N_MIXERS = 3

GLA_HEADS = 4
GLA_TAU = 16.0
GLA_GATE_PAD = 128

ATT_HEAD_DIM = 64
ATT_KV_HEADS = 4
ATT_GROUP = 4
ATT_BLOCK = 128
ATT_ROWS = 32
ATT_STEP_BLOCKS = 2
ROPE_BASE = 10000.0
LOG2_E = 1.4426950408889634

RNN_HEADS = 10
RNN_HD = 128
CONV_W = 4
LRU_C = 8.0

ROW_TILE = 256
GLA_TILE = 256
GLA_CHUNK = 128
GLA_BLOCK = 64
GLA_SAFE_EXPONENT = 60.0
RNN_TILE = 256
HALO = 16
MOD_ROWS = 8
VMEM_LIMIT = 48 * 1024 * 1024


def _cparams(*sem):
    return pltpu.CompilerParams(dimension_semantics=sem, vmem_limit_bytes=VMEM_LIMIT)


def _sigmoid(x):
    return 0.5 * jnp.tanh(0.5 * x) + 0.5


def _silu(x):
    return x * _sigmoid(x)


def _softplus(x):
    return jnp.maximum(x, 0.0) + jnp.log1p(jnp.exp(-jnp.abs(x)))


def _log_sigmoid(x):
    return jnp.minimum(x, 0.0) - jnp.log(1.0 + jnp.exp(-jnp.abs(x)))


def _split3(a):
    hi = a.astype(BF16)
    r = a - hi.astype(F32)
    mid = r.astype(BF16)
    lo = (r - mid.astype(F32)).astype(BF16)
    return hi, mid, lo


def _dot(a, b):
    return jnp.dot(a, b, preferred_element_type=F32)


def _dot_nt(a, b):
    return lax.dot_general(a, b, (((1,), (1,)), ((), ())), preferred_element_type=F32)


def _dot_tn(a, b):
    return lax.dot_general(a, b, (((0,), (0,)), ((), ())), preferred_element_type=F32)


def _rms(x):
    return x * lax.rsqrt(jnp.mean(x * x, axis=-1, keepdims=True) + EPS)


def _mod_kernel(c_ref, w_ref, b_ref, o_ref):
    s = _silu(c_ref[...])
    s_hi, s_mid, s_lo = _split3(s)
    w_hi, w_mid, w_lo = _split3(w_ref[...])
    acc = _dot(s_hi, w_hi)
    acc += _dot(s_hi, w_mid) + _dot(s_mid, w_hi)
    acc += _dot(s_hi, w_lo) + _dot(s_mid, w_mid) + _dot(s_lo, w_hi)
    o_ref[...] = acc + b_ref[...]


def _modulation(cvec, w_mod, b_mod):
    depth, d, d3 = w_mod.shape
    out = pl.pallas_call(
        _mod_kernel,
        out_shape=jax.ShapeDtypeStruct((depth, MOD_ROWS, d3), F32),
        grid=(depth, d3 // d),
        in_specs=[
            pl.BlockSpec((MOD_ROWS, d), lambda i, j: (0, 0)),
            pl.BlockSpec((None, d, d), lambda i, j: (i, 0, j)),
            pl.BlockSpec((None, 1, d), lambda i, j: (i, 0, j)),
        ],
        out_specs=pl.BlockSpec((None, MOD_ROWS, d), lambda i, j: (i, 0, j)),
        compiler_params=_cparams("parallel", "parallel"),
        name="modulation",
    )(cvec, w_mod, b_mod.reshape(depth, 1, d3))
    return out.reshape(depth, MOD_ROWS, 3, d)


def _prenorm(x, g, mod):
    return _rms(x) * g * (1.0 + mod[1:2]) + mod[0:1]


def _mod_spec(d, n_ctx_tiles, ctx_row, offset=0):
    def index(b, t):
        return (jnp.where(t + offset < n_ctx_tiles, ctx_row, b), 0, 0)
    return pl.BlockSpec((None, 3, d), index)


def _row_spec(width, offset=0):
    return pl.BlockSpec((None, ROW_TILE, width), lambda b, t: (b, t + offset, 0))


def _token_specs(xs, n_ctx, offset=0):
    if not isinstance(xs, tuple):
        return [_row_spec(xs.shape[-1], offset)]
    d = xs[0].shape[-1]
    return [pl.BlockSpec((None, ROW_TILE, d), lambda b, t: (b, jnp.minimum(t + offset, n_ctx - 1), 0)),
            pl.BlockSpec((None, ROW_TILE, d), lambda b, t: (b, jnp.maximum(t + offset - n_ctx, 0), 0))]


def _token_tile(x_refs, n_ctx_tiles):
    if len(x_refs) == 1:
        return x_refs[0][...]
    return jnp.where(pl.program_id(1) < n_ctx_tiles, x_refs[0][...], x_refs[1][...])


def _full_spec(shape):
    zeros = (0,) * len(shape)
    return pl.BlockSpec(shape, lambda b, t: zeros)


def _gla_pre_part(x, mod_ref, g_ref, w_ref, wg1_ref, wg2_ref, bg_ref, qk_ref, v_ref, z_ref, la_ref, peak_ref,
                  *, dk):
    h = _prenorm(x, g_ref[...], mod_ref[...]).astype(BF16)
    hk = dk // GLA_HEADS
    dv = v_ref.shape[-1]
    g1 = _dot(h, wg1_ref[...]).astype(BF16)
    gate = _dot(g1, wg2_ref[...]) + bg_ref[...]
    la = _log_sigmoid(gate) / GLA_TAU
    la_ref[...] = la.astype(BF16)
    peak = jnp.max(jnp.max(-la, axis=1, keepdims=True), axis=0, keepdims=True)
    peak_ref[...] = jnp.broadcast_to(peak, peak_ref.shape)
    q = _dot(h, w_ref[:, :dk]) * hk ** -0.5
    qk_ref[:, :dk] = q.astype(BF16)
    qk_ref[:, dk:] = _dot(h, w_ref[:, dk:2 * dk]).astype(BF16)
    v_ref[...] = _dot(h, w_ref[:, 2 * dk:2 * dk + dv]).astype(BF16)
    z_ref[...] = _dot(h, w_ref[:, 2 * dk + dv:]).astype(BF16)


def _gla_core_kernel(exact_ref, qkf_ref, vf_ref, laf_ref, qkb_ref, vb_ref, lab_ref,
                     of_ref, ob_ref, st_ref, row_ref, *, dk, n_ctx, n_all):
    b, n = pl.program_id(0), pl.program_id(1)

    @pl.when(n == 0)
    def _():
        st_ref[...] = jnp.zeros_like(st_ref)

    io = ((qkf_ref, vf_ref, laf_ref, of_ref), (qkb_ref, vb_ref, lab_ref, ob_ref))
    n_sub = qkf_ref.shape[0] // GLA_CHUNK

    def run(exact):
        chunks = [_gla_chunk_pair(io, (s * GLA_CHUNK, (n_sub - 1 - s) * GLA_CHUNK), st_ref, dk,
                                  row_ref if exact else None) for s in range(n_sub)]
        for _ in range(3):
            for chunk in chunks:
                next(chunk, None)

    out_of_range = exact_ref[b, n] + exact_ref[b, _bwd_order(n, n_ctx, n_all)]
    pl.when(out_of_range == 0)(lambda: run(False))
    pl.when(out_of_range != 0)(lambda: run(True))


def _gla_exact_intra(d, q, k, v, b, row_ref):
    c, hk = q.shape
    row_ref[:, :hk] = k
    row_ref[:, hk:2 * hk] = b
    row_ref[:, 2 * hk:] = v.astype(F32)
    t_idx = lax.broadcasted_iota(jnp.int32, (c, 1), 0)

    def keys(g, acc):
        group = row_ref[pl.ds(pl.multiple_of(g * 8, 8), 8), :]
        for r in range(8):
            s = g * 8 + r
            k_s, b_s, v_s = group[r:r + 1, :hk], group[r:r + 1, hk:2 * hk], group[r:r + 1, 2 * hk:]
            decay = jnp.exp(jnp.minimum(b - b_s, 0.0))
            col = jnp.sum(q * decay * k_s, axis=1, keepdims=True)
            causal = (t_idx >= s) if d == 0 else (t_idx <= s)
            acc = acc + jnp.where(causal, col, 0.0) * v_s
        return acc

    return lax.fori_loop(0, c // 8, keys, jnp.zeros((c, v.shape[1]), F32))


def _gla_chunk_pair(io, base, st_ref, dk, row_ref):
    c = GLA_CHUNK
    hk = dk // GLA_HEADS
    hv = io[0][1].shape[-1] // GLA_HEADS
    row = lax.broadcasted_iota(jnp.int32, (c, c), 0)
    col = lax.broadcasted_iota(jnp.int32, (c, c), 1)
    blk = GLA_BLOCK
    keeps = (col <= row, col >= row)
    lasts = (c - 1, 0)
    tile = [slice(r0, r0 + c) for r0 in base]
    cums = []
    for d, (_, _, la_ref, _) in enumerate(io):
        tri = jnp.where(keeps[d], 1.0, 0.0).astype(BF16)
        cums.append(_dot(tri, la_ref[tile[d], d * dk:(d + 1) * dk]))
    yield

    stage = {}
    for d, (qk_ref, v_ref, _, _) in enumerate(io):
        for h in range(GLA_HEADS):
            b = cums[d][:, h * hk:(h + 1) * hk]
            b_last = b[lasts[d]:lasts[d] + 1]
            q = qk_ref[tile[d], h * hk:(h + 1) * hk].astype(F32)
            k = qk_ref[tile[d], dk + h * hk:dk + (h + 1) * hk].astype(F32)
            v = v_ref[tile[d], h * hv:(h + 1) * hv]
            scores = []
            for i in (range(c // blk) if row_ref is None else ()):
                rows = slice(i * blk, (i + 1) * blk)
                keys = slice(0, (i + 1) * blk) if d == 0 else slice(i * blk, c)
                b_ref = b[i * blk + blk // 2:i * blk + blk // 2 + 1]
                qe = (q[rows] * jnp.exp(b[rows] - b_ref)).astype(BF16)
                ke = (k[keys] * jnp.exp(b_ref - b[keys])).astype(BF16)
                scores.append((rows, keys, _dot_nt(qe, ke)))
            qs = (q * jnp.exp(b)).astype(BF16)
            kd = (k * jnp.exp(b_last - b)).astype(BF16)
            st = st_ref[d, h]
            inter = _dot_nt(qs, st.astype(BF16))
            st_ref[d, h] = st * jnp.exp(b_last) + _dot_tn(v, kd)
            stage[d, h] = (scores, inter, v, (q, k, b))
    yield

    for d, (_, _, _, o_ref) in enumerate(io):
        for h in range(GLA_HEADS):
            scores, inter, v, (q, k, b) = stage[d, h]
            if row_ref is not None:
                o = _gla_exact_intra(d, q, k, v, b, row_ref) + inter
                o_ref[tile[d], h * hv:(h + 1) * hv] = o.astype(BF16)
            for rows, keys, s in scores:
                t_idx = lax.broadcasted_iota(jnp.int32, s.shape, 0) + rows.start
                s_idx = lax.broadcasted_iota(jnp.int32, s.shape, 1) + keys.start
                p = jnp.where((s_idx <= t_idx) if d == 0 else (s_idx >= t_idx), s, 0.0).astype(BF16)
                o = _dot(p, v[keys]) + inter[rows]
                o_ref[base[d] + rows.start:base[d] + rows.stop, h * hv:(h + 1) * hv] = o.astype(BF16)


def _bwd_order(n, n_ctx, n_all):
    return jnp.where(n < n_ctx, n_ctx - 1 - n, n_all - 1 - (n - n_ctx))


def _gla_core(qk, v, la, exact, n_ctx_rows):
    bsz, tt, dv = v.shape
    dk = qk.shape[-1] // 2
    c = GLA_TILE
    n_all, n_ctx = tt // c, n_ctx_rows // c
    hk, hv = dk // GLA_HEADS, dv // GLA_HEADS
    fwd = lambda b, n: (b, n, 0)
    bwd = lambda b, n: (b, _bwd_order(n, n_ctx, n_all), 0)
    spec = lambda w, idx: pl.BlockSpec((None, c, w), idx)
    return pl.pallas_call(
        functools.partial(_gla_core_kernel, dk=dk, n_ctx=n_ctx, n_all=n_all),
        out_shape=[jax.ShapeDtypeStruct((bsz, tt, dv), BF16)] * 2,
        grid=(bsz, n_all),
        in_specs=[pl.BlockSpec(memory_space=pltpu.SMEM),
                  spec(2 * dk, fwd), spec(dv, fwd), spec(2 * dk, fwd),
                  spec(2 * dk, bwd), spec(dv, bwd), spec(2 * dk, bwd)],
        out_specs=[spec(dv, fwd), spec(dv, bwd)],
        scratch_shapes=[pltpu.VMEM((2, GLA_HEADS, hv, hk), F32),
                        pltpu.VMEM((GLA_CHUNK, 2 * hk + hv), F32)],
        compiler_params=_cparams("arbitrary", "arbitrary"),
        name="gla_core",
    )(exact, qk, v, la, qk, v, la)


def _rope(x, cos, sin_signed):
    lane = lax.broadcasted_iota(jnp.int32, (x.shape[0], 128), 1)
    first = (lane % 32) < 16
    out = []
    for j in range(x.shape[1] // 128):
        xj = x[:, j * 128:(j + 1) * 128]
        partner = jnp.where(first, pltpu.roll(xj, 112, 1), pltpu.roll(xj, 16, 1))
        out.append(xj * cos + partner * sin_signed)
    return jnp.concatenate(out, axis=1)


def _swa_pre_part(x, mod_ref, g_ref, w_ref, cos_ref, sin_ref, q_ref, kv_ref, z_ref):
    h = _prenorm(x, g_ref[...], mod_ref[...]).astype(BF16)
    qw = q_ref.shape[-1]
    kw = kv_ref.shape[-1] // 2
    cos, sin = cos_ref[...], sin_ref[...]
    q = _rope(_dot(h, w_ref[:, :qw]), cos, sin) * (ATT_HEAD_DIM ** -0.5 * LOG2_E)
    q_ref[...] = q.astype(BF16)
    kv_ref[:, :kw] = _rope(_dot(h, w_ref[:, qw:qw + kw]), cos, sin).astype(BF16)
    kv_ref[:, kw:] = _dot(h, w_ref[:, qw + kw:qw + 2 * kw]).astype(BF16)
    z_ref[...] = _dot(h, w_ref[:, qw + 2 * kw:]).astype(BF16)


def _swa_core_kernel(sink_ref, q_ref, kvp_ref, kvc_ref, kvn_ref, kvx_ref, o_ref, *, n_ctx, n_all):
    blk = ATT_BLOCK
    own = [kvc_ref.at[i * blk:(i + 1) * blk] for i in range(ATT_STEP_BLOCKS)]
    window = [kvp_ref] + own + [kvn_ref]
    blocks = [_swa_block(sink_ref, q_ref.at[i * blk:(i + 1) * blk], window[i:i + 3], kvx_ref,
                         o_ref.at[i * blk:(i + 1) * blk], pl.program_id(1) * ATT_STEP_BLOCKS + i, n_ctx, n_all)
              for i in range(ATT_STEP_BLOCKS)]
    for _ in range(3):
        for block in blocks:
            next(block, None)


def _swa_block(sink_ref, q_ref, kv_refs, kvx_ref, o_ref, t, n_ctx, n_all):
    blk = ATT_BLOCK
    kvp_ref, kvc_ref, kvn_ref = kv_refs
    n_keys = 3 * blk + kvx_ref.shape[0]
    kw = kvc_ref.shape[-1] // 2
    n_loc = 3 * blk
    i = lax.broadcasted_iota(jnp.int32, (blk, n_loc), 0)
    j = lax.broadcasted_iota(jnp.int32, (blk, n_loc), 1)
    never = n_keys
    off_prev = jnp.where(t > n_ctx, 0, never)
    off_cur = jnp.where(t >= n_ctx, 0, never)
    off_next = jnp.where(jnp.logical_and(t >= n_ctx, t < n_all - 1), 0, never)
    valid = (((j >= i + off_prev) & (j < blk))
             | ((j >= blk + off_cur) & (j < 2 * blk))
             | ((j >= 2 * blk) & (j <= i + 2 * blk - off_next)))
    bias = jnp.where(valid, 0.0, -jnp.inf)
    logits = []
    for h in range(ATT_KV_HEADS):
        ks = slice(h * ATT_HEAD_DIM, (h + 1) * ATT_HEAD_DIM)
        k = jnp.concatenate([kvp_ref[:, ks], kvc_ref[:, ks], kvn_ref[:, ks], kvx_ref[:, ks]], axis=0)
        q = jnp.concatenate(
            [q_ref[:, (h * ATT_GROUP + g) * ATT_HEAD_DIM:(h * ATT_GROUP + g + 1) * ATT_HEAD_DIM]
             for g in range(ATT_GROUP)], axis=0)
        logits.append(_dot_nt(q, k))
    yield

    chunks = [(h, r0) for h in range(ATT_KV_HEADS) for r0 in range(0, ATT_GROUP * blk, ATT_ROWS)]

    def slabs(h, r0):
        out = []
        for c0 in range(0, n_keys, 128):
            s = logits[h][r0:r0 + ATT_ROWS, c0:c0 + 128]
            if c0 < n_loc:
                s = s + bias[r0 % blk:r0 % blk + ATT_ROWS, c0:c0 + 128]
            out.append(s)
        return out

    sinks = {(h, r0): sink_ref[h * ATT_GROUP + r0 // blk] * LOG2_E for h, r0 in chunks}
    maxes = {}
    for h, r0 in chunks:
        lane_max = functools.reduce(jnp.maximum, slabs(h, r0))
        maxes[h, r0] = jnp.maximum(jnp.max(lane_max, axis=-1, keepdims=True), sinks[h, r0])
    yield

    head_lane = lax.broadcasted_iota(jnp.int32, (n_keys, 2 * ATT_HEAD_DIM), 1) // ATT_HEAD_DIM
    for h in range(ATT_KV_HEADS):
        vs = slice(kw + (h // 2) * 2 * ATT_HEAD_DIM, kw + (h // 2 + 1) * 2 * ATT_HEAD_DIM)
        v = jnp.concatenate([kvp_ref[:, vs], kvc_ref[:, vs], kvn_ref[:, vs], kvx_ref[:, vs]], axis=0)
        v = jnp.where(head_lane == h % 2, v, jnp.ones_like(v))
        probs, sink_terms = [], []
        for _, r0 in chunks[:len(chunks) // ATT_KV_HEADS]:
            p = [jnp.exp2(s - maxes[h, r0]) for s in slabs(h, r0)]
            probs.append(jnp.concatenate(p, axis=1).astype(BF16))
            sink_terms.append(jnp.exp2(sinks[h, r0] - maxes[h, r0]))
        pv = _dot(jnp.concatenate(probs, axis=0), v)
        own, other = (h % 2) * ATT_HEAD_DIM, (1 - h % 2) * ATT_HEAD_DIM
        denom = pv[:, other:other + 1] + jnp.concatenate(sink_terms, axis=0)
        o = pv[:, own:own + ATT_HEAD_DIM] * (1.0 / denom)
        for g in range(ATT_GROUP):
            lo = (h * ATT_GROUP + g) * ATT_HEAD_DIM
            o_ref[:, lo:lo + ATT_HEAD_DIM] = o[g * blk:(g + 1) * blk].astype(BF16)


def _swa_core(q, kv, sink, n_ctx_rows):
    bsz, tt, qw = q.shape
    blk, step = ATT_BLOCK, ATT_BLOCK * ATT_STEP_BLOCKS
    n_all, n_ctx = tt // blk, n_ctx_rows // blk
    kvw = kv.shape[-1]
    kv_spec = lambda idx: pl.BlockSpec((None, blk, kvw), idx)
    return pl.pallas_call(
        functools.partial(_swa_core_kernel, n_ctx=n_ctx, n_all=n_all),
        out_shape=jax.ShapeDtypeStruct((bsz, tt, qw), BF16),
        grid=(bsz, tt // step),
        in_specs=[
            pl.BlockSpec(memory_space=pltpu.SMEM),
            pl.BlockSpec((None, step, qw), lambda b, t: (b, t, 0)),
            kv_spec(lambda b, t: (b, jnp.maximum(t * ATT_STEP_BLOCKS - 1, 0), 0)),
            pl.BlockSpec((None, step, kvw), lambda b, t: (b, t, 0)),
            kv_spec(lambda b, t: (b, jnp.minimum((t + 1) * ATT_STEP_BLOCKS, n_all - 1), 0)),
            pl.BlockSpec((None, n_ctx_rows, kvw), lambda b, t: (b, 0, 0)),
        ],
        out_specs=pl.BlockSpec((None, step, qw), lambda b, t: (b, t, 0)),
        compiler_params=_cparams("parallel", "parallel"),
        name="swa_core",
    )(sink, q, kv, kv, kv, kv)


def _segment_permutation(n):
    r = jnp.arange(n)
    p = (r[None, :] == ((r % 8) * (n // 8) + r // 8)[:, None]).astype(BF16)
    return p, p.T


def _rnn_pre_part(x, mod_ref, g_ref, w_ref, u_ref, z_ref):
    h = _prenorm(x, g_ref[...], mod_ref[...]).astype(BF16)
    rw = u_ref.shape[-1]
    u_ref[...] = _dot(h, w_ref[:, :rw]).astype(BF16)
    z_ref[...] = _dot(h, w_ref[:, rw:]).astype(BF16)


def _shift_rows(cur, other, up):
    sub = lax.broadcasted_iota(jnp.int32, cur.shape, 0)
    if up:
        return jnp.where(sub == 7, pltpu.roll(other, 7, 0), pltpu.roll(cur, 7, 0))
    return jnp.where(sub == 0, pltpu.roll(other, 1, 0), pltpu.roll(cur, 1, 0))


def _segment_scan(a, x, carry, reverse):
    n_g = a.shape[0] // 8
    order = range(n_g - 1, -1, -1) if reverse else range(n_g)
    hs, ps = [None] * n_g, [None] * n_g
    h = p = None
    for g in order:
        ag, xg = a[8 * g:8 * g + 8], x[8 * g:8 * g + 8]
        h = xg if h is None else ag * h + xg
        p = ag if p is None else ag * p
        hs[g], ps[g] = h, p
    sub = lax.broadcasted_iota(jnp.int32, h.shape, 0)
    for s in (1, 2, 4):
        shift = 8 - s if reverse else s
        inside = (sub < 8 - s) if reverse else (sub >= s)
        h = h + p * jnp.where(inside, pltpu.roll(h, shift, 0), 0.0)
        p = p * jnp.where(inside, pltpu.roll(p, shift, 0), 1.0)
    full = h + p * carry
    first, last = (7, 0) if reverse else (0, 7)
    seg_in = jnp.where(sub == first, carry, pltpu.roll(full, 7 if reverse else 1, 0))
    out = jnp.concatenate([hs[g] + ps[g] * seg_in for g in range(n_g)], axis=0)
    return out, jnp.broadcast_to(full[last:last + 1], full.shape)


def _rnn_core_kernel(upf_ref, ucf_ref, unf_ref, upb_ref, ucb_ref, unb_ref,
                     cw_ref, cb_ref, wg_ref, ba_ref, bx_ref, lam_ref,
                     perm_ref, unperm_ref, hf_ref, hb_ref, carry_ref, *, n_ctx, n_all):
    n = pl.program_id(1)

    @pl.when(n == 0)
    def _():
        carry_ref[...] = jnp.zeros_like(carry_ref)

    tiles = (n, _bwd_order(n, n_ctx, n_all))
    prev_ok = [jnp.where(jnp.logical_and(t != 0, t != n_ctx), 1.0, 0.0) for t in tiles]
    next_ok = [jnp.where(jnp.logical_and(t != n_ctx - 1, t != n_all - 1), 1.0, 0.0) for t in tiles]
    refs = ((upf_ref, ucf_ref, unf_ref, hf_ref), (upb_ref, ucb_ref, unb_ref, hb_ref))

    useg = [_dot(perm_ref[...], refs[d][1][...]) for d in range(2)]
    rows = useg[0].shape[0]
    convs, pres = {}, {}
    for d, (p_ref, _, n_ref, _) in enumerate(refs):
        before = p_ref[...].astype(F32)[HALO - 8:] * prev_ok[d]
        after = n_ref[...].astype(F32)[:8] * next_ok[d]
        before2 = pltpu.roll(before, 1, 0)
        for j in range(RNN_HEADS):
            lanes = slice(j * RNN_HD, (j + 1) * RNN_HD)
            u = useg[d][:, lanes]
            ext = jnp.concatenate([
                _shift_rows(u[rows - 16:rows - 8], before2[:, lanes], up=False),
                _shift_rows(u[rows - 8:], before[:, lanes], up=False),
                u,
                _shift_rows(u[:8], after[:, lanes], up=True)], axis=0)
            conv = cb_ref[:, lanes]
            for tap in range(CONV_W):
                conv = conv + ext[8 * tap:8 * tap + rows] * cw_ref[tap:tap + 1, lanes]
            convs[d, j] = conv
            pres[d, j] = _dot(conv.astype(BF16), wg_ref[d, j])

    half = RNN_HEADS // 2
    for j0 in (0, half):
        for d in range(2):
            outs = []
            for j in range(j0, j0 + half):
                pre, conv = pres[d, j], convs[d, j]
                u = jnp.tanh(pre[:, :RNN_HD] + ba_ref[d, j]) + 1.0
                i = jnp.tanh(pre[:, RNN_HD:] + bx_ref[d, j]) + 1.0
                rate = (0.5 * LRU_C) * _softplus(-lam_ref[d, j])
                a = jnp.exp2((-LOG2_E * rate) * u)
                y = 1.0 - a * a
                x = jnp.where(y > 0.0, y * lax.rsqrt(y), 0.0) * i * conv
                hs, carry = _segment_scan(a, x, carry_ref[d, j], reverse=(d == 1))
                carry_ref[d, j] = carry
                outs.append(hs.astype(BF16))
            h = _dot(unperm_ref[...], jnp.concatenate(outs, axis=1))
            refs[d][3][:, j0 * RNN_HD:(j0 + half) * RNN_HD] = h.astype(BF16)


def _rnn_core(u, conv_w, conv_b, wg, b_a, b_x, lam, n_ctx_rows):
    bsz, tt, rw = u.shape
    tile = RNN_TILE
    n_all, n_ctx = tt // tile, n_ctx_rows // tile
    per = tile // HALO
    n_halo = tt // HALO
    fwd = lambda n: n
    bwd = lambda n: _bwd_order(n, n_ctx, n_all)

    def specs(order):
        return [
            pl.BlockSpec((None, HALO, rw), lambda b, n: (b, jnp.maximum(order(n) * per - 1, 0), 0)),
            pl.BlockSpec((None, tile, rw), lambda b, n: (b, order(n), 0)),
            pl.BlockSpec((None, HALO, rw), lambda b, n: (b, jnp.minimum((order(n) + 1) * per, n_halo - 1), 0)),
        ]

    out_spec = lambda order: pl.BlockSpec((None, tile, rw), lambda b, n: (b, order(n), 0))
    perm, unperm = _segment_permutation(tile)
    consts = (conv_w, conv_b, wg, b_a, b_x, lam, perm, unperm)
    return pl.pallas_call(
        functools.partial(_rnn_core_kernel, n_ctx=n_ctx, n_all=n_all),
        out_shape=[jax.ShapeDtypeStruct((bsz, tt, rw), BF16)] * 2,
        grid=(bsz, n_all),
        in_specs=specs(fwd) + specs(bwd) + [_full_spec(c.shape) for c in consts],
        out_specs=[out_spec(fwd), out_spec(bwd)],
        scratch_shapes=[pltpu.VMEM((2, RNN_HEADS, 8, RNN_HD), F32)],
        compiler_params=_cparams("arbitrary", "arbitrary"),
        name="rglru_core",
    )(u, u, u, u, u, u, *consts)


def _residual(u, x, mod_ref, gp_ref, w_ref):
    y = _dot(u.astype(BF16), w_ref[...])
    return x + mod_ref[2:3] * (_rms(y) * gp_ref[...])


def _gla_post_part(x, mod_ref, gp_ref, w_ref, of_ref, ob_ref, z_ref, gh_ref, rows=slice(None)):
    o = of_ref[rows].astype(F32) + ob_ref[rows].astype(F32)
    hv = o.shape[-1] // GLA_HEADS
    o = jnp.concatenate([_rms(o[:, h * hv:(h + 1) * hv]) for h in range(GLA_HEADS)], axis=1)
    u = o * gh_ref[...] * _silu(z_ref[rows].astype(F32))
    return _residual(u, x, mod_ref, gp_ref, w_ref)


def _swa_post_part(x, mod_ref, gp_ref, w_ref, a_ref, z_ref, rows=slice(None)):
    u = a_ref[rows].astype(F32) * _silu(z_ref[rows].astype(F32))
    return _residual(u, x, mod_ref, gp_ref, w_ref)


def _rnn_post_part(x, mod_ref, gp_ref, w_ref, hf_ref, hb_ref, z_ref, rows=slice(None)):
    u = (hf_ref[rows].astype(F32) + hb_ref[rows].astype(F32)) * _silu(z_ref[rows].astype(F32))
    return _residual(u, x, mod_ref, gp_ref, w_ref)


class _Pre(NamedTuple):
    name: str
    part: Callable
    mod: jax.Array
    g_pre: jax.Array
    consts: tuple
    row_consts: tuple
    widths: tuple
    dtypes: tuple
    stats: int


class _Post(NamedTuple):
    name: str
    part: Callable
    mod: jax.Array
    g_post: jax.Array
    w_out: jax.Array
    acts: tuple
    vecs: tuple


def _pre_kernel(*refs, part, n_x, n_ctx):
    part(_token_tile(refs[:n_x], n_ctx), *refs[n_x:])


def _post_kernel(*refs, part, n_x, n_ctx):
    refs[-1][...] = part(_token_tile(refs[:n_x], n_ctx), *refs[n_x:-1])


def _mid_kernel(*refs, post_part, n_post, pre_part, n_pre, n_x, n_ctx):
    post_in, pre_in = refs[n_x:n_x + n_post], refs[n_x + n_post:n_x + n_post + n_pre]
    xo_ref, pre_out = refs[n_x + n_post + n_pre], refs[n_x + n_post + n_pre + 1:]
    x_in = _token_tile(refs[:n_x], n_ctx)
    half = x_in.shape[0] // 2
    x = jnp.concatenate([post_part(x_in[rows], *post_in, rows=rows)
                         for rows in (slice(0, half), slice(half, 2 * half))], axis=0)
    xo_ref[...] = x
    pre_part(x, *pre_in, *pre_out)


def _stream_shape(xs):
    if isinstance(xs, tuple):
        bsz, n_ctx_rows, d = xs[0].shape
        return bsz, n_ctx_rows + xs[1].shape[1], d
    return xs.shape


def _pre_io(pre, bsz, tt, d, n_ctx):
    operands = (pre.mod, pre.g_pre, *pre.consts, *pre.row_consts)
    in_specs = ([_mod_spec(d, n_ctx, bsz), _full_spec(pre.g_pre.shape)]
                + [_full_spec(c.shape) for c in pre.consts]
                + [pl.BlockSpec((ROW_TILE, r.shape[-1]), lambda b, t: (t, 0)) for r in pre.row_consts])
    out_shape = ([jax.ShapeDtypeStruct((bsz, tt, w), dt) for w, dt in zip(pre.widths, pre.dtypes)]
                 + [jax.ShapeDtypeStruct((bsz, tt // ROW_TILE, 8, 128), F32)] * pre.stats)
    out_specs = ([_row_spec(w) for w in pre.widths]
                 + [pl.BlockSpec((None, None, 8, 128), lambda b, t: (b, t, 0, 0))] * pre.stats)
    return operands, in_specs, out_shape, out_specs


def _post_io(post, bsz, d, n_ctx, off):
    operands = (post.mod, post.g_post, post.w_out, *post.acts, *post.vecs)
    in_specs = ([_mod_spec(d, n_ctx, bsz, off), _full_spec(post.g_post.shape), _full_spec(post.w_out.shape)]
                + [_row_spec(a.shape[-1], off) for a in post.acts]
                + [_full_spec(v.shape) for v in post.vecs])
    return operands, in_specs


def _pre_call(pre, xs, n_ctx_rows):
    bsz, tt, d = _stream_shape(xs)
    n_ctx = n_ctx_rows // ROW_TILE
    streams = xs if isinstance(xs, tuple) else (xs,)
    operands, in_specs, out_shape, out_specs = _pre_io(pre, bsz, tt, d, n_ctx)
    return pl.pallas_call(
        functools.partial(_pre_kernel, part=pre.part, n_x=len(streams), n_ctx=n_ctx),
        out_shape=out_shape,
        grid=(bsz, tt // ROW_TILE),
        in_specs=_token_specs(xs, n_ctx) + in_specs,
        out_specs=out_specs,
        compiler_params=_cparams("parallel", "parallel"),
        name=pre.name,
    )(*streams, *operands)


def _post_call(post, xs, n_ctx_rows, latent_only):
    bsz, tt, d = _stream_shape(xs)
    n_ctx = n_ctx_rows // ROW_TILE
    off = n_ctx if latent_only else 0
    n_tiles = tt // ROW_TILE - off
    streams = xs if isinstance(xs, tuple) else (xs,)
    operands, in_specs = _post_io(post, bsz, d, n_ctx, off)
    return pl.pallas_call(
        functools.partial(_post_kernel, part=post.part, n_x=len(streams), n_ctx=n_ctx - off),
        out_shape=jax.ShapeDtypeStruct((bsz, n_tiles * ROW_TILE, d), F32),
        grid=(bsz, n_tiles),
        in_specs=_token_specs(xs, n_ctx, off) + in_specs,
        out_specs=_row_spec(d),
        compiler_params=_cparams("parallel", "parallel"),
        name=post.name,
    )(*streams, *operands)


def _mid_call(post, pre, xs, n_ctx_rows):
    bsz, tt, d = _stream_shape(xs)
    n_ctx = n_ctx_rows // ROW_TILE
    streams = xs if isinstance(xs, tuple) else (xs,)
    post_ops, post_specs = _post_io(post, bsz, d, n_ctx, 0)
    pre_ops, pre_specs, pre_shape, pre_out_specs = _pre_io(pre, bsz, tt, d, n_ctx)
    outs = pl.pallas_call(
        functools.partial(_mid_kernel, post_part=post.part, n_post=len(post_ops),
                          pre_part=pre.part, n_pre=len(pre_ops), n_x=len(streams), n_ctx=n_ctx),
        out_shape=[jax.ShapeDtypeStruct((bsz, tt, d), F32)] + pre_shape,
        grid=(bsz, tt // ROW_TILE),
        in_specs=_token_specs(xs, n_ctx) + post_specs + pre_specs,
        out_specs=[_row_spec(d)] + pre_out_specs,
        compiler_params=_cparams("parallel", "parallel"),
        name=post.name + "_" + pre.name,
    )(*streams, *post_ops, *pre_ops)
    return outs[0], outs[1:]


def _gla_layer(mod, g_pre, g_post, w_in, w_g1, w_g2, b_g, g_head, w_out, n_ctx_rows):
    d = w_in.shape[0]
    rank, dk = w_g2.shape[1], w_g2.shape[2]
    dv = w_out.shape[0]
    wg1 = jnp.zeros((d, GLA_GATE_PAD), F32).at[:, :2 * rank].set(jnp.concatenate([w_g1[0], w_g1[1]], axis=1))
    wg2 = jnp.zeros((GLA_GATE_PAD, 2 * dk), F32)
    wg2 = wg2.at[:rank, :dk].set(w_g2[0]).at[rank:2 * rank, dk:].set(w_g2[1])
    pre = _Pre("gla_pre", functools.partial(_gla_pre_part, dk=dk), mod, g_pre,
               (w_in.astype(BF16), wg1.astype(BF16), wg2.astype(BF16), b_g.reshape(1, 2 * dk)), (),
               (2 * dk, dv, dv, 2 * dk), (BF16, BF16, BF16, BF16), 1)

    def mixer(outs):
        qk, v, z, la, peak = outs
        exact = (peak[:, :, 0, 0] * (GLA_BLOCK // 2) > GLA_SAFE_EXPONENT).astype(jnp.int32)
        o_f, o_b = _gla_core(qk, v, la, exact, n_ctx_rows)
        return _Post("gla_post", _gla_post_part, mod, g_post, w_out.astype(BF16), (o_f, o_b, z),
                     (g_head.reshape(1, dv),))

    return pre, mixer


def _swa_layer(mod, g_pre, g_post, w_in, sink, w_out, rope, n_ctx_rows):
    qw = w_out.shape[0]
    kw = (w_in.shape[1] - 2 * qw) // 2
    pre = _Pre("swa_pre", _swa_pre_part, mod, g_pre, (w_in.astype(BF16),), rope,
               (qw, 2 * kw, qw), (BF16, BF16, BF16), 0)

    def mixer(outs):
        q, kv, z = outs
        a = _swa_core(q, kv, sink, n_ctx_rows)
        return _Post("swa_post", _swa_post_part, mod, g_post, w_out.astype(BF16), (a, z), ())

    return pre, mixer


def _rnn_layer(mod, g_pre, g_post, w_in, conv_w, conv_b, w_ra, b_ra, w_ri, b_ri, lam, w_out, n_ctx_rows):
    rw = w_out.shape[0]
    pre = _Pre("rglru_pre", _rnn_pre_part, mod, g_pre, (w_in.astype(BF16),), (), (rw, rw), (BF16, BF16), 0)
    wg = jnp.concatenate([w_ra, w_ri], axis=-1).astype(BF16)
    vec = lambda a: a.reshape(2, RNN_HEADS, 1, RNN_HD)

    def mixer(outs):
        u, z = outs
        h_f, h_b = _rnn_core(u, 0.5 * conv_w, 0.5 * conv_b.reshape(1, rw), wg, vec(0.5 * b_ra), vec(0.5 * b_ri),
                             vec(lam), n_ctx_rows)
        return _Post("rglru_post", _rnn_post_part, mod, g_post, w_out.astype(BF16), (h_f, h_b, z), ())

    return pre, mixer


def _rope_tables(t, n_ctx_rows):
    pairs = ATT_HEAD_DIM // 4
    pos = jnp.arange(t, dtype=jnp.int32)
    row = (pos // GRID_W).astype(F32)
    col = (pos % GRID_W).astype(F32)
    freqs = ROPE_BASE ** (-jnp.arange(pairs, dtype=F32) / pairs)
    ang_r, ang_c = row[:, None] * freqs, col[:, None] * freqs
    cos = jnp.concatenate([jnp.cos(ang_r)] * 2 + [jnp.cos(ang_c)] * 2, axis=1)
    sin = jnp.concatenate([-jnp.sin(ang_r), jnp.sin(ang_r), -jnp.sin(ang_c), jnp.sin(ang_c)], axis=1)
    cos = jnp.concatenate([jnp.ones((n_ctx_rows, ATT_HEAD_DIM), F32), cos], axis=0)
    sin = jnp.concatenate([jnp.zeros((n_ctx_rows, ATT_HEAD_DIM), F32), sin], axis=0)
    return jnp.tile(cos, (1, 2)), jnp.tile(sin, (1, 2))


def kernel(x, c, ctx, c_ctx, w_mod, b_mod, g_pre, g_post, a_w_in, a_w_g1, a_w_g2, a_b_g, a_g_head, a_w_out, b_w_in, b_sink, b_w_out, c_w_in, c_conv_w, c_conv_b, c_w_ra, c_b_ra, c_w_ri, c_b_ri, c_lam, c_w_out):
    bsz, t, d = x.shape
    n_ctx_rows = ctx.shape[1]
    depth = w_mod.shape[0]
    assert bsz < MOD_ROWS and n_ctx_rows % ROW_TILE == 0 and t % ROW_TILE == 0

    cvec = jnp.zeros((MOD_ROWS, d), F32).at[:bsz].set(c).at[bsz].set(c_ctx)
    mods = _modulation(cvec, w_mod, b_mod)
    rope = _rope_tables(t, n_ctx_rows)
    layers = []
    for i in range(depth):
        kind, j = i % N_MIXERS, i // N_MIXERS
        gpre, gpost = g_pre[i].reshape(1, d), g_post[i].reshape(1, d)
        if kind == 0:
            layers.append(_gla_layer(mods[i], gpre, gpost, a_w_in[j], a_w_g1[j], a_w_g2[j], a_b_g[j],
                                     a_g_head[j], a_w_out[j], n_ctx_rows))
        elif kind == 1:
            layers.append(_swa_layer(mods[i], gpre, gpost, b_w_in[j], b_sink[j], b_w_out[j], rope, n_ctx_rows))
        else:
            layers.append(_rnn_layer(mods[i], gpre, gpost, c_w_in[j], c_conv_w[j], c_conv_b[j], c_w_ra[j],
                                     c_b_ra[j], c_w_ri[j], c_b_ri[j], c_lam[j], c_w_out[j], n_ctx_rows))

    xs = (ctx, x)
    outs = _pre_call(layers[0][0], xs, n_ctx_rows)
    for i in range(depth - 1):
        xs, outs = _mid_call(layers[i][1](outs), layers[i + 1][0], xs, n_ctx_rows)
    return _post_call(layers[-1][1](outs), xs, n_ctx_rows, latent_only=True)
```

```python
import functools
from typing import Callable, NamedTuple

import jax
import jax.numpy as jnp
from jax import lax
from jax.experimental import pallas as pl
from jax.experimental.pallas import tpu as pltpu

F32 = jnp.float32
BF16 = jnp.bfloat16

EPS = 1e-6
GRID_W = 64
N_MIXERS = 3

GLA_HEADS = 4
GLA_TAU = 16.0
GLA_GATE_PAD = 128

ATT_HEAD_DIM = 64
ATT_KV_HEADS = 4
ATT_GROUP = 4
ATT_BLOCK = 128
ATT_ROWS = 32
ATT_STEP_BLOCKS = 2
ROPE_BASE = 10000.0
LOG2_E = 1.4426950408889634

RNN_HEADS = 10
RNN_HD = 128
CONV_W = 4
LRU_C = 8.0

ROW_TILE = 256
FINAL_TILE = 512
GLA_TILE = 256
GLA_CHUNK = 128
GLA_BLOCK = 64
GLA_SAFE_EXPONENT = 60.0
RNN_TILE = 256
HALO = 16
MOD_ROWS = 8
VMEM_LIMIT = 48 * 1024 * 1024


def _cparams(*sem):
    return pltpu.CompilerParams(dimension_semantics=sem, vmem_limit_bytes=VMEM_LIMIT)


def _sigmoid(x):
    return 0.5 * jnp.tanh(0.5 * x) + 0.5


def _silu(x):
    return x * _sigmoid(x)


def _softplus(x):
    return jnp.maximum(x, 0.0) + jnp.log1p(jnp.exp(-jnp.abs(x)))


def _log_sigmoid(x):
    return jnp.minimum(x, 0.0) - jnp.log(1.0 + jnp.exp(-jnp.abs(x)))


def _split3(a):
    hi = a.astype(BF16)
    r = a - hi.astype(F32)
    mid = r.astype(BF16)
    lo = (r - mid.astype(F32)).astype(BF16)
    return hi, mid, lo


def _dot(a, b):
    return jnp.dot(a, b, preferred_element_type=F32)


def _dot_nt(a, b):
    return lax.dot_general(a, b, (((1,), (1,)), ((), ())), preferred_element_type=F32)


def _dot_tn(a, b):
    return lax.dot_general(a, b, (((0,), (0,)), ((), ())), preferred_element_type=F32)


def _rms(x):
    return x * lax.rsqrt(jnp.mean(x * x, axis=-1, keepdims=True) + EPS)


def _mod_kernel(c_ref, w_ref, b_ref, o_ref):
    s = _silu(c_ref[...])
    s_hi, s_mid, s_lo = _split3(s)
    w_hi, w_mid, w_lo = _split3(w_ref[...])
    acc = _dot(s_hi, w_hi)
    acc += _dot(s_hi, w_mid) + _dot(s_mid, w_hi)
    acc += _dot(s_hi, w_lo) + _dot(s_mid, w_mid) + _dot(s_lo, w_hi)
    o_ref[...] = acc + b_ref[...]


def _modulation(cvec, w_mod, b_mod):
    depth, d, d3 = w_mod.shape
    out = pl.pallas_call(
        _mod_kernel,
        out_shape=jax.ShapeDtypeStruct((depth, MOD_ROWS, d3), F32),
        grid=(depth, d3 // d),
        in_specs=[
            pl.BlockSpec((MOD_ROWS, d), lambda i, j: (0, 0)),
            pl.BlockSpec((None, d, d), lambda i, j: (i, 0, j)),
            pl.BlockSpec((None, 1, d), lambda i, j: (i, 0, j)),
        ],
        out_specs=pl.BlockSpec((None, MOD_ROWS, d), lambda i, j: (i, 0, j)),
        compiler_params=_cparams("parallel", "parallel"),
        name="modulation",
    )(cvec, w_mod, b_mod.reshape(depth, 1, d3))
    return out.reshape(depth, MOD_ROWS, 3, d)


def _prenorm(x, g, mod):
    return _rms(x) * g * (1.0 + mod[1:2]) + mod[0:1]


def _mod_spec(d, n_ctx_tiles, ctx_row, offset=0):
    def index(b, t):
        return (jnp.where(t + offset < n_ctx_tiles, ctx_row, b), 0, 0)
    return pl.BlockSpec((None, 3, d), index)


def _row_spec(width, offset=0):
    return pl.BlockSpec((None, ROW_TILE, width), lambda b, t: (b, t + offset, 0))


def _token_specs(xs, n_ctx, offset=0):
    if not isinstance(xs, tuple):
        return [_row_spec(xs.shape[-1], offset)]
    d = xs[0].shape[-1]
    return [pl.BlockSpec((None, ROW_TILE, d), lambda b, t: (b, jnp.minimum(t + offset, n_ctx - 1), 0)),
            pl.BlockSpec((None, ROW_TILE, d), lambda b, t: (b, jnp.maximum(t + offset - n_ctx, 0), 0))]


def _token_tile(x_refs, n_ctx_tiles):
    if len(x_refs) == 1:
        return x_refs[0][...]
    return jnp.where(pl.program_id(1) < n_ctx_tiles, x_refs[0][...], x_refs[1][...])


def _full_spec(shape):
    zeros = (0,) * len(shape)
    return pl.BlockSpec(shape, lambda b, t: zeros)


def _gla_pre_part(x, mod_ref, g_ref, w_ref, wg1_ref, wg2_ref, bg_ref, qk_ref, v_ref, z_ref, la_ref, peak_ref,
                  *, dk):
    h = _prenorm(x, g_ref[...], mod_ref[...]).astype(BF16)
    hk = dk // GLA_HEADS
    dv = v_ref.shape[-1]
    g1 = _dot(h, wg1_ref[...]).astype(BF16)
    gate = _dot(g1, wg2_ref[...]) + bg_ref[...]
    la = _log_sigmoid(gate) / GLA_TAU
    la_ref[...] = la.astype(BF16)
    peak = jnp.max(jnp.max(-la, axis=1, keepdims=True), axis=0, keepdims=True)
    peak_ref[...] = jnp.broadcast_to(peak, peak_ref.shape)
    q = _dot(h, w_ref[:, :dk]) * hk ** -0.5
    qk_ref[:, :dk] = q.astype(BF16)
    qk_ref[:, dk:] = _dot(h, w_ref[:, dk:2 * dk]).astype(BF16)
    v_ref[...] = _dot(h, w_ref[:, 2 * dk:2 * dk + dv]).astype(BF16)
    z_ref[...] = _dot(h, w_ref[:, 2 * dk + dv:]).astype(BF16)


def _gla_core_kernel(exact_ref, qkf_ref, vf_ref, laf_ref, qkb_ref, vb_ref, lab_ref,
                     of_ref, ob_ref, st_ref, row_ref, *, dk, n_ctx, n_all):
    b, n = pl.program_id(0), pl.program_id(1)

    @pl.when(n == 0)
    def _():
        st_ref[...] = jnp.zeros_like(st_ref)

    io = ((qkf_ref, vf_ref, laf_ref, of_ref), (qkb_ref, vb_ref, lab_ref, ob_ref))
    n_sub = qkf_ref.shape[0] // GLA_CHUNK

    def run(exact):
        chunks = [_gla_chunk_pair(io, (s * GLA_CHUNK, (n_sub - 1 - s) * GLA_CHUNK), st_ref, dk,
                                  row_ref if exact else None) for s in range(n_sub)]
        for _ in range(3):
            for chunk in chunks:
                next(chunk, None)

    out_of_range = exact_ref[b, n] + exact_ref[b, _bwd_order(n, n_ctx, n_all)]
    pl.when(out_of_range == 0)(lambda: run(False))
    pl.when(out_of_range != 0)(lambda: run(True))


def _gla_exact_intra(d, q, k, v, b, row_ref):
    c, hk = q.shape
    row_ref[:, :hk] = k
    row_ref[:, hk:2 * hk] = b
    row_ref[:, 2 * hk:] = v.astype(F32)
    t_idx = lax.broadcasted_iota(jnp.int32, (c, 1), 0)

    def keys(g, acc):
        group = row_ref[pl.ds(pl.multiple_of(g * 8, 8), 8), :]
        for r in range(8):
            s = g * 8 + r
            k_s, b_s, v_s = group[r:r + 1, :hk], group[r:r + 1, hk:2 * hk], group[r:r + 1, 2 * hk:]
            decay = jnp.exp(jnp.minimum(b - b_s, 0.0))
            col = jnp.sum(q * decay * k_s, axis=1, keepdims=True)
            causal = (t_idx >= s) if d == 0 else (t_idx <= s)
            acc = acc + jnp.where(causal, col, 0.0) * v_s
        return acc

    return lax.fori_loop(0, c // 8, keys, jnp.zeros((c, v.shape[1]), F32))


def _gla_chunk_pair(io, base, st_ref, dk, row_ref):
    c = GLA_CHUNK
    hk = dk // GLA_HEADS
    hv = io[0][1].shape[-1] // GLA_HEADS
    row = lax.broadcasted_iota(jnp.int32, (c, c), 0)
    col = lax.broadcasted_iota(jnp.int32, (c, c), 1)
    blk = GLA_BLOCK
    keeps = (col <= row, col >= row)
    lasts = (c - 1, 0)
    tile = [slice(r0, r0 + c) for r0 in base]
    cums = []
    for d, (_, _, la_ref, _) in enumerate(io):
        tri = jnp.where(keeps[d], 1.0, 0.0).astype(BF16)
        cums.append(_dot(tri, la_ref[tile[d], d * dk:(d + 1) * dk]))
    yield

    stage = {}
    for d, (qk_ref, v_ref, _, _) in enumerate(io):
        for h in range(GLA_HEADS):
            b = cums[d][:, h * hk:(h + 1) * hk]
            b_last = b[lasts[d]:lasts[d] + 1]
            q = qk_ref[tile[d], h * hk:(h + 1) * hk].astype(F32)
            k = qk_ref[tile[d], dk + h * hk:dk + (h + 1) * hk].astype(F32)
            v = v_ref[tile[d], h * hv:(h + 1) * hv]
            scores = []
            for i in (range(c // blk) if row_ref is None else ()):
                rows = slice(i * blk, (i + 1) * blk)
                keys = slice(0, (i + 1) * blk) if d == 0 else slice(i * blk, c)
                b_ref = b[i * blk + blk // 2:i * blk + blk // 2 + 1]
                qe = (q[rows] * jnp.exp(b[rows] - b_ref)).astype(BF16)
                ke = (k[keys] * jnp.exp(b_ref - b[keys])).astype(BF16)
                scores.append((rows, keys, _dot_nt(qe, ke)))
            qs = (q * jnp.exp(b)).astype(BF16)
            kd = (k * jnp.exp(b_last - b)).astype(BF16)
            st = st_ref[d, h]
            inter = _dot_nt(qs, st.astype(BF16))
            st_ref[d, h] = st * jnp.exp(b_last) + _dot_tn(v, kd)
            stage[d, h] = (scores, inter, v, (q, k, b))
    yield

    for d, (_, _, _, o_ref) in enumerate(io):
        for h in range(GLA_HEADS):
            scores, inter, v, (q, k, b) = stage[d, h]
            if row_ref is not None:
                o = _gla_exact_intra(d, q, k, v, b, row_ref) + inter
                o_ref[tile[d], h * hv:(h + 1) * hv] = o.astype(BF16)
            for rows, keys, s in scores:
                t_idx = lax.broadcasted_iota(jnp.int32, s.shape, 0) + rows.start
                s_idx = lax.broadcasted_iota(jnp.int32, s.shape, 1) + keys.start
                p = jnp.where((s_idx <= t_idx) if d == 0 else (s_idx >= t_idx), s, 0.0).astype(BF16)
                o = _dot(p, v[keys]) + inter[rows]
                o_ref[base[d] + rows.start:base[d] + rows.stop, h * hv:(h + 1) * hv] = o.astype(BF16)


def _bwd_order(n, n_ctx, n_all):
    return jnp.where(n < n_ctx, n_ctx - 1 - n, n_all - 1 - (n - n_ctx))


def _gla_core(qk, v, la, exact, n_ctx_rows):
    bsz, tt, dv = v.shape
    dk = qk.shape[-1] // 2
    c = GLA_TILE
    n_all, n_ctx = tt // c, n_ctx_rows // c
    hk, hv = dk // GLA_HEADS, dv // GLA_HEADS
    fwd = lambda b, n: (b, n, 0)
    bwd = lambda b, n: (b, _bwd_order(n, n_ctx, n_all), 0)
    spec = lambda w, idx: pl.BlockSpec((None, c, w), idx)
    return pl.pallas_call(
        functools.partial(_gla_core_kernel, dk=dk, n_ctx=n_ctx, n_all=n_all),
        out_shape=[jax.ShapeDtypeStruct((bsz, tt, dv), BF16)] * 2,
        grid=(bsz, n_all),
        in_specs=[pl.BlockSpec(memory_space=pltpu.SMEM),
                  spec(2 * dk, fwd), spec(dv, fwd), spec(2 * dk, fwd),
                  spec(2 * dk, bwd), spec(dv, bwd), spec(2 * dk, bwd)],
        out_specs=[spec(dv, fwd), spec(dv, bwd)],
        scratch_shapes=[pltpu.VMEM((2, GLA_HEADS, hv, hk), F32),
                        pltpu.VMEM((GLA_CHUNK, 2 * hk + hv), F32)],
        compiler_params=_cparams("arbitrary", "arbitrary"),
        name="gla_core",
    )(exact, qk, v, la, qk, v, la)


def _rope(x, cos, sin_signed):
    lane = lax.broadcasted_iota(jnp.int32, (x.shape[0], 128), 1)
    first = (lane % 32) < 16
    out = []
    for j in range(x.shape[1] // 128):
        xj = x[:, j * 128:(j + 1) * 128]
        partner = jnp.where(first, pltpu.roll(xj, 112, 1), pltpu.roll(xj, 16, 1))
        out.append(xj * cos + partner * sin_signed)
    return jnp.concatenate(out, axis=1)


def _swa_pre_part(x, mod_ref, g_ref, w_ref, cos_ref, sin_ref, q_ref, kv_ref, z_ref):
    h = _prenorm(x, g_ref[...], mod_ref[...]).astype(BF16)
    qw = q_ref.shape[-1]
    kw = kv_ref.shape[-1] // 2
    cos, sin = cos_ref[...], sin_ref[...]
    q = _rope(_dot(h, w_ref[:, :qw]), cos, sin) * (ATT_HEAD_DIM ** -0.5 * LOG2_E)
    q_ref[...] = q.astype(BF16)
    kv_ref[:, :kw] = _rope(_dot(h, w_ref[:, qw:qw + kw]), cos, sin).astype(BF16)
    kv_ref[:, kw:] = _dot(h, w_ref[:, qw + kw:qw + 2 * kw]).astype(BF16)
    z_ref[...] = _dot(h, w_ref[:, qw + 2 * kw:]).astype(BF16)


def _swa_core_kernel(sink_ref, q_ref, kvp_ref, kvc_ref, kvn_ref, kvx_ref, o_ref, *, n_ctx, n_all):
    blk = ATT_BLOCK
    own = [kvc_ref.at[i * blk:(i + 1) * blk] for i in range(ATT_STEP_BLOCKS)]
    window = [kvp_ref] + own + [kvn_ref]
    blocks = [_swa_block(sink_ref, q_ref.at[i * blk:(i + 1) * blk], window[i:i + 3], kvx_ref,
                         o_ref.at[i * blk:(i + 1) * blk], pl.program_id(1) * ATT_STEP_BLOCKS + i, n_ctx, n_all)
              for i in range(ATT_STEP_BLOCKS)]
    for _ in range(3):
        for block in blocks:
            next(block, None)


def _swa_block(sink_ref, q_ref, kv_refs, kvx_ref, o_ref, t, n_ctx, n_all):
    blk = ATT_BLOCK
    kvp_ref, kvc_ref, kvn_ref = kv_refs
    n_keys = 3 * blk + kvx_ref.shape[0]
    kw = kvc_ref.shape[-1] // 2
    n_loc = 3 * blk
    i = lax.broadcasted_iota(jnp.int32, (blk, n_loc), 0)
    j = lax.broadcasted_iota(jnp.int32, (blk, n_loc), 1)
    never = n_keys
    off_prev = jnp.where(t > n_ctx, 0, never)
    off_cur = jnp.where(t >= n_ctx, 0, never)
    off_next = jnp.where(jnp.logical_and(t >= n_ctx, t < n_all - 1), 0, never)
    valid = (((j >= i + off_prev) & (j < blk))
             | ((j >= blk + off_cur) & (j < 2 * blk))
             | ((j >= 2 * blk) & (j <= i + 2 * blk - off_next)))
    bias = jnp.where(valid, 0.0, -jnp.inf)
    logits = []
    for h in range(ATT_KV_HEADS):
        ks = slice(h * ATT_HEAD_DIM, (h + 1) * ATT_HEAD_DIM)
        k = jnp.concatenate([kvp_ref[:, ks], kvc_ref[:, ks], kvn_ref[:, ks], kvx_ref[:, ks]], axis=0)
        q = jnp.concatenate(
            [q_ref[:, (h * ATT_GROUP + g) * ATT_HEAD_DIM:(h * ATT_GROUP + g + 1) * ATT_HEAD_DIM]
             for g in range(ATT_GROUP)], axis=0)
        logits.append(_dot_nt(q, k))
    yield

    chunks = [(h, r0) for h in range(ATT_KV_HEADS) for r0 in range(0, ATT_GROUP * blk, ATT_ROWS)]

    def slabs(h, r0):
        out = []
        for c0 in range(0, n_keys, 128):
            s = logits[h][r0:r0 + ATT_ROWS, c0:c0 + 128]
            if c0 < n_loc:
                s = s + bias[r0 % blk:r0 % blk + ATT_ROWS, c0:c0 + 128]
            out.append(s)
        return out

    sinks = {(h, r0): sink_ref[h * ATT_GROUP + r0 // blk] * LOG2_E for h, r0 in chunks}
    maxes = {}
    for h, r0 in chunks:
        lane_max = functools.reduce(jnp.maximum, slabs(h, r0))
        maxes[h, r0] = jnp.maximum(jnp.max(lane_max, axis=-1, keepdims=True), sinks[h, r0])
    yield

    head_lane = lax.broadcasted_iota(jnp.int32, (n_keys, 2 * ATT_HEAD_DIM), 1) // ATT_HEAD_DIM
    for h in range(ATT_KV_HEADS):
        vs = slice(kw + (h // 2) * 2 * ATT_HEAD_DIM, kw + (h // 2 + 1) * 2 * ATT_HEAD_DIM)
        v = jnp.concatenate([kvp_ref[:, vs], kvc_ref[:, vs], kvn_ref[:, vs], kvx_ref[:, vs]], axis=0)
        v = jnp.where(head_lane == h % 2, v, jnp.ones_like(v))
        probs, sink_terms = [], []
        for _, r0 in chunks[:len(chunks) // ATT_KV_HEADS]:
            p = [jnp.exp2(s - maxes[h, r0]) for s in slabs(h, r0)]
            probs.append(jnp.concatenate(p, axis=1).astype(BF16))
            sink_terms.append(jnp.exp2(sinks[h, r0] - maxes[h, r0]))
        pv = _dot(jnp.concatenate(probs, axis=0), v)
        own, other = (h % 2) * ATT_HEAD_DIM, (1 - h % 2) * ATT_HEAD_DIM
        denom = pv[:, other:other + 1] + jnp.concatenate(sink_terms, axis=0)
        o = pv[:, own:own + ATT_HEAD_DIM] * (1.0 / denom)
        for g in range(ATT_GROUP):
            lo = (h * ATT_GROUP + g) * ATT_HEAD_DIM
            o_ref[:, lo:lo + ATT_HEAD_DIM] = o[g * blk:(g + 1) * blk].astype(BF16)


def _swa_core(q, kv, sink, n_ctx_rows):
    bsz, tt, qw = q.shape
    blk, step = ATT_BLOCK, ATT_BLOCK * ATT_STEP_BLOCKS
    n_all, n_ctx = tt // blk, n_ctx_rows // blk
    kvw = kv.shape[-1]
    kv_spec = lambda idx: pl.BlockSpec((None, blk, kvw), idx)
    return pl.pallas_call(
        functools.partial(_swa_core_kernel, n_ctx=n_ctx, n_all=n_all),
        out_shape=jax.ShapeDtypeStruct((bsz, tt, qw), BF16),
        grid=(bsz, tt // step),
        in_specs=[
            pl.BlockSpec(memory_space=pltpu.SMEM),
            pl.BlockSpec((None, step, qw), lambda b, t: (b, t, 0)),
            kv_spec(lambda b, t: (b, jnp.maximum(t * ATT_STEP_BLOCKS - 1, 0), 0)),
            pl.BlockSpec((None, step, kvw), lambda b, t: (b, t, 0)),
            kv_spec(lambda b, t: (b, jnp.minimum((t + 1) * ATT_STEP_BLOCKS, n_all - 1), 0)),
            pl.BlockSpec((None, n_ctx_rows, kvw), lambda b, t: (b, 0, 0)),
        ],
        out_specs=pl.BlockSpec((None, step, qw), lambda b, t: (b, t, 0)),
        compiler_params=_cparams("parallel", "parallel"),
        name="swa_core",
    )(sink, q, kv, kv, kv, kv)


def _segment_permutation(n):
    r = jnp.arange(n)
    p = (r[None, :] == ((r % 8) * (n // 8) + r // 8)[:, None]).astype(BF16)
    return p, p.T


def _rnn_pre_part(x, mod_ref, g_ref, w_ref, u_ref, z_ref):
    h = _prenorm(x, g_ref[...], mod_ref[...]).astype(BF16)
    rw = u_ref.shape[-1]
    u_ref[...] = _dot(h, w_ref[:, :rw]).astype(BF16)
    z_ref[...] = _dot(h, w_ref[:, rw:]).astype(BF16)


def _shift_rows(cur, other, up):
    sub = lax.broadcasted_iota(jnp.int32, cur.shape, 0)
    if up:
        return jnp.where(sub == 7, pltpu.roll(other, 7, 0), pltpu.roll(cur, 7, 0))
    return jnp.where(sub == 0, pltpu.roll(other, 1, 0), pltpu.roll(cur, 1, 0))


def _segment_scan(a, x, carry, reverse):
    n_g = a.shape[0] // 8
    order = range(n_g - 1, -1, -1) if reverse else range(n_g)
    hs, ps = [None] * n_g, [None] * n_g
    h = p = None
    for g in order:
        ag, xg = a[8 * g:8 * g + 8], x[8 * g:8 * g + 8]
        h = xg if h is None else ag * h + xg
        p = ag if p is None else ag * p
        hs[g], ps[g] = h, p
    sub = lax.broadcasted_iota(jnp.int32, h.shape, 0)
    for s in (1, 2, 4):
        shift = 8 - s if reverse else s
        inside = (sub < 8 - s) if reverse else (sub >= s)
        h = h + p * jnp.where(inside, pltpu.roll(h, shift, 0), 0.0)
        p = p * jnp.where(inside, pltpu.roll(p, shift, 0), 1.0)
    full = h + p * carry
    first, last = (7, 0) if reverse else (0, 7)
    seg_in = jnp.where(sub == first, carry, pltpu.roll(full, 7 if reverse else 1, 0))
    out = jnp.concatenate([hs[g] + ps[g] * seg_in for g in range(n_g)], axis=0)
    return out, jnp.broadcast_to(full[last:last + 1], full.shape)


def _rnn_core_kernel(upf_ref, ucf_ref, unf_ref, upb_ref, ucb_ref, unb_ref,
                     cw_ref, cb_ref, wg_ref, ba_ref, bx_ref, lam_ref,
                     perm_ref, unperm_ref, hf_ref, hb_ref, carry_ref, *, n_ctx, n_all):
    n = pl.program_id(1)

    @pl.when(n == 0)
    def _():
        carry_ref[...] = jnp.zeros_like(carry_ref)

    tiles = (n, _bwd_order(n, n_ctx, n_all))
    prev_ok = [jnp.where(jnp.logical_and(t != 0, t != n_ctx), 1.0, 0.0) for t in tiles]
    next_ok = [jnp.where(jnp.logical_and(t != n_ctx - 1, t != n_all - 1), 1.0, 0.0) for t in tiles]
    refs = ((upf_ref, ucf_ref, unf_ref, hf_ref), (upb_ref, ucb_ref, unb_ref, hb_ref))

    useg = [_dot(perm_ref[...], refs[d][1][...]) for d in range(2)]
    rows = useg[0].shape[0]
    convs, pres = {}, {}
    for d, (p_ref, _, n_ref, _) in enumerate(refs):
        before = p_ref[...].astype(F32)[HALO - 8:] * prev_ok[d]
        after = n_ref[...].astype(F32)[:8] * next_ok[d]
        before2 = pltpu.roll(before, 1, 0)
        for j in range(RNN_HEADS):
            lanes = slice(j * RNN_HD, (j + 1) * RNN_HD)
            u = useg[d][:, lanes]
            ext = jnp.concatenate([
                _shift_rows(u[rows - 16:rows - 8], before2[:, lanes], up=False),
                _shift_rows(u[rows - 8:], before[:, lanes], up=False),
                u,
                _shift_rows(u[:8], after[:, lanes], up=True)], axis=0)
            conv = cb_ref[:, lanes]
            for tap in range(CONV_W):
                conv = conv + ext[8 * tap:8 * tap + rows] * cw_ref[tap:tap + 1, lanes]
            convs[d, j] = conv
            pres[d, j] = _dot(conv.astype(BF16), wg_ref[d, j])

    half = RNN_HEADS // 2
    for j0 in (0, half):
        for d in range(2):
            outs = []
            for j in range(j0, j0 + half):
                pre, conv = pres[d, j], convs[d, j]
                u = jnp.tanh(pre[:, :RNN_HD] + ba_ref[d, j]) + 1.0
                i = jnp.tanh(pre[:, RNN_HD:] + bx_ref[d, j]) + 1.0
                rate = (0.5 * LRU_C) * _softplus(-lam_ref[d, j])
                a = jnp.exp2((-LOG2_E * rate) * u)
                y = 1.0 - a * a
                x = jnp.where(y > 0.0, y * lax.rsqrt(y), 0.0) * i * conv
                hs, carry = _segment_scan(a, x, carry_ref[d, j], reverse=(d == 1))
                carry_ref[d, j] = carry
                outs.append(hs.astype(BF16))
            h = _dot(unperm_ref[...], jnp.concatenate(outs, axis=1))
            refs[d][3][:, j0 * RNN_HD:(j0 + half) * RNN_HD] = h.astype(BF16)


def _rnn_core(u, conv_w, conv_b, wg, b_a, b_x, lam, n_ctx_rows):
    bsz, tt, rw = u.shape
    tile = RNN_TILE
    n_all, n_ctx = tt // tile, n_ctx_rows // tile
    per = tile // HALO
    n_halo = tt // HALO
    fwd = lambda n: n
    bwd = lambda n: _bwd_order(n, n_ctx, n_all)

    def specs(order):
        return [
            pl.BlockSpec((None, HALO, rw), lambda b, n: (b, jnp.maximum(order(n) * per - 1, 0), 0)),
            pl.BlockSpec((None, tile, rw), lambda b, n: (b, order(n), 0)),
            pl.BlockSpec((None, HALO, rw), lambda b, n: (b, jnp.minimum((order(n) + 1) * per, n_halo - 1), 0)),
        ]

    out_spec = lambda order: pl.BlockSpec((None, tile, rw), lambda b, n: (b, order(n), 0))
    perm, unperm = _segment_permutation(tile)
    consts = (conv_w, conv_b, wg, b_a, b_x, lam, perm, unperm)
    return pl.pallas_call(
        functools.partial(_rnn_core_kernel, n_ctx=n_ctx, n_all=n_all),
        out_shape=[jax.ShapeDtypeStruct((bsz, tt, rw), BF16)] * 2,
        grid=(bsz, n_all),
        in_specs=specs(fwd) + specs(bwd) + [_full_spec(c.shape) for c in consts],
        out_specs=[out_spec(fwd), out_spec(bwd)],
        scratch_shapes=[pltpu.VMEM((2, RNN_HEADS, 8, RNN_HD), F32)],
        compiler_params=_cparams("arbitrary", "arbitrary"),
        name="rglru_core",
    )(u, u, u, u, u, u, *consts)


def _residual(u, x, mod_ref, gp_ref, w_ref):
    y = _dot(u.astype(BF16), w_ref[...])
    return x + mod_ref[2:3] * (_rms(y) * gp_ref[...])


def _gla_post_part(x, mod_ref, gp_ref, w_ref, of_ref, ob_ref, z_ref, gh_ref, rows=slice(None)):
    o = of_ref[rows].astype(F32) + ob_ref[rows].astype(F32)
    hv = o.shape[-1] // GLA_HEADS
    o = jnp.concatenate([_rms(o[:, h * hv:(h + 1) * hv]) for h in range(GLA_HEADS)], axis=1)
    u = o * gh_ref[...] * _silu(z_ref[rows].astype(F32))
    return _residual(u, x, mod_ref, gp_ref, w_ref)


def _swa_post_part(x, mod_ref, gp_ref, w_ref, a_ref, z_ref, rows=slice(None)):
    u = a_ref[rows].astype(F32) * _silu(z_ref[rows].astype(F32))
    return _residual(u, x, mod_ref, gp_ref, w_ref)


def _rnn_post_part(x, mod_ref, gp_ref, w_ref, hf_ref, hb_ref, z_ref, rows=slice(None)):
    u = (hf_ref[rows].astype(F32) + hb_ref[rows].astype(F32)) * _silu(z_ref[rows].astype(F32))
    return _residual(u, x, mod_ref, gp_ref, w_ref)


class _Pre(NamedTuple):
    name: str
    part: Callable
    mod: jax.Array
    g_pre: jax.Array
    consts: tuple
    row_consts: tuple
    widths: tuple
    dtypes: tuple
    stats: int


class _Post(NamedTuple):
    name: str
    part: Callable
    mod: jax.Array
    g_post: jax.Array
    w_out: jax.Array
    acts: tuple
    vecs: tuple


def _pre_kernel(*refs, part, n_x, n_ctx):
    part(_token_tile(refs[:n_x], n_ctx), *refs[n_x:])


def _post_halves(part, x, refs):
    half = x.shape[0] // 2
    return jnp.concatenate([part(x[rows], *refs, rows=rows)
                            for rows in (slice(0, half), slice(half, 2 * half))], axis=0)


def _post_kernel(*refs, part, n_x, n_ctx):
    refs[-1][...] = _post_halves(part, _token_tile(refs[:n_x], n_ctx), refs[n_x:-1])


def _mid_kernel(*refs, post_part, n_post, pre_part, n_pre, n_x, n_ctx):
    post_in, pre_in = refs[n_x:n_x + n_post], refs[n_x + n_post:n_x + n_post + n_pre]
    xo_ref, pre_out = refs[n_x + n_post + n_pre], refs[n_x + n_post + n_pre + 1:]
    x = _post_halves(post_part, _token_tile(refs[:n_x], n_ctx), post_in)
    xo_ref[...] = x
    pre_part(x, *pre_in, *pre_out)


def _stream_shape(xs):
    if isinstance(xs, tuple):
        bsz, n_ctx_rows, d = xs[0].shape
        return bsz, n_ctx_rows + xs[1].shape[1], d
    return xs.shape


def _pre_io(pre, bsz, tt, d, n_ctx):
    operands = (pre.mod, pre.g_pre, *pre.consts, *pre.row_consts)
    in_specs = ([_mod_spec(d, n_ctx, bsz), _full_spec(pre.g_pre.shape)]
                + [_full_spec(c.shape) for c in pre.consts]
                + [pl.BlockSpec((ROW_TILE, r.shape[-1]), lambda b, t: (t, 0)) for r in pre.row_consts])
    out_shape = ([jax.ShapeDtypeStruct((bsz, tt, w), dt) for w, dt in zip(pre.widths, pre.dtypes)]
                 + [jax.ShapeDtypeStruct((bsz, tt // ROW_TILE, 8, 128), F32)] * pre.stats)
    out_specs = ([_row_spec(w) for w in pre.widths]
                 + [pl.BlockSpec((None, None, 8, 128), lambda b, t: (b, t, 0, 0))] * pre.stats)
    return operands, in_specs, out_shape, out_specs


def _post_io(post, bsz, d, n_ctx, off):
    operands = (post.mod, post.g_post, post.w_out, *post.acts, *post.vecs)
    in_specs = ([_mod_spec(d, n_ctx, bsz, off), _full_spec(post.g_post.shape), _full_spec(post.w_out.shape)]
                + [_row_spec(a.shape[-1], off) for a in post.acts]
                + [_full_spec(v.shape) for v in post.vecs])
    return operands, in_specs


def _pre_call(pre, xs, n_ctx_rows):
    bsz, tt, d = _stream_shape(xs)
    n_ctx = n_ctx_rows // ROW_TILE
    streams = xs if isinstance(xs, tuple) else (xs,)
    operands, in_specs, out_shape, out_specs = _pre_io(pre, bsz, tt, d, n_ctx)
    return pl.pallas_call(
        functools.partial(_pre_kernel, part=pre.part, n_x=len(streams), n_ctx=n_ctx),
        out_shape=out_shape,
        grid=(bsz, tt // ROW_TILE),
        in_specs=_token_specs(xs, n_ctx) + in_specs,
        out_specs=out_specs,
        compiler_params=_cparams("parallel", "parallel"),
        name=pre.name,
    )(*streams, *operands)


def _post_call(post, xs, n_ctx_rows, latent_only):
    bsz, tt, d = _stream_shape(xs)
    n_ctx = n_ctx_rows // ROW_TILE
    off = n_ctx if latent_only else 0
    n_tiles = tt // ROW_TILE - off
    streams = xs if isinstance(xs, tuple) else (xs,)
    operands, in_specs = _post_io(post, bsz, d, n_ctx, off)
    return pl.pallas_call(
        functools.partial(_post_kernel, part=post.part, n_x=len(streams), n_ctx=n_ctx - off),
        out_shape=jax.ShapeDtypeStruct((bsz, n_tiles * ROW_TILE, d), F32),
        grid=(bsz, n_tiles),
        in_specs=_token_specs(xs, n_ctx, off) + in_specs,
        out_specs=_row_spec(d),
        compiler_params=_cparams("parallel", "parallel"),
        name=post.name,
    )(*streams, *operands)


def _final_call(post, xs, n_ctx_rows):
    bsz, tt, d = xs.shape
    rows = lambda w: pl.BlockSpec((pl.Element(1), pl.Element(FINAL_TILE), pl.Element(w)),
                                  lambda b, t: (b, pl.multiple_of(n_ctx_rows + t * FINAL_TILE, 8), 0))
    operands = (post.mod, post.g_post, post.w_out, *post.acts, *post.vecs)
    in_specs = ([rows(d), pl.BlockSpec((None, 3, d), lambda b, t: (b, 0, 0)),
                 _full_spec(post.g_post.shape), _full_spec(post.w_out.shape)]
                + [rows(a.shape[-1]) for a in post.acts]
                + [_full_spec(v.shape) for v in post.vecs])

    def body(x_ref, mod_ref, gp_ref, w_ref, *rest):
        acts = [r.at[0] for r in rest[:len(post.acts)]]
        vecs, o_ref = rest[len(post.acts):-1], rest[-1]
        o_ref[...] = _post_halves(post.part, x_ref[0], (mod_ref, gp_ref, w_ref, *acts, *vecs))

    return pl.pallas_call(
        body,
        out_shape=jax.ShapeDtypeStruct((bsz, tt - n_ctx_rows, d), F32),
        grid=(bsz, (tt - n_ctx_rows) // FINAL_TILE),
        in_specs=in_specs,
        out_specs=pl.BlockSpec((None, FINAL_TILE, d), lambda b, t: (b, t, 0)),
        compiler_params=_cparams("parallel", "parallel"),
        name=post.name,
    )(xs, *operands)


def _mid_call(post, pre, xs, n_ctx_rows):
    bsz, tt, d = _stream_shape(xs)
    n_ctx = n_ctx_rows // ROW_TILE
    streams = xs if isinstance(xs, tuple) else (xs,)
    post_ops, post_specs = _post_io(post, bsz, d, n_ctx, 0)
    pre_ops, pre_specs, pre_shape, pre_out_specs = _pre_io(pre, bsz, tt, d, n_ctx)
    outs = pl.pallas_call(
        functools.partial(_mid_kernel, post_part=post.part, n_post=len(post_ops),
                          pre_part=pre.part, n_pre=len(pre_ops), n_x=len(streams), n_ctx=n_ctx),
        out_shape=[jax.ShapeDtypeStruct((bsz, tt, d), F32)] + pre_shape,
        grid=(bsz, tt // ROW_TILE),
        in_specs=_token_specs(xs, n_ctx) + post_specs + pre_specs,
        out_specs=[_row_spec(d)] + pre_out_specs,
        compiler_params=_cparams("parallel", "parallel"),
        name=post.name + "_" + pre.name,
    )(*streams, *post_ops, *pre_ops)
    return outs[0], outs[1:]


def _gla_layer(mod, g_pre, g_post, w_in, w_g1, w_g2, b_g, g_head, w_out, n_ctx_rows):
    d = w_in.shape[0]
    rank, dk = w_g2.shape[1], w_g2.shape[2]
    dv = w_out.shape[0]
    wg1 = jnp.zeros((d, GLA_GATE_PAD), F32).at[:, :2 * rank].set(jnp.concatenate([w_g1[0], w_g1[1]], axis=1))
    wg2 = jnp.zeros((GLA_GATE_PAD, 2 * dk), F32)
    wg2 = wg2.at[:rank, :dk].set(w_g2[0]).at[rank:2 * rank, dk:].set(w_g2[1])
    pre = _Pre("gla_pre", functools.partial(_gla_pre_part, dk=dk), mod, g_pre,
               (w_in.astype(BF16), wg1.astype(BF16), wg2.astype(BF16), b_g.reshape(1, 2 * dk)), (),
               (2 * dk, dv, dv, 2 * dk), (BF16, BF16, BF16, BF16), 1)

    def mixer(outs):
        qk, v, z, la, peak = outs
        exact = (peak[:, :, 0, 0] * (GLA_BLOCK // 2) > GLA_SAFE_EXPONENT).astype(jnp.int32)
        o_f, o_b = _gla_core(qk, v, la, exact, n_ctx_rows)
        return _Post("gla_post", _gla_post_part, mod, g_post, w_out.astype(BF16), (o_f, o_b, z),
                     (g_head.reshape(1, dv),))

    return pre, mixer


def _swa_layer(mod, g_pre, g_post, w_in, sink, w_out, rope, n_ctx_rows):
    qw = w_out.shape[0]
    kw = (w_in.shape[1] - 2 * qw) // 2
    pre = _Pre("swa_pre", _swa_pre_part, mod, g_pre, (w_in.astype(BF16),), rope,
               (qw, 2 * kw, qw), (BF16, BF16, BF16), 0)

    def mixer(outs):
        q, kv, z = outs
        a = _swa_core(q, kv, sink, n_ctx_rows)
        return _Post("swa_post", _swa_post_part, mod, g_post, w_out.astype(BF16), (a, z), ())

    return pre, mixer


def _rnn_layer(mod, g_pre, g_post, w_in, conv_w, conv_b, w_ra, b_ra, w_ri, b_ri, lam, w_out, n_ctx_rows):
    rw = w_out.shape[0]
    pre = _Pre("rglru_pre", _rnn_pre_part, mod, g_pre, (w_in.astype(BF16),), (), (rw, rw), (BF16, BF16), 0)
    wg = jnp.concatenate([w_ra, w_ri], axis=-1).astype(BF16)
    vec = lambda a: a.reshape(2, RNN_HEADS, 1, RNN_HD)

    def mixer(outs):
        u, z = outs
        h_f, h_b = _rnn_core(u, 0.5 * conv_w, 0.5 * conv_b.reshape(1, rw), wg, vec(0.5 * b_ra), vec(0.5 * b_ri),
                             vec(lam), n_ctx_rows)
        return _Post("rglru_post", _rnn_post_part, mod, g_post, w_out.astype(BF16), (h_f, h_b, z), ())

    return pre, mixer


def _rope_tables(t, n_ctx_rows):
    pairs = ATT_HEAD_DIM // 4
    pos = jnp.arange(t, dtype=jnp.int32)
    row = (pos // GRID_W).astype(F32)
    col = (pos % GRID_W).astype(F32)
    freqs = ROPE_BASE ** (-jnp.arange(pairs, dtype=F32) / pairs)
    ang_r, ang_c = row[:, None] * freqs, col[:, None] * freqs
    cos = jnp.concatenate([jnp.cos(ang_r)] * 2 + [jnp.cos(ang_c)] * 2, axis=1)
    sin = jnp.concatenate([-jnp.sin(ang_r), jnp.sin(ang_r), -jnp.sin(ang_c), jnp.sin(ang_c)], axis=1)
    cos = jnp.concatenate([jnp.ones((n_ctx_rows, ATT_HEAD_DIM), F32), cos], axis=0)
    sin = jnp.concatenate([jnp.zeros((n_ctx_rows, ATT_HEAD_DIM), F32), sin], axis=0)
    return jnp.tile(cos, (1, 2)), jnp.tile(sin, (1, 2))


def kernel(x, c, ctx, c_ctx, w_mod, b_mod, g_pre, g_post, a_w_in, a_w_g1, a_w_g2, a_b_g, a_g_head, a_w_out, b_w_in, b_sink, b_w_out, c_w_in, c_conv_w, c_conv_b, c_w_ra, c_b_ra, c_w_ri, c_b_ri, c_lam, c_w_out):
    bsz, t, d = x.shape
    n_ctx_rows = ctx.shape[1]
    depth = w_mod.shape[0]
    assert bsz < MOD_ROWS and n_ctx_rows % ROW_TILE == 0 and t % ROW_TILE == 0

    cvec = jnp.zeros((MOD_ROWS, d), F32).at[:bsz].set(c).at[bsz].set(c_ctx)
    mods = _modulation(cvec, w_mod, b_mod)
    rope = _rope_tables(t, n_ctx_rows)
    layers = []
    for i in range(depth):
        kind, j = i % N_MIXERS, i // N_MIXERS
        gpre, gpost = g_pre[i].reshape(1, d), g_post[i].reshape(1, d)
        if kind == 0:
            layers.append(_gla_layer(mods[i], gpre, gpost, a_w_in[j], a_w_g1[j], a_w_g2[j], a_b_g[j],
                                     a_g_head[j], a_w_out[j], n_ctx_rows))
        elif kind == 1:
            layers.append(_swa_layer(mods[i], gpre, gpost, b_w_in[j], b_sink[j], b_w_out[j], rope, n_ctx_rows))
        else:
            layers.append(_rnn_layer(mods[i], gpre, gpost, c_w_in[j], c_conv_w[j], c_conv_b[j], c_w_ra[j],
                                     c_b_ra[j], c_w_ri[j], c_b_ri[j], c_lam[j], c_w_out[j], n_ctx_rows))

    xs = (ctx, x)
    outs = _pre_call(layers[0][0], xs, n_ctx_rows)
    for i in range(depth - 1):
        xs, outs = _mid_call(layers[i][1](outs), layers[i + 1][0], xs, n_ctx_rows)
    if isinstance(xs, tuple) or t % FINAL_TILE:
        return _post_call(layers[-1][1](outs), xs, n_ctx_rows, latent_only=True)
    return _final_call(layers[-1][1](outs), xs, n_ctx_rows)
```

```python
import functools
from typing import Callable, NamedTuple

import jax
import jax.numpy as jnp
from jax import lax
from jax.experimental import pallas as pl
from jax.experimental.pallas import tpu as pltpu

F32 = jnp.float32
BF16 = jnp.bfloat16

EPS = 1e-6
GRID_W = 64
N_MIXERS = 3

GLA_HEADS = 4
GLA_TAU = 16.0
GLA_GATE_PAD = 128

ATT_HEAD_DIM = 64
ATT_KV_HEADS = 4
ATT_GROUP = 4
ATT_BLOCK = 128
ATT_ROWS = 32
ATT_STEP_BLOCKS = 2
ROPE_BASE = 10000.0
LOG2_E = 1.4426950408889634

RNN_HEADS = 10
RNN_HD = 128
CONV_W = 4
LRU_C = 8.0

ROW_TILE = 256
FINAL_TILE = 1024
GLA_TILE = 256
GLA_CHUNK = 128
GLA_BLOCK = 64
GLA_SAFE_EXPONENT = 60.0
RNN_TILE = 256
HALO = 16
MOD_ROWS = 8
VMEM_LIMIT = 48 * 1024 * 1024


def _cparams(*sem):
    return pltpu.CompilerParams(dimension_semantics=sem, vmem_limit_bytes=VMEM_LIMIT)


def _sigmoid(x):
    return 0.5 * jnp.tanh(0.5 * x) + 0.5


def _silu(x):
    return x * _sigmoid(x)


def _softplus(x):
    return jnp.maximum(x, 0.0) + jnp.log1p(jnp.exp(-jnp.abs(x)))


def _log_sigmoid(x):
    return jnp.minimum(x, 0.0) - jnp.log(1.0 + jnp.exp(-jnp.abs(x)))


def _split3(a):
    hi = a.astype(BF16)
    r = a - hi.astype(F32)
    mid = r.astype(BF16)
    lo = (r - mid.astype(F32)).astype(BF16)
    return hi, mid, lo


def _dot(a, b):
    return jnp.dot(a, b, preferred_element_type=F32)


def _dot_nt(a, b):
    return lax.dot_general(a, b, (((1,), (1,)), ((), ())), preferred_element_type=F32)


def _dot_tn(a, b):
    return lax.dot_general(a, b, (((0,), (0,)), ((), ())), preferred_element_type=F32)


def _rms(x):
    return x * lax.rsqrt(jnp.mean(x * x, axis=-1, keepdims=True) + EPS)


def _mod_kernel(c_ref, w_ref, b_ref, o_ref):
    s = _silu(c_ref[...])
    s_hi, s_mid, s_lo = _split3(s)
    w_hi, w_mid, w_lo = _split3(w_ref[...])
    acc = _dot(s_hi, w_hi)
    acc += _dot(s_hi, w_mid) + _dot(s_mid, w_hi)
    acc += _dot(s_hi, w_lo) + _dot(s_mid, w_mid) + _dot(s_lo, w_hi)
    o_ref[...] = acc + b_ref[...]


def _modulation(cvec, w_mod, b_mod):
    depth, d, d3 = w_mod.shape
    out = pl.pallas_call(
        _mod_kernel,
        out_shape=jax.ShapeDtypeStruct((depth, MOD_ROWS, d3), F32),
        grid=(depth, d3 // d),
        in_specs=[
            pl.BlockSpec((MOD_ROWS, d), lambda i, j: (0, 0)),
            pl.BlockSpec((None, d, d), lambda i, j: (i, 0, j)),
            pl.BlockSpec((None, 1, d), lambda i, j: (i, 0, j)),
        ],
        out_specs=pl.BlockSpec((None, MOD_ROWS, d), lambda i, j: (i, 0, j)),
        compiler_params=_cparams("parallel", "parallel"),
        name="modulation",
    )(cvec, w_mod, b_mod.reshape(depth, 1, d3))
    return out.reshape(depth, MOD_ROWS, 3, d)


def _prenorm(x, g, mod):
    return _rms(x) * g * (1.0 + mod[1:2]) + mod[0:1]


def _mod_spec(d, n_ctx_tiles, ctx_row, offset=0):
    def index(b, t):
        return (jnp.where(t + offset < n_ctx_tiles, ctx_row, b), 0, 0)
    return pl.BlockSpec((None, 3, d), index)


def _row_spec(width, offset=0):
    return pl.BlockSpec((None, ROW_TILE, width), lambda b, t: (b, t + offset, 0))


def _token_specs(xs, n_ctx, offset=0):
    if not isinstance(xs, tuple):
        return [_row_spec(xs.shape[-1], offset)]
    d = xs[0].shape[-1]
    return [pl.BlockSpec((None, ROW_TILE, d), lambda b, t: (b, jnp.minimum(t + offset, n_ctx - 1), 0)),
            pl.BlockSpec((None, ROW_TILE, d), lambda b, t: (b, jnp.maximum(t + offset - n_ctx, 0), 0))]


def _token_tile(x_refs, n_ctx_tiles):
    if len(x_refs) == 1:
        return x_refs[0][...]
    return jnp.where(pl.program_id(1) < n_ctx_tiles, x_refs[0][...], x_refs[1][...])


def _full_spec(shape):
    zeros = (0,) * len(shape)
    return pl.BlockSpec(shape, lambda b, t: zeros)


def _gla_pre_part(x, mod_ref, g_ref, w_ref, wg1_ref, wg2_ref, bg_ref, qk_ref, v_ref, z_ref, la_ref, peak_ref,
                  *, dk):
    h = _prenorm(x, g_ref[...], mod_ref[...]).astype(BF16)
    hk = dk // GLA_HEADS
    dv = v_ref.shape[-1]
    g1 = _dot(h, wg1_ref[...]).astype(BF16)
    gate = _dot(g1, wg2_ref[...]) + bg_ref[...]
    la = _log_sigmoid(gate) / GLA_TAU
    la_ref[...] = la.astype(BF16)
    peak = jnp.max(jnp.max(-la, axis=1, keepdims=True), axis=0, keepdims=True)
    peak_ref[...] = jnp.broadcast_to(peak, peak_ref.shape)
    q = _dot(h, w_ref[:, :dk]) * hk ** -0.5
    qk_ref[:, :dk] = q.astype(BF16)
    qk_ref[:, dk:] = _dot(h, w_ref[:, dk:2 * dk]).astype(BF16)
    v_ref[...] = _dot(h, w_ref[:, 2 * dk:2 * dk + dv]).astype(BF16)
    z_ref[...] = _dot(h, w_ref[:, 2 * dk + dv:]).astype(BF16)


def _gla_core_kernel(exact_ref, qkf_ref, vf_ref, laf_ref, qkb_ref, vb_ref, lab_ref,
                     of_ref, ob_ref, st_ref, row_ref, *, dk, n_ctx, n_all):
    b, n = pl.program_id(0), pl.program_id(1)

    @pl.when(n == 0)
    def _():
        st_ref[...] = jnp.zeros_like(st_ref)

    io = ((qkf_ref, vf_ref, laf_ref, of_ref), (qkb_ref, vb_ref, lab_ref, ob_ref))
    n_sub = qkf_ref.shape[0] // GLA_CHUNK

    def run(exact):
        chunks = [_gla_chunk_pair(io, (s * GLA_CHUNK, (n_sub - 1 - s) * GLA_CHUNK), st_ref, dk,
                                  row_ref if exact else None) for s in range(n_sub)]
        for _ in range(3):
            for chunk in chunks:
                next(chunk, None)

    out_of_range = exact_ref[b, n] + exact_ref[b, _bwd_order(n, n_ctx, n_all)]
    pl.when(out_of_range == 0)(lambda: run(False))
    pl.when(out_of_range != 0)(lambda: run(True))


def _gla_exact_intra(d, q, k, v, b, row_ref):
    c, hk = q.shape
    row_ref[:, :hk] = k
    row_ref[:, hk:2 * hk] = b
    row_ref[:, 2 * hk:] = v.astype(F32)
    t_idx = lax.broadcasted_iota(jnp.int32, (c, 1), 0)

    def keys(g, acc):
        group = row_ref[pl.ds(pl.multiple_of(g * 8, 8), 8), :]
        for r in range(8):
            s = g * 8 + r
            k_s, b_s, v_s = group[r:r + 1, :hk], group[r:r + 1, hk:2 * hk], group[r:r + 1, 2 * hk:]
            decay = jnp.exp(jnp.minimum(b - b_s, 0.0))
            col = jnp.sum(q * decay * k_s, axis=1, keepdims=True)
            causal = (t_idx >= s) if d == 0 else (t_idx <= s)
            acc = acc + jnp.where(causal, col, 0.0) * v_s
        return acc

    return lax.fori_loop(0, c // 8, keys, jnp.zeros((c, v.shape[1]), F32))


def _gla_chunk_pair(io, base, st_ref, dk, row_ref):
    c = GLA_CHUNK
    hk = dk // GLA_HEADS
    hv = io[0][1].shape[-1] // GLA_HEADS
    row = lax.broadcasted_iota(jnp.int32, (c, c), 0)
    col = lax.broadcasted_iota(jnp.int32, (c, c), 1)
    blk = GLA_BLOCK
    keeps = (col <= row, col >= row)
    lasts = (c - 1, 0)
    tile = [slice(r0, r0 + c) for r0 in base]
    cums = []
    for d, (_, _, la_ref, _) in enumerate(io):
        tri = jnp.where(keeps[d], 1.0, 0.0).astype(BF16)
        cums.append(_dot(tri, la_ref[tile[d], d * dk:(d + 1) * dk]))
    yield

    stage = {}
    for d, (qk_ref, v_ref, _, _) in enumerate(io):
        for h in range(GLA_HEADS):
            b = cums[d][:, h * hk:(h + 1) * hk]
            b_last = b[lasts[d]:lasts[d] + 1]
            q = qk_ref[tile[d], h * hk:(h + 1) * hk].astype(F32)
            k = qk_ref[tile[d], dk + h * hk:dk + (h + 1) * hk].astype(F32)
            v = v_ref[tile[d], h * hv:(h + 1) * hv]
            scores = []
            for i in (range(c // blk) if row_ref is None else ()):
                rows = slice(i * blk, (i + 1) * blk)
                keys = slice(0, (i + 1) * blk) if d == 0 else slice(i * blk, c)
                b_ref = b[i * blk + blk // 2:i * blk + blk // 2 + 1]
                qe = (q[rows] * jnp.exp(b[rows] - b_ref)).astype(BF16)
                ke = (k[keys] * jnp.exp(b_ref - b[keys])).astype(BF16)
                scores.append((rows, keys, _dot_nt(qe, ke)))
            qs = (q * jnp.exp(b)).astype(BF16)
            kd = (k * jnp.exp(b_last - b)).astype(BF16)
            st = st_ref[d, h]
            inter = _dot_nt(qs, st.astype(BF16))
            st_ref[d, h] = st * jnp.exp(b_last) + _dot_tn(v, kd)
            stage[d, h] = (scores, inter, v, (q, k, b))
    yield

    for d, (_, _, _, o_ref) in enumerate(io):
        for h in range(GLA_HEADS):
            scores, inter, v, (q, k, b) = stage[d, h]
            if row_ref is not None:
                o = _gla_exact_intra(d, q, k, v, b, row_ref) + inter
                o_ref[tile[d], h * hv:(h + 1) * hv] = o.astype(BF16)
            for rows, keys, s in scores:
                t_idx = lax.broadcasted_iota(jnp.int32, s.shape, 0) + rows.start
                s_idx = lax.broadcasted_iota(jnp.int32, s.shape, 1) + keys.start
                p = jnp.where((s_idx <= t_idx) if d == 0 else (s_idx >= t_idx), s, 0.0).astype(BF16)
                o = _dot(p, v[keys]) + inter[rows]
                o_ref[base[d] + rows.start:base[d] + rows.stop, h * hv:(h + 1) * hv] = o.astype(BF16)


def _bwd_order(n, n_ctx, n_all):
    return jnp.where(n < n_ctx, n_ctx - 1 - n, n_all - 1 - (n - n_ctx))


def _gla_core(qk, v, la, exact, n_ctx_rows):
    bsz, tt, dv = v.shape
    dk = qk.shape[-1] // 2
    c = GLA_TILE
    n_all, n_ctx = tt // c, n_ctx_rows // c
    hk, hv = dk // GLA_HEADS, dv // GLA_HEADS
    fwd = lambda b, n: (b, n, 0)
    bwd = lambda b, n: (b, _bwd_order(n, n_ctx, n_all), 0)
    spec = lambda w, idx: pl.BlockSpec((None, c, w), idx)
    return pl.pallas_call(
        functools.partial(_gla_core_kernel, dk=dk, n_ctx=n_ctx, n_all=n_all),
        out_shape=[jax.ShapeDtypeStruct((bsz, tt, dv), BF16)] * 2,
        grid=(bsz, n_all),
        in_specs=[pl.BlockSpec(memory_space=pltpu.SMEM),
                  spec(2 * dk, fwd), spec(dv, fwd), spec(2 * dk, fwd),
                  spec(2 * dk, bwd), spec(dv, bwd), spec(2 * dk, bwd)],
        out_specs=[spec(dv, fwd), spec(dv, bwd)],
        scratch_shapes=[pltpu.VMEM((2, GLA_HEADS, hv, hk), F32),
                        pltpu.VMEM((GLA_CHUNK, 2 * hk + hv), F32)],
        compiler_params=_cparams("arbitrary", "arbitrary"),
        name="gla_core",
    )(exact, qk, v, la, qk, v, la)


def _rope(x, cos, sin_signed):
    lane = lax.broadcasted_iota(jnp.int32, (x.shape[0], 128), 1)
    first = (lane % 32) < 16
    out = []
    for j in range(x.shape[1] // 128):
        xj = x[:, j * 128:(j + 1) * 128]
        partner = jnp.where(first, pltpu.roll(xj, 112, 1), pltpu.roll(xj, 16, 1))
        out.append(xj * cos + partner * sin_signed)
    return jnp.concatenate(out, axis=1)


def _swa_pre_part(x, mod_ref, g_ref, w_ref, cos_ref, sin_ref, q_ref, kv_ref, z_ref):
    h = _prenorm(x, g_ref[...], mod_ref[...]).astype(BF16)
    qw = q_ref.shape[-1]
    kw = kv_ref.shape[-1] // 2
    cos, sin = cos_ref[...], sin_ref[...]
    q = _rope(_dot(h, w_ref[:, :qw]), cos, sin) * (ATT_HEAD_DIM ** -0.5 * LOG2_E)
    q_ref[...] = q.astype(BF16)
    kv_ref[:, :kw] = _rope(_dot(h, w_ref[:, qw:qw + kw]), cos, sin).astype(BF16)
    kv_ref[:, kw:] = _dot(h, w_ref[:, qw + kw:qw + 2 * kw]).astype(BF16)
    z_ref[...] = _dot(h, w_ref[:, qw + 2 * kw:]).astype(BF16)


def _swa_core_kernel(sink_ref, q_ref, kvp_ref, kvc_ref, kvn_ref, kvx_ref, o_ref, *, n_ctx, n_all):
    blk = ATT_BLOCK
    own = [kvc_ref.at[i * blk:(i + 1) * blk] for i in range(ATT_STEP_BLOCKS)]
    window = [kvp_ref] + own + [kvn_ref]
    blocks = [_swa_block(sink_ref, q_ref.at[i * blk:(i + 1) * blk], window[i:i + 3], kvx_ref,
                         o_ref.at[i * blk:(i + 1) * blk], pl.program_id(1) * ATT_STEP_BLOCKS + i, n_ctx, n_all)
              for i in range(ATT_STEP_BLOCKS)]
    for _ in range(3):
        for block in blocks:
            next(block, None)


def _swa_block(sink_ref, q_ref, kv_refs, kvx_ref, o_ref, t, n_ctx, n_all):
    blk = ATT_BLOCK
    kvp_ref, kvc_ref, kvn_ref = kv_refs
    n_keys = 3 * blk + kvx_ref.shape[0]
    kw = kvc_ref.shape[-1] // 2
    n_loc = 3 * blk
    i = lax.broadcasted_iota(jnp.int32, (blk, n_loc), 0)
    j = lax.broadcasted_iota(jnp.int32, (blk, n_loc), 1)
    never = n_keys
    off_prev = jnp.where(t > n_ctx, 0, never)
    off_cur = jnp.where(t >= n_ctx, 0, never)
    off_next = jnp.where(jnp.logical_and(t >= n_ctx, t < n_all - 1), 0, never)
    valid = (((j >= i + off_prev) & (j < blk))
             | ((j >= blk + off_cur) & (j < 2 * blk))
             | ((j >= 2 * blk) & (j <= i + 2 * blk - off_next)))
    bias = jnp.where(valid, 0.0, -jnp.inf)
    logits = []
    for h in range(ATT_KV_HEADS):
        ks = slice(h * ATT_HEAD_DIM, (h + 1) * ATT_HEAD_DIM)
        k = jnp.concatenate([kvp_ref[:, ks], kvc_ref[:, ks], kvn_ref[:, ks], kvx_ref[:, ks]], axis=0)
        q = jnp.concatenate(
            [q_ref[:, (h * ATT_GROUP + g) * ATT_HEAD_DIM:(h * ATT_GROUP + g + 1) * ATT_HEAD_DIM]
             for g in range(ATT_GROUP)], axis=0)
        logits.append(_dot_nt(q, k))
    yield

    chunks = [(h, r0) for h in range(ATT_KV_HEADS) for r0 in range(0, ATT_GROUP * blk, ATT_ROWS)]

    def slabs(h, r0):
        out = []
        for c0 in range(0, n_keys, 128):
            s = logits[h][r0:r0 + ATT_ROWS, c0:c0 + 128]
            if c0 < n_loc:
                s = s + bias[r0 % blk:r0 % blk + ATT_ROWS, c0:c0 + 128]
            out.append(s)
        return out

    sinks = {(h, r0): sink_ref[h * ATT_GROUP + r0 // blk] * LOG2_E for h, r0 in chunks}
    maxes = {}
    for h, r0 in chunks:
        lane_max = functools.reduce(jnp.maximum, slabs(h, r0))
        maxes[h, r0] = jnp.maximum(jnp.max(lane_max, axis=-1, keepdims=True), sinks[h, r0])
    yield

    head_lane = lax.broadcasted_iota(jnp.int32, (n_keys, 2 * ATT_HEAD_DIM), 1) // ATT_HEAD_DIM
    for h in range(ATT_KV_HEADS):
        vs = slice(kw + (h // 2) * 2 * ATT_HEAD_DIM, kw + (h // 2 + 1) * 2 * ATT_HEAD_DIM)
        v = jnp.concatenate([kvp_ref[:, vs], kvc_ref[:, vs], kvn_ref[:, vs], kvx_ref[:, vs]], axis=0)
        v = jnp.where(head_lane == h % 2, v, jnp.ones_like(v))
        probs, sink_terms = [], []
        for _, r0 in chunks[:len(chunks) // ATT_KV_HEADS]:
            p = [jnp.exp2(s - maxes[h, r0]) for s in slabs(h, r0)]
            probs.append(jnp.concatenate(p, axis=1).astype(BF16))
            sink_terms.append(jnp.exp2(sinks[h, r0] - maxes[h, r0]))
        pv = _dot(jnp.concatenate(probs, axis=0), v)
        own, other = (h % 2) * ATT_HEAD_DIM, (1 - h % 2) * ATT_HEAD_DIM
        denom = pv[:, other:other + 1] + jnp.concatenate(sink_terms, axis=0)
        o = pv[:, own:own + ATT_HEAD_DIM] * (1.0 / denom)
        for g in range(ATT_GROUP):
            lo = (h * ATT_GROUP + g) * ATT_HEAD_DIM
            o_ref[:, lo:lo + ATT_HEAD_DIM] = o[g * blk:(g + 1) * blk].astype(BF16)


def _swa_core(q, kv, sink, n_ctx_rows):
    bsz, tt, qw = q.shape
    blk, step = ATT_BLOCK, ATT_BLOCK * ATT_STEP_BLOCKS
    n_all, n_ctx = tt // blk, n_ctx_rows // blk
    kvw = kv.shape[-1]
    kv_spec = lambda idx: pl.BlockSpec((None, blk, kvw), idx)
    return pl.pallas_call(
        functools.partial(_swa_core_kernel, n_ctx=n_ctx, n_all=n_all),
        out_shape=jax.ShapeDtypeStruct((bsz, tt, qw), BF16),
        grid=(bsz, tt // step),
        in_specs=[
            pl.BlockSpec(memory_space=pltpu.SMEM),
            pl.BlockSpec((None, step, qw), lambda b, t: (b, t, 0)),
            kv_spec(lambda b, t: (b, jnp.maximum(t * ATT_STEP_BLOCKS - 1, 0), 0)),
            pl.BlockSpec((None, step, kvw), lambda b, t: (b, t, 0)),
            kv_spec(lambda b, t: (b, jnp.minimum((t + 1) * ATT_STEP_BLOCKS, n_all - 1), 0)),
            pl.BlockSpec((None, n_ctx_rows, kvw), lambda b, t: (b, 0, 0)),
        ],
        out_specs=pl.BlockSpec((None, step, qw), lambda b, t: (b, t, 0)),
        compiler_params=_cparams("parallel", "parallel"),
        name="swa_core",
    )(sink, q, kv, kv, kv, kv)


def _segment_permutation(n):
    r = jnp.arange(n)
    p = (r[None, :] == ((r % 8) * (n // 8) + r // 8)[:, None]).astype(BF16)
    return p, p.T


def _rnn_pre_part(x, mod_ref, g_ref, w_ref, u_ref, z_ref):
    h = _prenorm(x, g_ref[...], mod_ref[...]).astype(BF16)
    rw = u_ref.shape[-1]
    u_ref[...] = _dot(h, w_ref[:, :rw]).astype(BF16)
    z_ref[...] = _dot(h, w_ref[:, rw:]).astype(BF16)


def _shift_rows(cur, other, up):
    sub = lax.broadcasted_iota(jnp.int32, cur.shape, 0)
    if up:
        return jnp.where(sub == 7, pltpu.roll(other, 7, 0), pltpu.roll(cur, 7, 0))
    return jnp.where(sub == 0, pltpu.roll(other, 1, 0), pltpu.roll(cur, 1, 0))


def _segment_scan(a, x, carry, reverse):
    n_g = a.shape[0] // 8
    order = range(n_g - 1, -1, -1) if reverse else range(n_g)
    hs, ps = [None] * n_g, [None] * n_g
    h = p = None
    for g in order:
        ag, xg = a[8 * g:8 * g + 8], x[8 * g:8 * g + 8]
        h = xg if h is None else ag * h + xg
        p = ag if p is None else ag * p
        hs[g], ps[g] = h, p
    sub = lax.broadcasted_iota(jnp.int32, h.shape, 0)
    for s in (1, 2, 4):
        shift = 8 - s if reverse else s
        inside = (sub < 8 - s) if reverse else (sub >= s)
        h = h + p * jnp.where(inside, pltpu.roll(h, shift, 0), 0.0)
        p = p * jnp.where(inside, pltpu.roll(p, shift, 0), 1.0)
    full = h + p * carry
    first, last = (7, 0) if reverse else (0, 7)
    seg_in = jnp.where(sub == first, carry, pltpu.roll(full, 7 if reverse else 1, 0))
    out = jnp.concatenate([hs[g] + ps[g] * seg_in for g in range(n_g)], axis=0)
    return out, jnp.broadcast_to(full[last:last + 1], full.shape)


def _rnn_core_kernel(upf_ref, ucf_ref, unf_ref, upb_ref, ucb_ref, unb_ref,
                     cw_ref, cb_ref, wg_ref, lam_ref,
                     perm_ref, unperm_ref, hf_ref, hb_ref, carry_ref, *, n_ctx, n_all):
    n = pl.program_id(1)

    @pl.when(n == 0)
    def _():
        carry_ref[...] = jnp.zeros_like(carry_ref)

    tiles = (n, _bwd_order(n, n_ctx, n_all))
    prev_ok = [jnp.where(jnp.logical_and(t != 0, t != n_ctx), 1.0, 0.0) for t in tiles]
    next_ok = [jnp.where(jnp.logical_and(t != n_ctx - 1, t != n_all - 1), 1.0, 0.0) for t in tiles]
    refs = ((upf_ref, ucf_ref, unf_ref, hf_ref), (upb_ref, ucb_ref, unb_ref, hb_ref))

    useg = [_dot(perm_ref[...], refs[d][1][...]) for d in range(2)]
    rows = useg[0].shape[0]
    bias_taps = (lax.broadcasted_iota(jnp.int32, (rows, RNN_HD), 1) < 2).astype(BF16)
    convs, pres = {}, {}
    for d, (p_ref, _, n_ref, _) in enumerate(refs):
        before = p_ref[...].astype(F32)[HALO - 8:] * prev_ok[d]
        after = n_ref[...].astype(F32)[:8] * next_ok[d]
        before2 = pltpu.roll(before, 1, 0)
        for j in range(RNN_HEADS):
            lanes = slice(j * RNN_HD, (j + 1) * RNN_HD)
            u = useg[d][:, lanes]
            ext = jnp.concatenate([
                _shift_rows(u[rows - 16:rows - 8], before2[:, lanes], up=False),
                _shift_rows(u[rows - 8:], before[:, lanes], up=False),
                u,
                _shift_rows(u[:8], after[:, lanes], up=True)], axis=0)
            conv = cb_ref[:, lanes]
            for tap in range(CONV_W):
                conv = conv + ext[8 * tap:8 * tap + rows] * cw_ref[tap:tap + 1, lanes]
            convs[d, j] = conv
            lhs = jnp.concatenate([conv.astype(BF16), bias_taps], axis=1)
            pres[d, j] = _dot(lhs, wg_ref[d, j])

    half = RNN_HEADS // 2
    for j0 in (0, half):
        for d in range(2):
            outs = []
            for j in range(j0, j0 + half):
                pre, conv = pres[d, j], convs[d, j]
                u = jnp.tanh(pre[:, :RNN_HD]) + 1.0
                i = jnp.tanh(pre[:, RNN_HD:]) + 1.0
                rate = (0.5 * LRU_C) * _softplus(-lam_ref[d, j])
                a = jnp.exp2((-LOG2_E * rate) * u)
                y = 1.0 - a * a
                x = jnp.where(y > 0.0, y * lax.rsqrt(y), 0.0) * i * conv
                hs, carry = _segment_scan(a, x, carry_ref[d, j], reverse=(d == 1))
                carry_ref[d, j] = carry
                outs.append(hs.astype(BF16))
            h = _dot(unperm_ref[...], jnp.concatenate(outs, axis=1))
            refs[d][3][:, j0 * RNN_HD:(j0 + half) * RNN_HD] = h.astype(BF16)


def _rnn_core(u, conv_w, conv_b, wg, lam, n_ctx_rows):
    bsz, tt, rw = u.shape
    tile = RNN_TILE
    n_all, n_ctx = tt // tile, n_ctx_rows // tile
    per = tile // HALO
    n_halo = tt // HALO
    fwd = lambda n: n
    bwd = lambda n: _bwd_order(n, n_ctx, n_all)

    def specs(order):
        return [
            pl.BlockSpec((None, HALO, rw), lambda b, n: (b, jnp.maximum(order(n) * per - 1, 0), 0)),
            pl.BlockSpec((None, tile, rw), lambda b, n: (b, order(n), 0)),
            pl.BlockSpec((None, HALO, rw), lambda b, n: (b, jnp.minimum((order(n) + 1) * per, n_halo - 1), 0)),
        ]

    out_spec = lambda order: pl.BlockSpec((None, tile, rw), lambda b, n: (b, order(n), 0))
    perm, unperm = _segment_permutation(tile)
    consts = (conv_w, conv_b, wg, lam, perm, unperm)
    return pl.pallas_call(
        functools.partial(_rnn_core_kernel, n_ctx=n_ctx, n_all=n_all),
        out_shape=[jax.ShapeDtypeStruct((bsz, tt, rw), BF16)] * 2,
        grid=(bsz, n_all),
        in_specs=specs(fwd) + specs(bwd) + [_full_spec(c.shape) for c in consts],
        out_specs=[out_spec(fwd), out_spec(bwd)],
        scratch_shapes=[pltpu.VMEM((2, RNN_HEADS, 8, RNN_HD), F32)],
        compiler_params=_cparams("arbitrary", "arbitrary"),
        name="rglru_core",
    )(u, u, u, u, u, u, *consts)


def _residual(u, x, mod_ref, gp_ref, w_ref):
    y = _dot(u.astype(BF16), w_ref[...])
    return x + mod_ref[2:3] * (_rms(y) * gp_ref[...])


def _gla_post_part(x, mod_ref, gp_ref, w_ref, of_ref, ob_ref, z_ref, gh_ref, rows=slice(None)):
    o = of_ref[rows].astype(F32) + ob_ref[rows].astype(F32)
    hv = o.shape[-1] // GLA_HEADS
    o = jnp.concatenate([_rms(o[:, h * hv:(h + 1) * hv]) for h in range(GLA_HEADS)], axis=1)
    u = o * gh_ref[...] * _silu(z_ref[rows].astype(F32))
    return _residual(u, x, mod_ref, gp_ref, w_ref)


def _swa_post_part(x, mod_ref, gp_ref, w_ref, a_ref, z_ref, rows=slice(None)):
    u = a_ref[rows].astype(F32) * _silu(z_ref[rows].astype(F32))
    return _residual(u, x, mod_ref, gp_ref, w_ref)


def _rnn_post_part(x, mod_ref, gp_ref, w_ref, hf_ref, hb_ref, z_ref, rows=slice(None)):
    u = (hf_ref[rows].astype(F32) + hb_ref[rows].astype(F32)) * _silu(z_ref[rows].astype(F32))
    return _residual(u, x, mod_ref, gp_ref, w_ref)


class _Pre(NamedTuple):
    name: str
    part: Callable
    mod: jax.Array
    g_pre: jax.Array
    consts: tuple
    row_consts: tuple
    widths: tuple
    dtypes: tuple
    stats: int


class _Post(NamedTuple):
    name: str
    part: Callable
    mod: jax.Array
    g_post: jax.Array
    w_out: jax.Array
    acts: tuple
    vecs: tuple


def _pre_kernel(*refs, part, n_x, n_ctx):
    part(_token_tile(refs[:n_x], n_ctx), *refs[n_x:])


def _post_halves(part, x, refs):
    half = x.shape[0] // 2
    return jnp.concatenate([part(x[rows], *refs, rows=rows)
                            for rows in (slice(0, half), slice(half, 2 * half))], axis=0)


def _post_kernel(*refs, part, n_x, n_ctx):
    refs[-1][...] = _post_halves(part, _token_tile(refs[:n_x], n_ctx), refs[n_x:-1])


def _mid_kernel(*refs, post_part, n_post, pre_part, n_pre, n_x, n_ctx):
    post_in, pre_in = refs[n_x:n_x + n_post], refs[n_x + n_post:n_x + n_post + n_pre]
    xo_ref, pre_out = refs[n_x + n_post + n_pre], refs[n_x + n_post + n_pre + 1:]
    x = _post_halves(post_part, _token_tile(refs[:n_x], n_ctx), post_in)
    xo_ref[...] = x
    pre_part(x, *pre_in, *pre_out)


def _stream_shape(xs):
    if isinstance(xs, tuple):
        bsz, n_ctx_rows, d = xs[0].shape
        return bsz, n_ctx_rows + xs[1].shape[1], d
    return xs.shape


def _pre_io(pre, bsz, tt, d, n_ctx):
    operands = (pre.mod, pre.g_pre, *pre.consts, *pre.row_consts)
    in_specs = ([_mod_spec(d, n_ctx, bsz), _full_spec(pre.g_pre.shape)]
                + [_full_spec(c.shape) for c in pre.consts]
                + [pl.BlockSpec((ROW_TILE, r.shape[-1]), lambda b, t: (t, 0)) for r in pre.row_consts])
    out_shape = ([jax.ShapeDtypeStruct((bsz, tt, w), dt) for w, dt in zip(pre.widths, pre.dtypes)]
                 + [jax.ShapeDtypeStruct((bsz, tt // ROW_TILE, 8, 128), F32)] * pre.stats)
    out_specs = ([_row_spec(w) for w in pre.widths]
                 + [pl.BlockSpec((None, None, 8, 128), lambda b, t: (b, t, 0, 0))] * pre.stats)
    return operands, in_specs, out_shape, out_specs


def _post_io(post, bsz, d, n_ctx, off):
    operands = (post.mod, post.g_post, post.w_out, *post.acts, *post.vecs)
    in_specs = ([_mod_spec(d, n_ctx, bsz, off), _full_spec(post.g_post.shape), _full_spec(post.w_out.shape)]
                + [_row_spec(a.shape[-1], off) for a in post.acts]
                + [_full_spec(v.shape) for v in post.vecs])
    return operands, in_specs


def _pre_call(pre, xs, n_ctx_rows):
    bsz, tt, d = _stream_shape(xs)
    n_ctx = n_ctx_rows // ROW_TILE
    streams = xs if isinstance(xs, tuple) else (xs,)
    operands, in_specs, out_shape, out_specs = _pre_io(pre, bsz, tt, d, n_ctx)
    return pl.pallas_call(
        functools.partial(_pre_kernel, part=pre.part, n_x=len(streams), n_ctx=n_ctx),
        out_shape=out_shape,
        grid=(bsz, tt // ROW_TILE),
        in_specs=_token_specs(xs, n_ctx) + in_specs,
        out_specs=out_specs,
        compiler_params=_cparams("parallel", "parallel"),
        name=pre.name,
    )(*streams, *operands)


def _post_call(post, xs, n_ctx_rows, latent_only):
    bsz, tt, d = _stream_shape(xs)
    n_ctx = n_ctx_rows // ROW_TILE
    off = n_ctx if latent_only else 0
    n_tiles = tt // ROW_TILE - off
    streams = xs if isinstance(xs, tuple) else (xs,)
    operands, in_specs = _post_io(post, bsz, d, n_ctx, off)
    return pl.pallas_call(
        functools.partial(_post_kernel, part=post.part, n_x=len(streams), n_ctx=n_ctx - off),
        out_shape=jax.ShapeDtypeStruct((bsz, n_tiles * ROW_TILE, d), F32),
        grid=(bsz, n_tiles),
        in_specs=_token_specs(xs, n_ctx, off) + in_specs,
        out_specs=_row_spec(d),
        compiler_params=_cparams("parallel", "parallel"),
        name=post.name,
    )(*streams, *operands)


def _final_call(post, xs, n_ctx_rows):
    bsz, tt, d = xs.shape
    rows = lambda w: pl.BlockSpec((pl.Element(1), pl.Element(FINAL_TILE), pl.Element(w)),
                                  lambda b, t: (b, pl.multiple_of(n_ctx_rows + t * FINAL_TILE, 8), 0))
    operands = (post.mod, post.g_post, post.w_out, *post.acts, *post.vecs)
    in_specs = ([rows(d), pl.BlockSpec((None, 3, d), lambda b, t: (b, 0, 0)),
                 _full_spec(post.g_post.shape), _full_spec(post.w_out.shape)]
                + [rows(a.shape[-1]) for a in post.acts]
                + [_full_spec(v.shape) for v in post.vecs])

    def body(x_ref, mod_ref, gp_ref, w_ref, *rest):
        acts = [r.at[0] for r in rest[:len(post.acts)]]
        vecs, o_ref = rest[len(post.acts):-1], rest[-1]
        o_ref[...] = _post_halves(post.part, x_ref[0], (mod_ref, gp_ref, w_ref, *acts, *vecs))

    return pl.pallas_call(
        body,
        out_shape=jax.ShapeDtypeStruct((bsz, tt - n_ctx_rows, d), F32),
        grid=(bsz, (tt - n_ctx_rows) // FINAL_TILE),
        in_specs=in_specs,
        out_specs=pl.BlockSpec((None, FINAL_TILE, d), lambda b, t: (b, t, 0)),
        compiler_params=_cparams("parallel", "parallel"),
        name=post.name,
    )(xs, *operands)


def _mid_call(post, pre, xs, n_ctx_rows):
    bsz, tt, d = _stream_shape(xs)
    n_ctx = n_ctx_rows // ROW_TILE
    streams = xs if isinstance(xs, tuple) else (xs,)
    post_ops, post_specs = _post_io(post, bsz, d, n_ctx, 0)
    pre_ops, pre_specs, pre_shape, pre_out_specs = _pre_io(pre, bsz, tt, d, n_ctx)
    outs = pl.pallas_call(
        functools.partial(_mid_kernel, post_part=post.part, n_post=len(post_ops),
                          pre_part=pre.part, n_pre=len(pre_ops), n_x=len(streams), n_ctx=n_ctx),
        out_shape=[jax.ShapeDtypeStruct((bsz, tt, d), F32)] + pre_shape,
        grid=(bsz, tt // ROW_TILE),
        in_specs=_token_specs(xs, n_ctx) + post_specs + pre_specs,
        out_specs=[_row_spec(d)] + pre_out_specs,
        compiler_params=_cparams("parallel", "parallel"),
        name=post.name + "_" + pre.name,
    )(*streams, *post_ops, *pre_ops)
    return outs[0], outs[1:]


def _gla_layer(mod, g_pre, g_post, w_in, w_g1, w_g2, b_g, g_head, w_out, n_ctx_rows):
    d = w_in.shape[0]
    rank, dk = w_g2.shape[1], w_g2.shape[2]
    dv = w_out.shape[0]
    wg1 = jnp.zeros((d, GLA_GATE_PAD), F32).at[:, :2 * rank].set(jnp.concatenate([w_g1[0], w_g1[1]], axis=1))
    wg2 = jnp.zeros((GLA_GATE_PAD, 2 * dk), F32)
    wg2 = wg2.at[:rank, :dk].set(w_g2[0]).at[rank:2 * rank, dk:].set(w_g2[1])
    pre = _Pre("gla_pre", functools.partial(_gla_pre_part, dk=dk), mod, g_pre,
               (w_in.astype(BF16), wg1.astype(BF16), wg2.astype(BF16), b_g.reshape(1, 2 * dk)), (),
               (2 * dk, dv, dv, 2 * dk), (BF16, BF16, BF16, BF16), 1)

    def mixer(outs):
        qk, v, z, la, peak = outs
        exact = (peak[:, :, 0, 0] * (GLA_BLOCK // 2) > GLA_SAFE_EXPONENT).astype(jnp.int32)
        o_f, o_b = _gla_core(qk, v, la, exact, n_ctx_rows)
        return _Post("gla_post", _gla_post_part, mod, g_post, w_out.astype(BF16), (o_f, o_b, z),
                     (g_head.reshape(1, dv),))

    return pre, mixer


def _swa_layer(mod, g_pre, g_post, w_in, sink, w_out, rope, n_ctx_rows):
    qw = w_out.shape[0]
    kw = (w_in.shape[1] - 2 * qw) // 2
    pre = _Pre("swa_pre", _swa_pre_part, mod, g_pre, (w_in.astype(BF16),), rope,
               (qw, 2 * kw, qw), (BF16, BF16, BF16), 0)

    def mixer(outs):
        q, kv, z = outs
        a = _swa_core(q, kv, sink, n_ctx_rows)
        return _Post("swa_post", _swa_post_part, mod, g_post, w_out.astype(BF16), (a, z), ())

    return pre, mixer


def _rnn_layer(mod, g_pre, g_post, w_in, conv_w, conv_b, w_ra, b_ra, w_ri, b_ri, lam, w_out, n_ctx_rows):
    rw = w_out.shape[0]
    pre = _Pre("rglru_pre", _rnn_pre_part, mod, g_pre, (w_in.astype(BF16),), (), (rw, rw), (BF16, BF16), 0)
    vec = lambda a: a.reshape(2, RNN_HEADS, 1, RNN_HD)
    bias = 0.5 * jnp.concatenate([vec(b_ra), vec(b_ri)], axis=-1)
    bias_hi = bias.astype(BF16)
    bias_lo = (bias - bias_hi.astype(F32)).astype(BF16)
    wg = jnp.concatenate([jnp.concatenate([w_ra, w_ri], axis=-1).astype(BF16), bias_hi, bias_lo,
                          jnp.zeros((2, RNN_HEADS, RNN_HD - 2, 2 * RNN_HD), BF16)], axis=2)

    def mixer(outs):
        u, z = outs
        h_f, h_b = _rnn_core(u, 0.5 * conv_w, 0.5 * conv_b.reshape(1, rw), wg, vec(lam), n_ctx_rows)
        return _Post("rglru_post", _rnn_post_part, mod, g_post, w_out.astype(BF16), (h_f, h_b, z), ())

    return pre, mixer


def _rope_tables(t, n_ctx_rows):
    pairs = ATT_HEAD_DIM // 4
    pos = jnp.arange(t, dtype=jnp.int32)
    row = (pos // GRID_W).astype(F32)
    col = (pos % GRID_W).astype(F32)
    freqs = ROPE_BASE ** (-jnp.arange(pairs, dtype=F32) / pairs)
    ang_r, ang_c = row[:, None] * freqs, col[:, None] * freqs
    cos = jnp.concatenate([jnp.cos(ang_r)] * 2 + [jnp.cos(ang_c)] * 2, axis=1)
    sin = jnp.concatenate([-jnp.sin(ang_r), jnp.sin(ang_r), -jnp.sin(ang_c), jnp.sin(ang_c)], axis=1)
    cos = jnp.concatenate([jnp.ones((n_ctx_rows, ATT_HEAD_DIM), F32), cos], axis=0)
    sin = jnp.concatenate([jnp.zeros((n_ctx_rows, ATT_HEAD_DIM), F32), sin], axis=0)
    return jnp.tile(cos, (1, 2)), jnp.tile(sin, (1, 2))


def kernel(x, c, ctx, c_ctx, w_mod, b_mod, g_pre, g_post, a_w_in, a_w_g1, a_w_g2, a_b_g, a_g_head, a_w_out, b_w_in, b_sink, b_w_out, c_w_in, c_conv_w, c_conv_b, c_w_ra, c_b_ra, c_w_ri, c_b_ri, c_lam, c_w_out):
    bsz, t, d = x.shape
    n_ctx_rows = ctx.shape[1]
    depth = w_mod.shape[0]
    assert bsz < MOD_ROWS and n_ctx_rows % ROW_TILE == 0 and t % ROW_TILE == 0

    cvec = jnp.zeros((MOD_ROWS, d), F32).at[:bsz].set(c).at[bsz].set(c_ctx)
    mods = _modulation(cvec, w_mod, b_mod)
    rope = _rope_tables(t, n_ctx_rows)
    layers = []
    for i in range(depth):
        kind, j = i % N_MIXERS, i // N_MIXERS
        gpre, gpost = g_pre[i].reshape(1, d), g_post[i].reshape(1, d)
        if kind == 0:
            layers.append(_gla_layer(mods[i], gpre, gpost, a_w_in[j], a_w_g1[j], a_w_g2[j], a_b_g[j],
                                     a_g_head[j], a_w_out[j], n_ctx_rows))
        elif kind == 1:
            layers.append(_swa_layer(mods[i], gpre, gpost, b_w_in[j], b_sink[j], b_w_out[j], rope, n_ctx_rows))
        else:
            layers.append(_rnn_layer(mods[i], gpre, gpost, c_w_in[j], c_conv_w[j], c_conv_b[j], c_w_ra[j],
                                     c_b_ra[j], c_w_ri[j], c_b_ri[j], c_lam[j], c_w_out[j], n_ctx_rows))

    xs = (ctx, x)
    outs = _pre_call(layers[0][0], xs, n_ctx_rows)
    for i in range(depth - 1):
        xs, outs = _mid_call(layers[i][1](outs), layers[i + 1][0], xs, n_ctx_rows)
    if isinstance(xs, tuple) or t % FINAL_TILE:
        return _post_call(layers[-1][1](outs), xs, n_ctx_rows, latent_only=True)
    return _final_call(layers[-1][1](outs), xs, n_ctx_rows)
```

```python
import functools
from typing import Callable, NamedTuple

import jax
import jax.numpy as jnp
from jax import lax
from jax.experimental import pallas as pl
from jax.experimental.pallas import tpu as pltpu

F32 = jnp.float32
BF16 = jnp.bfloat16

EPS = 1e-6
GRID_W = 64
N_MIXERS = 3

GLA_HEADS = 4
GLA_TAU = 16.0
GLA_GATE_PAD = 128

ATT_HEAD_DIM = 64
ATT_KV_HEADS = 4
ATT_GROUP = 4
ATT_BLOCK = 128
ATT_ROWS = 32
ATT_STEP_BLOCKS = 2
ROPE_BASE = 10000.0
LOG2_E = 1.4426950408889634

RNN_HEADS = 10
RNN_HD = 128
CONV_W = 4
LRU_C = 8.0

ROW_TILE = 256
FINAL_TILE = 1024
GLA_TILE = 256
GLA_CHUNK = 128
GLA_BLOCK = 64
GLA_SAFE_EXPONENT = 60.0
RNN_TILE = 256
HALO = 16
MOD_ROWS = 8
VMEM_LIMIT = 48 * 1024 * 1024


def _cparams(*sem):
    return pltpu.CompilerParams(dimension_semantics=sem, vmem_limit_bytes=VMEM_LIMIT)


def _silu(x):
    half = 0.5 * x
    return half * (jnp.tanh(half) + 1.0)


def _softplus(x):
    return jnp.maximum(x, 0.0) + jnp.log1p(jnp.exp(-jnp.abs(x)))


def _log_sigmoid(x):
    return jnp.minimum(x, 0.0) - jnp.log(1.0 + jnp.exp(-jnp.abs(x)))


def _split3(a):
    hi = a.astype(BF16)
    r = a - hi.astype(F32)
    mid = r.astype(BF16)
    lo = (r - mid.astype(F32)).astype(BF16)
    return hi, mid, lo


def _dot(a, b):
    return jnp.dot(a, b, preferred_element_type=F32)


def _dot_nt(a, b):
    return lax.dot_general(a, b, (((1,), (1,)), ((), ())), preferred_element_type=F32)


def _dot_tn(a, b):
    return lax.dot_general(a, b, (((0,), (0,)), ((), ())), preferred_element_type=F32)


def _rms(x):
    return x * lax.rsqrt(jnp.mean(x * x, axis=-1, keepdims=True) + EPS)


def _mod_kernel(c_ref, w_ref, b_ref, o_ref):
    s = _silu(c_ref[...])
    s_hi, s_mid, s_lo = _split3(s)
    w_hi, w_mid, w_lo = _split3(w_ref[...])
    acc = _dot(s_hi, w_hi)
    acc += _dot(s_hi, w_mid) + _dot(s_mid, w_hi)
    acc += _dot(s_hi, w_lo) + _dot(s_mid, w_mid) + _dot(s_lo, w_hi)
    o_ref[...] = acc + b_ref[...]


def _modulation(cvec, w_mod, b_mod):
    depth, d, d3 = w_mod.shape
    out = pl.pallas_call(
        _mod_kernel,
        out_shape=jax.ShapeDtypeStruct((depth, MOD_ROWS, d3), F32),
        grid=(depth, d3 // d),
        in_specs=[
            pl.BlockSpec((MOD_ROWS, d), lambda i, j: (0, 0)),
            pl.BlockSpec((None, d, d), lambda i, j: (i, 0, j)),
            pl.BlockSpec((None, 1, d), lambda i, j: (i, 0, j)),
        ],
        out_specs=pl.BlockSpec((None, MOD_ROWS, d), lambda i, j: (i, 0, j)),
        compiler_params=_cparams("parallel", "parallel"),
        name="modulation",
    )(cvec, w_mod, b_mod.reshape(depth, 1, d3))
    return out.reshape(depth, MOD_ROWS, 3, d)


def _prenorm(x, g, mod):
    return _rms(x) * (g * (1.0 + mod[1:2])) + mod[0:1]


def _mod_spec(d, n_ctx_tiles, ctx_row, offset=0):
    def index(b, t):
        return (jnp.where(t + offset < n_ctx_tiles, ctx_row, b), 0, 0)
    return pl.BlockSpec((None, 3, d), index)


def _row_spec(width, offset=0):
    return pl.BlockSpec((None, ROW_TILE, width), lambda b, t: (b, t + offset, 0))


def _token_specs(xs, n_ctx, offset=0):
    if not isinstance(xs, tuple):
        return [_row_spec(xs.shape[-1], offset)]
    d = xs[0].shape[-1]
    return [pl.BlockSpec((None, ROW_TILE, d), lambda b, t: (b, jnp.minimum(t + offset, n_ctx - 1), 0)),
            pl.BlockSpec((None, ROW_TILE, d), lambda b, t: (b, jnp.maximum(t + offset - n_ctx, 0), 0))]


def _token_tile(x_refs, n_ctx_tiles):
    if len(x_refs) == 1:
        return x_refs[0][...]
    return jnp.where(pl.program_id(1) < n_ctx_tiles, x_refs[0][...], x_refs[1][...])


def _full_spec(shape):
    zeros = (0,) * len(shape)
    return pl.BlockSpec(shape, lambda b, t: zeros)


def _gla_pre_part(x, mod_ref, g_ref, w_ref, wg1_ref, wg2_ref, bg_ref, qk_ref, v_ref, z_ref, la_ref, peak_ref,
                  *, dk):
    h = _prenorm(x, g_ref[...], mod_ref[...]).astype(BF16)
    hk = dk // GLA_HEADS
    dv = v_ref.shape[-1]
    g1 = _dot(h, wg1_ref[...]).astype(BF16)
    gate = _dot(g1, wg2_ref[...]) + bg_ref[...]
    la = _log_sigmoid(gate) / GLA_TAU
    la_ref[...] = la.astype(BF16)
    peak = jnp.max(jnp.max(-la, axis=1, keepdims=True), axis=0, keepdims=True)
    peak_ref[...] = jnp.broadcast_to(peak, peak_ref.shape)
    q = _dot(h, w_ref[:, :dk]) * hk ** -0.5
    qk_ref[:, :dk] = q.astype(BF16)
    qk_ref[:, dk:] = _dot(h, w_ref[:, dk:2 * dk]).astype(BF16)
    v_ref[...] = _dot(h, w_ref[:, 2 * dk:2 * dk + dv]).astype(BF16)
    z_ref[...] = _dot(h, w_ref[:, 2 * dk + dv:]).astype(BF16)


def _gla_core_kernel(exact_ref, qkf_ref, vf_ref, laf_ref, qkb_ref, vb_ref, lab_ref,
                     of_ref, ob_ref, st_ref, row_ref, *, dk, n_ctx, n_all):
    b, n = pl.program_id(0), pl.program_id(1)

    @pl.when(n == 0)
    def _():
        st_ref[...] = jnp.zeros_like(st_ref)

    io = ((qkf_ref, vf_ref, laf_ref, of_ref), (qkb_ref, vb_ref, lab_ref, ob_ref))
    n_sub = qkf_ref.shape[0] // GLA_CHUNK

    def run(exact):
        chunks = [_gla_chunk_pair(io, (s * GLA_CHUNK, (n_sub - 1 - s) * GLA_CHUNK), st_ref, dk,
                                  row_ref if exact else None) for s in range(n_sub)]
        for _ in range(3):
            for chunk in chunks:
                next(chunk, None)

    out_of_range = exact_ref[b, n] + exact_ref[b, _bwd_order(n, n_ctx, n_all)]
    pl.when(out_of_range == 0)(lambda: run(False))
    pl.when(out_of_range != 0)(lambda: run(True))


def _gla_exact_intra(d, q, k, v, b, row_ref):
    c, hk = q.shape
    row_ref[:, :hk] = k
    row_ref[:, hk:2 * hk] = b
    row_ref[:, 2 * hk:] = v.astype(F32)
    t_idx = lax.broadcasted_iota(jnp.int32, (c, 1), 0)

    def keys(g, acc):
        group = row_ref[pl.ds(pl.multiple_of(g * 8, 8), 8), :]
        for r in range(8):
            s = g * 8 + r
            k_s, b_s, v_s = group[r:r + 1, :hk], group[r:r + 1, hk:2 * hk], group[r:r + 1, 2 * hk:]
            decay = jnp.exp(jnp.minimum(b - b_s, 0.0))
            col = jnp.sum(q * decay * k_s, axis=1, keepdims=True)
            causal = (t_idx >= s) if d == 0 else (t_idx <= s)
            acc = acc + jnp.where(causal, col, 0.0) * v_s
        return acc

    return lax.fori_loop(0, c // 8, keys, jnp.zeros((c, v.shape[1]), F32))


def _gla_chunk_pair(io, base, st_ref, dk, row_ref):
    c = GLA_CHUNK
    hk = dk // GLA_HEADS
    hv = io[0][1].shape[-1] // GLA_HEADS
    row = lax.broadcasted_iota(jnp.int32, (c, c), 0)
    col = lax.broadcasted_iota(jnp.int32, (c, c), 1)
    blk = GLA_BLOCK
    keeps = (col <= row, col >= row)
    lasts = (c - 1, 0)
    tile = [slice(r0, r0 + c) for r0 in base]
    cums = []
    for d, (_, _, la_ref, _) in enumerate(io):
        tri = jnp.where(keeps[d], 1.0, 0.0).astype(BF16)
        cums.append(_dot(tri, la_ref[tile[d], d * dk:(d + 1) * dk]))
    yield

    stage = {}
    for d, (qk_ref, v_ref, _, _) in enumerate(io):
        for h in range(GLA_HEADS):
            b = cums[d][:, h * hk:(h + 1) * hk]
            b_last = b[lasts[d]:lasts[d] + 1]
            q = qk_ref[tile[d], h * hk:(h + 1) * hk].astype(F32)
            k = qk_ref[tile[d], dk + h * hk:dk + (h + 1) * hk].astype(F32)
            v = v_ref[tile[d], h * hv:(h + 1) * hv]
            scores = []
            for i in (range(c // blk) if row_ref is None else ()):
                rows = slice(i * blk, (i + 1) * blk)
                keys = slice(0, (i + 1) * blk) if d == 0 else slice(i * blk, c)
                b_ref = b[i * blk + blk // 2:i * blk + blk // 2 + 1]
                qe = (q[rows] * jnp.exp(b[rows] - b_ref)).astype(BF16)
                ke = (k[keys] * jnp.exp(b_ref - b[keys])).astype(BF16)
                scores.append((rows, keys, _dot_nt(qe, ke)))
            qs = (q * jnp.exp(b)).astype(BF16)
            kd = (k * jnp.exp(b_last - b)).astype(BF16)
            st = st_ref[d, h]
            inter = _dot_nt(qs, st.astype(BF16))
            st_ref[d, h] = st * jnp.exp(b_last) + _dot_tn(v, kd)
            stage[d, h] = (scores, inter, v, (q, k, b))
    yield

    for d, (_, _, _, o_ref) in enumerate(io):
        for h in range(GLA_HEADS):
            scores, inter, v, (q, k, b) = stage[d, h]
            if row_ref is not None:
                o = _gla_exact_intra(d, q, k, v, b, row_ref) + inter
                o_ref[tile[d], h * hv:(h + 1) * hv] = o.astype(BF16)
            for rows, keys, s in scores:
                t_idx = lax.broadcasted_iota(jnp.int32, s.shape, 0) + rows.start
                s_idx = lax.broadcasted_iota(jnp.int32, s.shape, 1) + keys.start
                p = jnp.where((s_idx <= t_idx) if d == 0 else (s_idx >= t_idx), s, 0.0).astype(BF16)
                o = _dot(p, v[keys]) + inter[rows]
                o_ref[base[d] + rows.start:base[d] + rows.stop, h * hv:(h + 1) * hv] = o.astype(BF16)


def _bwd_order(n, n_ctx, n_all):
    return jnp.where(n < n_ctx, n_ctx - 1 - n, n_all - 1 - (n - n_ctx))


def _gla_core(qk, v, la, exact, n_ctx_rows):
    bsz, tt, dv = v.shape
    dk = qk.shape[-1] // 2
    c = GLA_TILE
    n_all, n_ctx = tt // c, n_ctx_rows // c
    hk, hv = dk // GLA_HEADS, dv // GLA_HEADS
    fwd = lambda b, n: (b, n, 0)
    bwd = lambda b, n: (b, _bwd_order(n, n_ctx, n_all), 0)
    spec = lambda w, idx: pl.BlockSpec((None, c, w), idx)
    return pl.pallas_call(
        functools.partial(_gla_core_kernel, dk=dk, n_ctx=n_ctx, n_all=n_all),
        out_shape=[jax.ShapeDtypeStruct((bsz, tt, dv), BF16)] * 2,
        grid=(bsz, n_all),
        in_specs=[pl.BlockSpec(memory_space=pltpu.SMEM),
                  spec(2 * dk, fwd), spec(dv, fwd), spec(2 * dk, fwd),
                  spec(2 * dk, bwd), spec(dv, bwd), spec(2 * dk, bwd)],
        out_specs=[spec(dv, fwd), spec(dv, bwd)],
        scratch_shapes=[pltpu.VMEM((2, GLA_HEADS, hv, hk), F32),
                        pltpu.VMEM((GLA_CHUNK, 2 * hk + hv), F32)],
        compiler_params=_cparams("arbitrary", "arbitrary"),
        name="gla_core",
    )(exact, qk, v, la, qk, v, la)


def _rope(x, cos, sin_signed):
    lane = lax.broadcasted_iota(jnp.int32, (x.shape[0], 128), 1)
    first = (lane % 32) < 16
    out = []
    for j in range(x.shape[1] // 128):
        xj = x[:, j * 128:(j + 1) * 128]
        partner = jnp.where(first, pltpu.roll(xj, 112, 1), pltpu.roll(xj, 16, 1))
        out.append(xj * cos + partner * sin_signed)
    return jnp.concatenate(out, axis=1)


def _swa_pre_part(x, mod_ref, g_ref, w_ref, cos_ref, sin_ref, q_ref, kv_ref, z_ref):
    h = _prenorm(x, g_ref[...], mod_ref[...]).astype(BF16)
    qw = q_ref.shape[-1]
    kw = kv_ref.shape[-1] // 2
    cos, sin = cos_ref[...], sin_ref[...]
    q = _rope(_dot(h, w_ref[:, :qw]), cos, sin) * (ATT_HEAD_DIM ** -0.5 * LOG2_E)
    q_ref[...] = q.astype(BF16)
    kv_ref[:, :kw] = _rope(_dot(h, w_ref[:, qw:qw + kw]), cos, sin).astype(BF16)
    kv_ref[:, kw:] = _dot(h, w_ref[:, qw + kw:qw + 2 * kw]).astype(BF16)
    z_ref[...] = _dot(h, w_ref[:, qw + 2 * kw:]).astype(BF16)


def _swa_core_kernel(sink_ref, q_ref, kvp_ref, kvc_ref, kvn_ref, kvx_ref, o_ref, *, n_ctx, n_all):
    blk = ATT_BLOCK
    own = [kvc_ref.at[i * blk:(i + 1) * blk] for i in range(ATT_STEP_BLOCKS)]
    window = [kvp_ref] + own + [kvn_ref]
    blocks = [_swa_block(sink_ref, q_ref.at[i * blk:(i + 1) * blk], window[i:i + 3], kvx_ref,
                         o_ref.at[i * blk:(i + 1) * blk], pl.program_id(1) * ATT_STEP_BLOCKS + i, n_ctx, n_all)
              for i in range(ATT_STEP_BLOCKS)]
    for _ in range(3):
        for block in blocks:
            next(block, None)


def _swa_block(sink_ref, q_ref, kv_refs, kvx_ref, o_ref, t, n_ctx, n_all):
    blk = ATT_BLOCK
    kvp_ref, kvc_ref, kvn_ref = kv_refs
    n_keys = 3 * blk + kvx_ref.shape[0]
    kw = kvc_ref.shape[-1] // 2
    n_loc = 3 * blk
    i = lax.broadcasted_iota(jnp.int32, (blk, n_loc), 0)
    j = lax.broadcasted_iota(jnp.int32, (blk, n_loc), 1)
    never = n_keys
    off_prev = jnp.where(t > n_ctx, 0, never)
    off_cur = jnp.where(t >= n_ctx, 0, never)
    off_next = jnp.where(jnp.logical_and(t >= n_ctx, t < n_all - 1), 0, never)
    valid = (((j >= i + off_prev) & (j < blk))
             | ((j >= blk + off_cur) & (j < 2 * blk))
             | ((j >= 2 * blk) & (j <= i + 2 * blk - off_next)))
    bias = jnp.where(valid, 0.0, -jnp.inf)
    logits = []
    for h in range(ATT_KV_HEADS):
        ks = slice(h * ATT_HEAD_DIM, (h + 1) * ATT_HEAD_DIM)
        k = jnp.concatenate([kvp_ref[:, ks], kvc_ref[:, ks], kvn_ref[:, ks], kvx_ref[:, ks]], axis=0)
        q = jnp.concatenate(
            [q_ref[:, (h * ATT_GROUP + g) * ATT_HEAD_DIM:(h * ATT_GROUP + g + 1) * ATT_HEAD_DIM]
             for g in range(ATT_GROUP)], axis=0)
        logits.append(_dot_nt(q, k))
    yield

    chunks = [(h, r0) for h in range(ATT_KV_HEADS) for r0 in range(0, ATT_GROUP * blk, ATT_ROWS)]

    def slabs(h, r0):
        out = []
        for c0 in range(0, n_keys, 128):
            s = logits[h][r0:r0 + ATT_ROWS, c0:c0 + 128]
            if c0 < n_loc:
                s = s + bias[r0 % blk:r0 % blk + ATT_ROWS, c0:c0 + 128]
            out.append(s)
        return out

    sinks = {(h, r0): sink_ref[h * ATT_GROUP + r0 // blk] * LOG2_E for h, r0 in chunks}
    maxes = {}
    for h, r0 in chunks:
        lane_max = functools.reduce(jnp.maximum, slabs(h, r0))
        maxes[h, r0] = jnp.maximum(jnp.max(lane_max, axis=-1, keepdims=True), sinks[h, r0])
    yield

    head_lane = lax.broadcasted_iota(jnp.int32, (n_keys, 2 * ATT_HEAD_DIM), 1) // ATT_HEAD_DIM
    for h in range(ATT_KV_HEADS):
        vs = slice(kw + (h // 2) * 2 * ATT_HEAD_DIM, kw + (h // 2 + 1) * 2 * ATT_HEAD_DIM)
        v = jnp.concatenate([kvp_ref[:, vs], kvc_ref[:, vs], kvn_ref[:, vs], kvx_ref[:, vs]], axis=0)
        v = jnp.where(head_lane == h % 2, v, jnp.ones_like(v))
        probs, sink_terms = [], []
        for _, r0 in chunks[:len(chunks) // ATT_KV_HEADS]:
            p = [jnp.exp2(s - maxes[h, r0]) for s in slabs(h, r0)]
            probs.append(jnp.concatenate(p, axis=1).astype(BF16))
            sink_terms.append(jnp.exp2(sinks[h, r0] - maxes[h, r0]))
        pv = _dot(jnp.concatenate(probs, axis=0), v)
        own, other = (h % 2) * ATT_HEAD_DIM, (1 - h % 2) * ATT_HEAD_DIM
        denom = pv[:, other:other + 1] + jnp.concatenate(sink_terms, axis=0)
        o = pv[:, own:own + ATT_HEAD_DIM] * (1.0 / denom)
        for g in range(ATT_GROUP):
            lo = (h * ATT_GROUP + g) * ATT_HEAD_DIM
            o_ref[:, lo:lo + ATT_HEAD_DIM] = o[g * blk:(g + 1) * blk].astype(BF16)


def _swa_core(q, kv, sink, n_ctx_rows):
    bsz, tt, qw = q.shape
    blk, step = ATT_BLOCK, ATT_BLOCK * ATT_STEP_BLOCKS
    n_all, n_ctx = tt // blk, n_ctx_rows // blk
    kvw = kv.shape[-1]
    kv_spec = lambda idx: pl.BlockSpec((None, blk, kvw), idx)
    return pl.pallas_call(
        functools.partial(_swa_core_kernel, n_ctx=n_ctx, n_all=n_all),
        out_shape=jax.ShapeDtypeStruct((bsz, tt, qw), BF16),
        grid=(bsz, tt // step),
        in_specs=[
            pl.BlockSpec(memory_space=pltpu.SMEM),
            pl.BlockSpec((None, step, qw), lambda b, t: (b, t, 0)),
            kv_spec(lambda b, t: (b, jnp.maximum(t * ATT_STEP_BLOCKS - 1, 0), 0)),
            pl.BlockSpec((None, step, kvw), lambda b, t: (b, t, 0)),
            kv_spec(lambda b, t: (b, jnp.minimum((t + 1) * ATT_STEP_BLOCKS, n_all - 1), 0)),
            pl.BlockSpec((None, n_ctx_rows, kvw), lambda b, t: (b, 0, 0)),
        ],
        out_specs=pl.BlockSpec((None, step, qw), lambda b, t: (b, t, 0)),
        compiler_params=_cparams("parallel", "parallel"),
        name="swa_core",
    )(sink, q, kv, kv, kv, kv)


def _segment_permutation(n):
    r = jnp.arange(n)
    p = (r[None, :] == ((r % 8) * (n // 8) + r // 8)[:, None]).astype(BF16)
    return p, p.T


def _rnn_pre_part(x, mod_ref, g_ref, w_ref, u_ref, z_ref):
    h = _prenorm(x, g_ref[...], mod_ref[...]).astype(BF16)
    rw = u_ref.shape[-1]
    u_ref[...] = _dot(h, w_ref[:, :rw]).astype(BF16)
    z_ref[...] = _dot(h, w_ref[:, rw:]).astype(BF16)


def _shift_rows(cur, other, up):
    sub = lax.broadcasted_iota(jnp.int32, cur.shape, 0)
    if up:
        return jnp.where(sub == 7, pltpu.roll(other, 7, 0), pltpu.roll(cur, 7, 0))
    return jnp.where(sub == 0, pltpu.roll(other, 1, 0), pltpu.roll(cur, 1, 0))


def _segment_scan(a, x, carry, reverse):
    n_g = a.shape[0] // 8
    order = range(n_g - 1, -1, -1) if reverse else range(n_g)
    hs, ps = [None] * n_g, [None] * n_g
    h = p = None
    for g in order:
        ag, xg = a[8 * g:8 * g + 8], x[8 * g:8 * g + 8]
        h = xg if h is None else ag * h + xg
        p = ag if p is None else ag * p
        hs[g], ps[g] = h, p
    sub = lax.broadcasted_iota(jnp.int32, h.shape, 0)
    for s in (1, 2, 4):
        shift = 8 - s if reverse else s
        inside = (sub < 8 - s) if reverse else (sub >= s)
        h = h + p * jnp.where(inside, pltpu.roll(h, shift, 0), 0.0)
        p = p * jnp.where(inside, pltpu.roll(p, shift, 0), 1.0)
    full = h + p * carry
    first, last = (7, 0) if reverse else (0, 7)
    seg_in = jnp.where(sub == first, carry, pltpu.roll(full, 7 if reverse else 1, 0))
    out = jnp.concatenate([hs[g] + ps[g] * seg_in for g in range(n_g)], axis=0)
    return out, jnp.broadcast_to(full[last:last + 1], full.shape)


def _rnn_core_kernel(upf_ref, ucf_ref, unf_ref, upb_ref, ucb_ref, unb_ref,
                     cw_ref, cb_ref, wg_ref, lam_ref,
                     perm_ref, unperm_ref, hf_ref, hb_ref, carry_ref, *, n_ctx, n_all):
    n = pl.program_id(1)

    @pl.when(n == 0)
    def _():
        carry_ref[...] = jnp.zeros_like(carry_ref)

    tiles = (n, _bwd_order(n, n_ctx, n_all))
    prev_ok = [jnp.where(jnp.logical_and(t != 0, t != n_ctx), 1.0, 0.0) for t in tiles]
    next_ok = [jnp.where(jnp.logical_and(t != n_ctx - 1, t != n_all - 1), 1.0, 0.0) for t in tiles]
    refs = ((upf_ref, ucf_ref, unf_ref, hf_ref), (upb_ref, ucb_ref, unb_ref, hb_ref))

    useg = [_dot(perm_ref[...], refs[d][1][...]) for d in range(2)]
    rows = useg[0].shape[0]
    bias_taps = (lax.broadcasted_iota(jnp.int32, (rows, RNN_HD), 1) < 2).astype(BF16)
    convs, pres = {}, {}
    for d, (p_ref, _, n_ref, _) in enumerate(refs):
        before = p_ref[...].astype(F32)[HALO - 8:] * prev_ok[d]
        after = n_ref[...].astype(F32)[:8] * next_ok[d]
        before2 = pltpu.roll(before, 1, 0)
        for j in range(RNN_HEADS):
            lanes = slice(j * RNN_HD, (j + 1) * RNN_HD)
            u = useg[d][:, lanes]
            ext = jnp.concatenate([
                _shift_rows(u[rows - 16:rows - 8], before2[:, lanes], up=False),
                _shift_rows(u[rows - 8:], before[:, lanes], up=False),
                u,
                _shift_rows(u[:8], after[:, lanes], up=True)], axis=0)
            conv = cb_ref[:, lanes]
            for tap in range(CONV_W):
                conv = conv + ext[8 * tap:8 * tap + rows] * cw_ref[tap:tap + 1, lanes]
            convs[d, j] = conv
            lhs = jnp.concatenate([conv.astype(BF16), bias_taps], axis=1)
            pres[d, j] = _dot(lhs, wg_ref[d, j])

    half = RNN_HEADS // 2
    for j0 in (0, half):
        for d in range(2):
            outs = []
            for j in range(j0, j0 + half):
                pre, conv = pres[d, j], convs[d, j]
                u = jnp.tanh(pre[:, :RNN_HD]) + 1.0
                i = jnp.tanh(pre[:, RNN_HD:]) + 1.0
                rate = (0.5 * LRU_C) * _softplus(-lam_ref[d, j])
                a = jnp.exp2((-LOG2_E * rate) * u)
                y = 1.0 - a * a
                x = jnp.where(y > 0.0, y * lax.rsqrt(y), 0.0) * i * conv
                hs, carry = _segment_scan(a, x, carry_ref[d, j], reverse=(d == 1))
                carry_ref[d, j] = carry
                outs.append(hs.astype(BF16))
            h = _dot(unperm_ref[...], jnp.concatenate(outs, axis=1))
            refs[d][3][:, j0 * RNN_HD:(j0 + half) * RNN_HD] = h.astype(BF16)


def _rnn_core(u, conv_w, conv_b, wg, lam, n_ctx_rows):
    bsz, tt, rw = u.shape
    tile = RNN_TILE
    n_all, n_ctx = tt // tile, n_ctx_rows // tile
    per = tile // HALO
    n_halo = tt // HALO
    fwd = lambda n: n
    bwd = lambda n: _bwd_order(n, n_ctx, n_all)

    def specs(order):
        return [
            pl.BlockSpec((None, HALO, rw), lambda b, n: (b, jnp.maximum(order(n) * per - 1, 0), 0)),
            pl.BlockSpec((None, tile, rw), lambda b, n: (b, order(n), 0)),
            pl.BlockSpec((None, HALO, rw), lambda b, n: (b, jnp.minimum((order(n) + 1) * per, n_halo - 1), 0)),
        ]

    out_spec = lambda order: pl.BlockSpec((None, tile, rw), lambda b, n: (b, order(n), 0))
    perm, unperm = _segment_permutation(tile)
    consts = (conv_w, conv_b, wg, lam, perm, unperm)
    return pl.pallas_call(
        functools.partial(_rnn_core_kernel, n_ctx=n_ctx, n_all=n_all),
        out_shape=[jax.ShapeDtypeStruct((bsz, tt, rw), BF16)] * 2,
        grid=(bsz, n_all),
        in_specs=specs(fwd) + specs(bwd) + [_full_spec(c.shape) for c in consts],
        out_specs=[out_spec(fwd), out_spec(bwd)],
        scratch_shapes=[pltpu.VMEM((2, RNN_HEADS, 8, RNN_HD), F32)],
        compiler_params=_cparams("arbitrary", "arbitrary"),
        name="rglru_core",
    )(u, u, u, u, u, u, *consts)


def _residual(u, x, mod_ref, gp_ref, w_ref):
    y = _dot(u.astype(BF16), w_ref[...])
    return x + _rms(y) * (mod_ref[2:3] * gp_ref[...])


def _gla_post_part(x, mod_ref, gp_ref, w_ref, of_ref, ob_ref, z_ref, gh_ref, rows=slice(None)):
    o = of_ref[rows].astype(F32) + ob_ref[rows].astype(F32)
    hv = o.shape[-1] // GLA_HEADS
    o = jnp.concatenate([_rms(o[:, h * hv:(h + 1) * hv]) for h in range(GLA_HEADS)], axis=1)
    u = o * gh_ref[...] * _silu(z_ref[rows].astype(F32))
    return _residual(u, x, mod_ref, gp_ref, w_ref)


def _swa_post_part(x, mod_ref, gp_ref, w_ref, a_ref, z_ref, rows=slice(None)):
    u = a_ref[rows].astype(F32) * _silu(z_ref[rows].astype(F32))
    return _residual(u, x, mod_ref, gp_ref, w_ref)


def _rnn_post_part(x, mod_ref, gp_ref, w_ref, hf_ref, hb_ref, z_ref, rows=slice(None)):
    u = (hf_ref[rows].astype(F32) + hb_ref[rows].astype(F32)) * _silu(z_ref[rows].astype(F32))
    return _residual(u, x, mod_ref, gp_ref, w_ref)


class _Pre(NamedTuple):
    name: str
    part: Callable
    mod: jax.Array
    g_pre: jax.Array
    consts: tuple
    row_consts: tuple
    widths: tuple
    dtypes: tuple
    stats: int


class _Post(NamedTuple):
    name: str
    part: Callable
    mod: jax.Array
    g_post: jax.Array
    w_out: jax.Array
    acts: tuple
    vecs: tuple


def _pre_kernel(*refs, part, n_x, n_ctx):
    part(_token_tile(refs[:n_x], n_ctx), *refs[n_x:])


def _post_halves(part, x, refs):
    half = x.shape[0] // 2
    return jnp.concatenate([part(x[rows], *refs, rows=rows)
                            for rows in (slice(0, half), slice(half, 2 * half))], axis=0)


def _post_kernel(*refs, part, n_x, n_ctx):
    refs[-1][...] = _post_halves(part, _token_tile(refs[:n_x], n_ctx), refs[n_x:-1])


def _mid_kernel(*refs, post_part, n_post, pre_part, n_pre, n_x, n_ctx):
    post_in, pre_in = refs[n_x:n_x + n_post], refs[n_x + n_post:n_x + n_post + n_pre]
    xo_ref, pre_out = refs[n_x + n_post + n_pre], refs[n_x + n_post + n_pre + 1:]
    x = _post_halves(post_part, _token_tile(refs[:n_x], n_ctx), post_in)
    xo_ref[...] = x
    pre_part(x, *pre_in, *pre_out)


def _stream_shape(xs):
    if isinstance(xs, tuple):
        bsz, n_ctx_rows, d = xs[0].shape
        return bsz, n_ctx_rows + xs[1].shape[1], d
    return xs.shape


def _pre_io(pre, bsz, tt, d, n_ctx):
    operands = (pre.mod, pre.g_pre, *pre.consts, *pre.row_consts)
    in_specs = ([_mod_spec(d, n_ctx, bsz), _full_spec(pre.g_pre.shape)]
                + [_full_spec(c.shape) for c in pre.consts]
                + [pl.BlockSpec((ROW_TILE, r.shape[-1]), lambda b, t: (t, 0)) for r in pre.row_consts])
    out_shape = ([jax.ShapeDtypeStruct((bsz, tt, w), dt) for w, dt in zip(pre.widths, pre.dtypes)]
                 + [jax.ShapeDtypeStruct((bsz, tt // ROW_TILE, 8, 128), F32)] * pre.stats)
    out_specs = ([_row_spec(w) for w in pre.widths]
                 + [pl.BlockSpec((None, None, 8, 128), lambda b, t: (b, t, 0, 0))] * pre.stats)
    return operands, in_specs, out_shape, out_specs


def _post_io(post, bsz, d, n_ctx, off):
    operands = (post.mod, post.g_post, post.w_out, *post.acts, *post.vecs)
    in_specs = ([_mod_spec(d, n_ctx, bsz, off), _full_spec(post.g_post.shape), _full_spec(post.w_out.shape)]
                + [_row_spec(a.shape[-1], off) for a in post.acts]
                + [_full_spec(v.shape) for v in post.vecs])
    return operands, in_specs


def _pre_call(pre, xs, n_ctx_rows):
    bsz, tt, d = _stream_shape(xs)
    n_ctx = n_ctx_rows // ROW_TILE
    streams = xs if isinstance(xs, tuple) else (xs,)
    operands, in_specs, out_shape, out_specs = _pre_io(pre, bsz, tt, d, n_ctx)
    return pl.pallas_call(
        functools.partial(_pre_kernel, part=pre.part, n_x=len(streams), n_ctx=n_ctx),
        out_shape=out_shape,
        grid=(bsz, tt // ROW_TILE),
        in_specs=_token_specs(xs, n_ctx) + in_specs,
        out_specs=out_specs,
        compiler_params=_cparams("parallel", "parallel"),
        name=pre.name,
    )(*streams, *operands)


def _post_call(post, xs, n_ctx_rows, latent_only):
    bsz, tt, d = _stream_shape(xs)
    n_ctx = n_ctx_rows // ROW_TILE
    off = n_ctx if latent_only else 0
    n_tiles = tt // ROW_TILE - off
    streams = xs if isinstance(xs, tuple) else (xs,)
    operands, in_specs = _post_io(post, bsz, d, n_ctx, off)
    return pl.pallas_call(
        functools.partial(_post_kernel, part=post.part, n_x=len(streams), n_ctx=n_ctx - off),
        out_shape=jax.ShapeDtypeStruct((bsz, n_tiles * ROW_TILE, d), F32),
        grid=(bsz, n_tiles),
        in_specs=_token_specs(xs, n_ctx, off) + in_specs,
        out_specs=_row_spec(d),
        compiler_params=_cparams("parallel", "parallel"),
        name=post.name,
    )(*streams, *operands)


def _final_call(post, xs, n_ctx_rows):
    bsz, tt, d = xs.shape
    rows = lambda w: pl.BlockSpec((pl.Element(1), pl.Element(FINAL_TILE), pl.Element(w)),
                                  lambda b, t: (b, pl.multiple_of(n_ctx_rows + t * FINAL_TILE, 8), 0))
    operands = (post.mod, post.g_post, post.w_out, *post.acts, *post.vecs)
    in_specs = ([rows(d), pl.BlockSpec((None, 3, d), lambda b, t: (b, 0, 0)),
                 _full_spec(post.g_post.shape), _full_spec(post.w_out.shape)]
                + [rows(a.shape[-1]) for a in post.acts]
                + [_full_spec(v.shape) for v in post.vecs])

    def body(x_ref, mod_ref, gp_ref, w_ref, *rest):
        acts = [r.at[0] for r in rest[:len(post.acts)]]
        vecs, o_ref = rest[len(post.acts):-1], rest[-1]
        o_ref[...] = _post_halves(post.part, x_ref[0], (mod_ref, gp_ref, w_ref, *acts, *vecs))

    return pl.pallas_call(
        body,
        out_shape=jax.ShapeDtypeStruct((bsz, tt - n_ctx_rows, d), F32),
        grid=(bsz, (tt - n_ctx_rows) // FINAL_TILE),
        in_specs=in_specs,
        out_specs=pl.BlockSpec((None, FINAL_TILE, d), lambda b, t: (b, t, 0)),
        compiler_params=_cparams("parallel", "parallel"),
        name=post.name,
    )(xs, *operands)


def _mid_call(post, pre, xs, n_ctx_rows):
    bsz, tt, d = _stream_shape(xs)
    n_ctx = n_ctx_rows // ROW_TILE
    streams = xs if isinstance(xs, tuple) else (xs,)
    post_ops, post_specs = _post_io(post, bsz, d, n_ctx, 0)
    pre_ops, pre_specs, pre_shape, pre_out_specs = _pre_io(pre, bsz, tt, d, n_ctx)
    outs = pl.pallas_call(
        functools.partial(_mid_kernel, post_part=post.part, n_post=len(post_ops),
                          pre_part=pre.part, n_pre=len(pre_ops), n_x=len(streams), n_ctx=n_ctx),
        out_shape=[jax.ShapeDtypeStruct((bsz, tt, d), F32)] + pre_shape,
        grid=(bsz, tt // ROW_TILE),
        in_specs=_token_specs(xs, n_ctx) + post_specs + pre_specs,
        out_specs=[_row_spec(d)] + pre_out_specs,
        compiler_params=_cparams("parallel", "parallel"),
        name=post.name + "_" + pre.name,
    )(*streams, *post_ops, *pre_ops)
    return outs[0], outs[1:]


def _gla_layer(mod, g_pre, g_post, w_in, w_g1, w_g2, b_g, g_head, w_out, n_ctx_rows):
    d = w_in.shape[0]
    rank, dk = w_g2.shape[1], w_g2.shape[2]
    dv = w_out.shape[0]
    wg1 = jnp.zeros((d, GLA_GATE_PAD), F32).at[:, :2 * rank].set(jnp.concatenate([w_g1[0], w_g1[1]], axis=1))
    wg2 = jnp.zeros((GLA_GATE_PAD, 2 * dk), F32)
    wg2 = wg2.at[:rank, :dk].set(w_g2[0]).at[rank:2 * rank, dk:].set(w_g2[1])
    pre = _Pre("gla_pre", functools.partial(_gla_pre_part, dk=dk), mod, g_pre,
               (w_in.astype(BF16), wg1.astype(BF16), wg2.astype(BF16), b_g.reshape(1, 2 * dk)), (),
               (2 * dk, dv, dv, 2 * dk), (BF16, BF16, BF16, BF16), 1)

    def mixer(outs):
        qk, v, z, la, peak = outs
        exact = (peak[:, :, 0, 0] * (GLA_BLOCK // 2) > GLA_SAFE_EXPONENT).astype(jnp.int32)
        o_f, o_b = _gla_core(qk, v, la, exact, n_ctx_rows)
        return _Post("gla_post", _gla_post_part, mod, g_post, w_out.astype(BF16), (o_f, o_b, z),
                     (g_head.reshape(1, dv),))

    return pre, mixer


def _swa_layer(mod, g_pre, g_post, w_in, sink, w_out, rope, n_ctx_rows):
    qw = w_out.shape[0]
    kw = (w_in.shape[1] - 2 * qw) // 2
    pre = _Pre("swa_pre", _swa_pre_part, mod, g_pre, (w_in.astype(BF16),), rope,
               (qw, 2 * kw, qw), (BF16, BF16, BF16), 0)

    def mixer(outs):
        q, kv, z = outs
        a = _swa_core(q, kv, sink, n_ctx_rows)
        return _Post("swa_post", _swa_post_part, mod, g_post, w_out.astype(BF16), (a, z), ())

    return pre, mixer


def _rnn_layer(mod, g_pre, g_post, w_in, conv_w, conv_b, w_ra, b_ra, w_ri, b_ri, lam, w_out, n_ctx_rows):
    rw = w_out.shape[0]
    pre = _Pre("rglru_pre", _rnn_pre_part, mod, g_pre, (w_in.astype(BF16),), (), (rw, rw), (BF16, BF16), 0)
    vec = lambda a: a.reshape(2, RNN_HEADS, 1, RNN_HD)
    bias = 0.5 * jnp.concatenate([vec(b_ra), vec(b_ri)], axis=-1)
    bias_hi = bias.astype(BF16)
    bias_lo = (bias - bias_hi.astype(F32)).astype(BF16)
    wg = jnp.concatenate([jnp.concatenate([w_ra, w_ri], axis=-1).astype(BF16), bias_hi, bias_lo,
                          jnp.zeros((2, RNN_HEADS, RNN_HD - 2, 2 * RNN_HD), BF16)], axis=2)

    def mixer(outs):
        u, z = outs
        h_f, h_b = _rnn_core(u, 0.5 * conv_w, 0.5 * conv_b.reshape(1, rw), wg, vec(lam), n_ctx_rows)
        return _Post("rglru_post", _rnn_post_part, mod, g_post, w_out.astype(BF16), (h_f, h_b, z), ())

    return pre, mixer


def _rope_tables(t, n_ctx_rows):
    pairs = ATT_HEAD_DIM // 4
    pos = jnp.arange(t, dtype=jnp.int32)
    row = (pos // GRID_W).astype(F32)
    col = (pos % GRID_W).astype(F32)
    freqs = ROPE_BASE ** (-jnp.arange(pairs, dtype=F32) / pairs)
    ang_r, ang_c = row[:, None] * freqs, col[:, None] * freqs
    cos = jnp.concatenate([jnp.cos(ang_r)] * 2 + [jnp.cos(ang_c)] * 2, axis=1)
    sin = jnp.concatenate([-jnp.sin(ang_r), jnp.sin(ang_r), -jnp.sin(ang_c), jnp.sin(ang_c)], axis=1)
    cos = jnp.concatenate([jnp.ones((n_ctx_rows, ATT_HEAD_DIM), F32), cos], axis=0)
    sin = jnp.concatenate([jnp.zeros((n_ctx_rows, ATT_HEAD_DIM), F32), sin], axis=0)
    return jnp.tile(cos, (1, 2)), jnp.tile(sin, (1, 2))


def kernel(x, c, ctx, c_ctx, w_mod, b_mod, g_pre, g_post, a_w_in, a_w_g1, a_w_g2, a_b_g, a_g_head, a_w_out, b_w_in, b_sink, b_w_out, c_w_in, c_conv_w, c_conv_b, c_w_ra, c_b_ra, c_w_ri, c_b_ri, c_lam, c_w_out):
    bsz, t, d = x.shape
    n_ctx_rows = ctx.shape[1]
    depth = w_mod.shape[0]
    assert bsz < MOD_ROWS and n_ctx_rows % ROW_TILE == 0 and t % ROW_TILE == 0

    cvec = jnp.zeros((MOD_ROWS, d), F32).at[:bsz].set(c).at[bsz].set(c_ctx)
    mods = _modulation(cvec, w_mod, b_mod)
    rope = _rope_tables(t, n_ctx_rows)
    layers = []
    for i in range(depth):
        kind, j = i % N_MIXERS, i // N_MIXERS
        gpre, gpost = g_pre[i].reshape(1, d), g_post[i].reshape(1, d)
        if kind == 0:
            layers.append(_gla_layer(mods[i], gpre, gpost, a_w_in[j], a_w_g1[j], a_w_g2[j], a_b_g[j],
                                     a_g_head[j], a_w_out[j], n_ctx_rows))
        elif kind == 1:
            layers.append(_swa_layer(mods[i], gpre, gpost, b_w_in[j], b_sink[j], b_w_out[j], rope, n_ctx_rows))
        else:
            layers.append(_rnn_layer(mods[i], gpre, gpost, c_w_in[j], c_conv_w[j], c_conv_b[j], c_w_ra[j],
                                     c_b_ra[j], c_w_ri[j], c_b_ri[j], c_lam[j], c_w_out[j], n_ctx_rows))

    xs = (ctx, x)
    outs = _pre_call(layers[0][0], xs, n_ctx_rows)
    for i in range(depth - 1):
        xs, outs = _mid_call(layers[i][1](outs), layers[i + 1][0], xs, n_ctx_rows)
    if isinstance(xs, tuple) or t % FINAL_TILE:
        return _post_call(layers[-1][1](outs), xs, n_ctx_rows, latent_only=True)
    return _final_call(layers[-1][1](outs), xs, n_ctx_rows)
```

```python
import functools
from typing import Callable, NamedTuple

import jax
import jax.numpy as jnp
from jax import lax
from jax.experimental import pallas as pl
from jax.experimental.pallas import tpu as pltpu

F32 = jnp.float32
BF16 = jnp.bfloat16

EPS = 1e-6
GRID_W = 64
N_MIXERS = 3

GLA_HEADS = 4
GLA_TAU = 16.0
GLA_GATE_PAD = 128

ATT_HEAD_DIM = 64
ATT_KV_HEADS = 4
ATT_GROUP = 4
ATT_BLOCK = 128
ATT_ROWS = 32
ATT_STEP_BLOCKS = 2
ROPE_BASE = 10000.0
LOG2_E = 1.4426950408889634

RNN_HEADS = 10
RNN_HD = 128
RNN_GROUP = 2
CONV_W = 4
LRU_C = 8.0

ROW_TILE = 256
FINAL_TILE = 1024
GLA_TILE = 256
GLA_CHUNK = 128
GLA_BLOCK = 128
GLA_SAFE_EXPONENT = 60.0
RNN_TILE = 256
HALO = 16
MOD_ROWS = 8
VMEM_LIMIT = 48 * 1024 * 1024


def _cparams(*sem):
    return pltpu.CompilerParams(dimension_semantics=sem, vmem_limit_bytes=VMEM_LIMIT)


def _silu(x):
    half = 0.5 * x
    return half * (jnp.tanh(half) + 1.0)


def _softplus(x):
    return jnp.maximum(x, 0.0) + jnp.log1p(jnp.exp(-jnp.abs(x)))


def _log_sigmoid(x):
    return jnp.minimum(x, 0.0) - jnp.log(1.0 + jnp.exp(-jnp.abs(x)))


def _split3(a):
    hi = a.astype(BF16)
    r = a - hi.astype(F32)
    mid = r.astype(BF16)
    lo = (r - mid.astype(F32)).astype(BF16)
    return hi, mid, lo


def _dot(a, b):
    return jnp.dot(a, b, preferred_element_type=F32)


def _dot_nt(a, b):
    return lax.dot_general(a, b, (((1,), (1,)), ((), ())), preferred_element_type=F32)


def _dot_tn(a, b):
    return lax.dot_general(a, b, (((0,), (0,)), ((), ())), preferred_element_type=F32)


def _rms(x):
    return x * lax.rsqrt(jnp.mean(x * x, axis=-1, keepdims=True) + EPS)


def _mod_kernel(c_ref, w_ref, b_ref, o_ref):
    s = _silu(c_ref[...])
    s_hi, s_mid, s_lo = _split3(s)
    w_hi, w_mid, w_lo = _split3(w_ref[...])
    acc = _dot(s_hi, w_hi)
    acc += _dot(s_hi, w_mid) + _dot(s_mid, w_hi)
    acc += _dot(s_hi, w_lo) + _dot(s_mid, w_mid) + _dot(s_lo, w_hi)
    o_ref[...] = acc + b_ref[...]


def _modulation(cvec, w_mod, b_mod):
    depth, d, d3 = w_mod.shape
    out = pl.pallas_call(
        _mod_kernel,
        out_shape=jax.ShapeDtypeStruct((depth, MOD_ROWS, d3), F32),
        grid=(depth, d3 // d),
        in_specs=[
            pl.BlockSpec((MOD_ROWS, d), lambda i, j: (0, 0)),
            pl.BlockSpec((None, d, d), lambda i, j: (i, 0, j)),
            pl.BlockSpec((None, 1, d), lambda i, j: (i, 0, j)),
        ],
        out_specs=pl.BlockSpec((None, MOD_ROWS, d), lambda i, j: (i, 0, j)),
        compiler_params=_cparams("parallel", "parallel"),
        name="modulation",
    )(cvec, w_mod, b_mod.reshape(depth, 1, d3))
    return out.reshape(depth, MOD_ROWS, 3, d)


def _prenorm(x, g, mod):
    return _rms(x) * (g * (1.0 + mod[1:2])) + mod[0:1]


def _mod_spec(d, n_ctx_tiles, ctx_row, offset=0):
    def index(b, t):
        return (jnp.where(t + offset < n_ctx_tiles, ctx_row, b), 0, 0)
    return pl.BlockSpec((None, 3, d), index)


def _row_spec(width, offset=0):
    return pl.BlockSpec((None, ROW_TILE, width), lambda b, t: (b, t + offset, 0))


def _token_specs(xs, n_ctx, offset=0):
    if not isinstance(xs, tuple):
        return [_row_spec(xs.shape[-1], offset)]
    d = xs[0].shape[-1]
    return [pl.BlockSpec((None, ROW_TILE, d), lambda b, t: (b, jnp.minimum(t + offset, n_ctx - 1), 0)),
            pl.BlockSpec((None, ROW_TILE, d), lambda b, t: (b, jnp.maximum(t + offset - n_ctx, 0), 0))]


def _token_tile(x_refs, n_ctx_tiles):
    if len(x_refs) == 1:
        return x_refs[0][...]
    return jnp.where(pl.program_id(1) < n_ctx_tiles, x_refs[0][...], x_refs[1][...])


def _full_spec(shape):
    zeros = (0,) * len(shape)
    return pl.BlockSpec(shape, lambda b, t: zeros)


def _gla_pre_part(x, mod_ref, g_ref, w_ref, wg1_ref, wg2_ref, bg_ref, qk_ref, v_ref, z_ref, la_ref, peak_ref,
                  *, dk):
    h = _prenorm(x, g_ref[...], mod_ref[...]).astype(BF16)
    hk = dk // GLA_HEADS
    dv = v_ref.shape[-1]
    g1 = _dot(h, wg1_ref[...]).astype(BF16)
    gate = _dot(g1, wg2_ref[...]) + bg_ref[...]
    la = _log_sigmoid(gate) / GLA_TAU
    la_ref[...] = la.astype(BF16)
    peak = jnp.max(jnp.max(-la, axis=1, keepdims=True), axis=0, keepdims=True)
    peak_ref[...] = jnp.broadcast_to(peak, peak_ref.shape)
    q = _dot(h, w_ref[:, :dk]) * hk ** -0.5
    qk_ref[:, :dk] = q.astype(BF16)
    qk_ref[:, dk:] = _dot(h, w_ref[:, dk:2 * dk]).astype(BF16)
    v_ref[...] = _dot(h, w_ref[:, 2 * dk:2 * dk + dv]).astype(BF16)
    z_ref[...] = _dot(h, w_ref[:, 2 * dk + dv:]).astype(BF16)


def _gla_core_kernel(exact_ref, qkf_ref, vf_ref, laf_ref, qkb_ref, vb_ref, lab_ref,
                     of_ref, ob_ref, st_ref, row_ref, *, dk, n_ctx, n_all):
    b, n = pl.program_id(0), pl.program_id(1)

    @pl.when(n == 0)
    def _():
        st_ref[...] = jnp.zeros_like(st_ref)

    io = ((qkf_ref, vf_ref, laf_ref, of_ref), (qkb_ref, vb_ref, lab_ref, ob_ref))
    n_sub = qkf_ref.shape[0] // GLA_CHUNK

    def run(exact):
        chunks = [_gla_chunk_pair(io, (s * GLA_CHUNK, (n_sub - 1 - s) * GLA_CHUNK), st_ref, dk,
                                  row_ref if exact else None) for s in range(n_sub)]
        for _ in range(3):
            for chunk in chunks:
                next(chunk, None)

    out_of_range = exact_ref[b, n] + exact_ref[b, _bwd_order(n, n_ctx, n_all)]
    pl.when(out_of_range == 0)(lambda: run(False))
    pl.when(out_of_range != 0)(lambda: run(True))


def _gla_exact_intra(d, q, k, v, b, row_ref):
    c, hk = q.shape
    row_ref[:, :hk] = k
    row_ref[:, hk:2 * hk] = b
    row_ref[:, 2 * hk:] = v.astype(F32)
    t_idx = lax.broadcasted_iota(jnp.int32, (c, 1), 0)

    def keys(g, acc):
        group = row_ref[pl.ds(pl.multiple_of(g * 8, 8), 8), :]
        for r in range(8):
            s = g * 8 + r
            k_s, b_s, v_s = group[r:r + 1, :hk], group[r:r + 1, hk:2 * hk], group[r:r + 1, 2 * hk:]
            decay = jnp.exp(jnp.minimum(b - b_s, 0.0))
            col = jnp.sum(q * decay * k_s, axis=1, keepdims=True)
            causal = (t_idx >= s) if d == 0 else (t_idx <= s)
            acc = acc + jnp.where(causal, col, 0.0) * v_s
        return acc

    return lax.fori_loop(0, c // 8, keys, jnp.zeros((c, v.shape[1]), F32))


def _gla_chunk_pair(io, base, st_ref, dk, row_ref):
    c = GLA_CHUNK
    hk = dk // GLA_HEADS
    hv = io[0][1].shape[-1] // GLA_HEADS
    row = lax.broadcasted_iota(jnp.int32, (c, c), 0)
    col = lax.broadcasted_iota(jnp.int32, (c, c), 1)
    blk = GLA_BLOCK
    keeps = (col <= row, col >= row)
    lasts = (c - 1, 0)
    tile = [slice(r0, r0 + c) for r0 in base]
    cums = []
    for d, (_, _, la_ref, _) in enumerate(io):
        tri = jnp.where(keeps[d], 1.0, 0.0).astype(BF16)
        cums.append(_dot(tri, la_ref[tile[d], d * dk:(d + 1) * dk]))
    yield

    stage = {}
    for d, (qk_ref, v_ref, _, _) in enumerate(io):
        for h in range(GLA_HEADS):
            b = cums[d][:, h * hk:(h + 1) * hk]
            b_last = b[lasts[d]:lasts[d] + 1]
            q = qk_ref[tile[d], h * hk:(h + 1) * hk].astype(F32)
            k = qk_ref[tile[d], dk + h * hk:dk + (h + 1) * hk].astype(F32)
            v = v_ref[tile[d], h * hv:(h + 1) * hv]
            scores = []
            for i in (range(c // blk) if row_ref is None else ()):
                rows = slice(i * blk, (i + 1) * blk)
                keys = slice(0, (i + 1) * blk) if d == 0 else slice(i * blk, c)
                b_ref = b[i * blk + blk // 2:i * blk + blk // 2 + 1]
                qe = (q[rows] * jnp.exp(b[rows] - b_ref)).astype(BF16)
                ke = (k[keys] * jnp.exp(b_ref - b[keys])).astype(BF16)
                scores.append((rows, keys, _dot_nt(qe, ke)))
            qs = (q * jnp.exp(b)).astype(BF16)
            kd = (k * jnp.exp(b_last - b)).astype(BF16)
            st = st_ref[d, h]
            inter = _dot_nt(qs, st.astype(BF16))
            st_ref[d, h] = st * jnp.exp(b_last) + _dot_tn(v, kd)
            stage[d, h] = (scores, inter, v, (q, k, b))
    yield

    for d, (_, _, _, o_ref) in enumerate(io):
        for h in range(GLA_HEADS):
            scores, inter, v, (q, k, b) = stage[d, h]
            if row_ref is not None:
                o = _gla_exact_intra(d, q, k, v, b, row_ref) + inter
                o_ref[tile[d], h * hv:(h + 1) * hv] = o.astype(BF16)
            for rows, keys, s in scores:
                t_idx = lax.broadcasted_iota(jnp.int32, s.shape, 0) + rows.start
                s_idx = lax.broadcasted_iota(jnp.int32, s.shape, 1) + keys.start
                p = jnp.where((s_idx <= t_idx) if d == 0 else (s_idx >= t_idx), s, 0.0).astype(BF16)
                o = _dot(p, v[keys]) + inter[rows]
                o_ref[base[d] + rows.start:base[d] + rows.stop, h * hv:(h + 1) * hv] = o.astype(BF16)


def _bwd_order(n, n_ctx, n_all):
    return jnp.where(n < n_ctx, n_ctx - 1 - n, n_all - 1 - (n - n_ctx))


def _gla_core(qk, v, la, exact, n_ctx_rows):
    bsz, tt, dv = v.shape
    dk = qk.shape[-1] // 2
    c = GLA_TILE
    n_all, n_ctx = tt // c, n_ctx_rows // c
    hk, hv = dk // GLA_HEADS, dv // GLA_HEADS
    fwd = lambda b, n: (b, n, 0)
    bwd = lambda b, n: (b, _bwd_order(n, n_ctx, n_all), 0)
    spec = lambda w, idx: pl.BlockSpec((None, c, w), idx)
    return pl.pallas_call(
        functools.partial(_gla_core_kernel, dk=dk, n_ctx=n_ctx, n_all=n_all),
        out_shape=[jax.ShapeDtypeStruct((bsz, tt, dv), BF16)] * 2,
        grid=(bsz, n_all),
        in_specs=[pl.BlockSpec(memory_space=pltpu.SMEM),
                  spec(2 * dk, fwd), spec(dv, fwd), spec(2 * dk, fwd),
                  spec(2 * dk, bwd), spec(dv, bwd), spec(2 * dk, bwd)],
        out_specs=[spec(dv, fwd), spec(dv, bwd)],
        scratch_shapes=[pltpu.VMEM((2, GLA_HEADS, hv, hk), F32),
                        pltpu.VMEM((GLA_CHUNK, 2 * hk + hv), F32)],
        compiler_params=_cparams("arbitrary", "arbitrary"),
        name="gla_core",
    )(exact, qk, v, la, qk, v, la)


def _rope(x, cos, sin_signed):
    lane = lax.broadcasted_iota(jnp.int32, (x.shape[0], 128), 1)
    first = (lane % 32) < 16
    out = []
    for j in range(x.shape[1] // 128):
        xj = x[:, j * 128:(j + 1) * 128]
        partner = jnp.where(first, pltpu.roll(xj, 112, 1), pltpu.roll(xj, 16, 1))
        out.append(xj * cos + partner * sin_signed)
    return jnp.concatenate(out, axis=1)


def _swa_pre_part(x, mod_ref, g_ref, w_ref, cos_ref, sin_ref, q_ref, kv_ref, z_ref):
    h = _prenorm(x, g_ref[...], mod_ref[...]).astype(BF16)
    qw = q_ref.shape[-1]
    kw = kv_ref.shape[-1] // 2
    cos, sin = cos_ref[...], sin_ref[...]
    q = _rope(_dot(h, w_ref[:, :qw]), cos, sin) * (ATT_HEAD_DIM ** -0.5 * LOG2_E)
    q_ref[...] = q.astype(BF16)
    kv_ref[:, :kw] = _rope(_dot(h, w_ref[:, qw:qw + kw]), cos, sin).astype(BF16)
    kv_ref[:, kw:] = _dot(h, w_ref[:, qw + kw:qw + 2 * kw]).astype(BF16)
    z_ref[...] = _dot(h, w_ref[:, qw + 2 * kw:]).astype(BF16)


def _swa_core_kernel(sink_ref, q_ref, kvp_ref, kvc_ref, kvn_ref, kvx_ref, o_ref, *, n_ctx, n_all):
    blk = ATT_BLOCK
    own = [kvc_ref.at[i * blk:(i + 1) * blk] for i in range(ATT_STEP_BLOCKS)]
    window = [kvp_ref] + own + [kvn_ref]
    blocks = [_swa_block(sink_ref, q_ref.at[i * blk:(i + 1) * blk], window[i:i + 3], kvx_ref,
                         o_ref.at[i * blk:(i + 1) * blk], pl.program_id(1) * ATT_STEP_BLOCKS + i, n_ctx, n_all)
              for i in range(ATT_STEP_BLOCKS)]
    for _ in range(3):
        for block in blocks:
            next(block, None)


def _swa_block(sink_ref, q_ref, kv_refs, kvx_ref, o_ref, t, n_ctx, n_all):
    blk = ATT_BLOCK
    kvp_ref, kvc_ref, kvn_ref = kv_refs
    n_keys = 3 * blk + kvx_ref.shape[0]
    kw = kvc_ref.shape[-1] // 2
    n_loc = 3 * blk
    i = lax.broadcasted_iota(jnp.int32, (blk, n_loc), 0)
    j = lax.broadcasted_iota(jnp.int32, (blk, n_loc), 1)
    never = n_keys
    off_prev = jnp.where(t > n_ctx, 0, never)
    off_cur = jnp.where(t >= n_ctx, 0, never)
    off_next = jnp.where(jnp.logical_and(t >= n_ctx, t < n_all - 1), 0, never)
    valid = (((j >= i + off_prev) & (j < blk))
             | ((j >= blk + off_cur) & (j < 2 * blk))
             | ((j >= 2 * blk) & (j <= i + 2 * blk - off_next)))
    bias = jnp.where(valid, 0.0, -jnp.inf)
    logits = []
    for h in range(ATT_KV_HEADS):
        ks = slice(h * ATT_HEAD_DIM, (h + 1) * ATT_HEAD_DIM)
        k = jnp.concatenate([kvp_ref[:, ks], kvc_ref[:, ks], kvn_ref[:, ks], kvx_ref[:, ks]], axis=0)
        q = jnp.concatenate(
            [q_ref[:, (h * ATT_GROUP + g) * ATT_HEAD_DIM:(h * ATT_GROUP + g + 1) * ATT_HEAD_DIM]
             for g in range(ATT_GROUP)], axis=0)
        logits.append(_dot_nt(q, k))
    yield

    chunks = [(h, r0) for h in range(ATT_KV_HEADS) for r0 in range(0, ATT_GROUP * blk, ATT_ROWS)]

    def slabs(h, r0):
        out = []
        for c0 in range(0, n_keys, 128):
            s = logits[h][r0:r0 + ATT_ROWS, c0:c0 + 128]
            if c0 < n_loc:
                s = s + bias[r0 % blk:r0 % blk + ATT_ROWS, c0:c0 + 128]
            out.append(s)
        return out

    sinks = {(h, r0): sink_ref[h * ATT_GROUP + r0 // blk] * LOG2_E for h, r0 in chunks}
    maxes = {}
    for h, r0 in chunks:
        lane_max = functools.reduce(jnp.maximum, slabs(h, r0))
        maxes[h, r0] = jnp.maximum(jnp.max(lane_max, axis=-1, keepdims=True), sinks[h, r0])
    yield

    head_lane = lax.broadcasted_iota(jnp.int32, (n_keys, 2 * ATT_HEAD_DIM), 1) // ATT_HEAD_DIM
    for h in range(ATT_KV_HEADS):
        vs = slice(kw + (h // 2) * 2 * ATT_HEAD_DIM, kw + (h // 2 + 1) * 2 * ATT_HEAD_DIM)
        v = jnp.concatenate([kvp_ref[:, vs], kvc_ref[:, vs], kvn_ref[:, vs], kvx_ref[:, vs]], axis=0)
        v = jnp.where(head_lane == h % 2, v, jnp.ones_like(v))
        probs, sink_terms = [], []
        for _, r0 in chunks[:len(chunks) // ATT_KV_HEADS]:
            p = [jnp.exp2(s - maxes[h, r0]) for s in slabs(h, r0)]
            probs.append(jnp.concatenate(p, axis=1).astype(BF16))
            sink_terms.append(jnp.exp2(sinks[h, r0] - maxes[h, r0]))
        pv = _dot(jnp.concatenate(probs, axis=0), v)
        own, other = (h % 2) * ATT_HEAD_DIM, (1 - h % 2) * ATT_HEAD_DIM
        denom = pv[:, other:other + 1] + jnp.concatenate(sink_terms, axis=0)
        o = pv[:, own:own + ATT_HEAD_DIM] * (1.0 / denom)
        for g in range(ATT_GROUP):
            lo = (h * ATT_GROUP + g) * ATT_HEAD_DIM
            o_ref[:, lo:lo + ATT_HEAD_DIM] = o[g * blk:(g + 1) * blk].astype(BF16)


def _swa_core(q, kv, sink, n_ctx_rows):
    bsz, tt, qw = q.shape
    blk, step = ATT_BLOCK, ATT_BLOCK * ATT_STEP_BLOCKS
    n_all, n_ctx = tt // blk, n_ctx_rows // blk
    kvw = kv.shape[-1]
    kv_spec = lambda idx: pl.BlockSpec((None, blk, kvw), idx)
    return pl.pallas_call(
        functools.partial(_swa_core_kernel, n_ctx=n_ctx, n_all=n_all),
        out_shape=jax.ShapeDtypeStruct((bsz, tt, qw), BF16),
        grid=(bsz, tt // step),
        in_specs=[
            pl.BlockSpec(memory_space=pltpu.SMEM),
            pl.BlockSpec((None, step, qw), lambda b, t: (b, t, 0)),
            kv_spec(lambda b, t: (b, jnp.maximum(t * ATT_STEP_BLOCKS - 1, 0), 0)),
            pl.BlockSpec((None, step, kvw), lambda b, t: (b, t, 0)),
            kv_spec(lambda b, t: (b, jnp.minimum((t + 1) * ATT_STEP_BLOCKS, n_all - 1), 0)),
            pl.BlockSpec((None, n_ctx_rows, kvw), lambda b, t: (b, 0, 0)),
        ],
        out_specs=pl.BlockSpec((None, step, qw), lambda b, t: (b, t, 0)),
        compiler_params=_cparams("parallel", "parallel"),
        name="swa_core",
    )(sink, q, kv, kv, kv, kv)


def _segment_permutation(n):
    r = jnp.arange(n)
    p = (r[None, :] == ((r % 8) * (n // 8) + r // 8)[:, None]).astype(BF16)
    return p, p.T


def _rnn_pre_part(x, mod_ref, g_ref, w_ref, u_ref, z_ref):
    h = _prenorm(x, g_ref[...], mod_ref[...]).astype(BF16)
    rw = u_ref.shape[-1]
    u_ref[...] = _dot(h, w_ref[:, :rw]).astype(BF16)
    z_ref[...] = _dot(h, w_ref[:, rw:]).astype(BF16)


def _shift_rows(cur, other, up):
    sub = lax.broadcasted_iota(jnp.int32, cur.shape, 0)
    if up:
        return jnp.where(sub == 7, pltpu.roll(other, 7, 0), pltpu.roll(cur, 7, 0))
    return jnp.where(sub == 0, pltpu.roll(other, 1, 0), pltpu.roll(cur, 1, 0))


def _segment_scan(a, x, carry, reverse):
    n_g = a.shape[0] // 8
    order = range(n_g - 1, -1, -1) if reverse else range(n_g)
    hs, ps = [None] * n_g, [None] * n_g
    h = p = None
    for g in order:
        ag, xg = a[8 * g:8 * g + 8], x[8 * g:8 * g + 8]
        h = xg if h is None else ag * h + xg
        p = ag if p is None else ag * p
        hs[g], ps[g] = h, p
    sub = lax.broadcasted_iota(jnp.int32, h.shape, 0)
    for s in (1, 2, 4):
        shift = 8 - s if reverse else s
        inside = (sub < 8 - s) if reverse else (sub >= s)
        h = h + p * jnp.where(inside, pltpu.roll(h, shift, 0), 0.0)
        p = p * jnp.where(inside, pltpu.roll(p, shift, 0), 1.0)
    full = h + p * carry
    first, last = (7, 0) if reverse else (0, 7)
    seg_in = jnp.where(sub == first, carry, pltpu.roll(full, 7 if reverse else 1, 0))
    out = jnp.concatenate([hs[g] + ps[g] * seg_in for g in range(n_g)], axis=0)
    return out, jnp.broadcast_to(full[last:last + 1], full.shape)


def _rnn_core_kernel(upf_ref, ucf_ref, unf_ref, upb_ref, ucb_ref, unb_ref,
                     cw_ref, cb_ref, wg_ref, lam_ref,
                     perm_ref, unperm_ref, hf_ref, hb_ref, carry_ref, *, n_ctx, n_all):
    n = pl.program_id(1)

    @pl.when(n == 0)
    def _():
        carry_ref[...] = jnp.zeros_like(carry_ref)

    tiles = (n, _bwd_order(n, n_ctx, n_all))
    prev_ok = [jnp.where(jnp.logical_and(t != 0, t != n_ctx), 1.0, 0.0) for t in tiles]
    next_ok = [jnp.where(jnp.logical_and(t != n_ctx - 1, t != n_all - 1), 1.0, 0.0) for t in tiles]
    refs = ((upf_ref, ucf_ref, unf_ref, hf_ref), (upb_ref, ucb_ref, unb_ref, hb_ref))

    useg = [_dot(perm_ref[...], refs[d][1][...]) for d in range(2)]
    rows = useg[0].shape[0]
    bias_taps = (lax.broadcasted_iota(jnp.int32, (rows, RNN_HD), 1) < 2).astype(BF16)
    convs, pres = {}, {}
    for d, (p_ref, _, n_ref, _) in enumerate(refs):
        before = p_ref[...].astype(F32)[HALO - 8:] * prev_ok[d]
        after = n_ref[...].astype(F32)[:8] * next_ok[d]
        before2 = pltpu.roll(before, 1, 0)
        for j in range(RNN_HEADS):
            lanes = slice(j * RNN_HD, (j + 1) * RNN_HD)
            u = useg[d][:, lanes]
            ext = jnp.concatenate([
                _shift_rows(u[rows - 16:rows - 8], before2[:, lanes], up=False),
                _shift_rows(u[rows - 8:], before[:, lanes], up=False),
                u,
                _shift_rows(u[:8], after[:, lanes], up=True)], axis=0)
            conv = cb_ref[:, lanes]
            for tap in range(CONV_W):
                conv = conv + ext[8 * tap:8 * tap + rows] * cw_ref[tap:tap + 1, lanes]
            convs[d, j] = conv
            lhs = jnp.concatenate([conv.astype(BF16), bias_taps], axis=1)
            pres[d, j] = _dot(lhs, wg_ref[d, j])

    group = RNN_GROUP
    for j0 in range(0, RNN_HEADS, group):
        for d in range(2):
            outs = []
            for j in range(j0, j0 + group):
                pre, conv = pres[d, j], convs[d, j]
                u = jnp.tanh(pre[:, :RNN_HD]) + 1.0
                i = jnp.tanh(pre[:, RNN_HD:]) + 1.0
                rate = (0.5 * LRU_C) * _softplus(-lam_ref[d, j])
                a = jnp.exp2((-LOG2_E * rate) * u)
                y = 1.0 - a * a
                x = jnp.where(y > 0.0, y * lax.rsqrt(y), 0.0) * i * conv
                hs, carry = _segment_scan(a, x, carry_ref[d, j], reverse=(d == 1))
                carry_ref[d, j] = carry
                outs.append(hs.astype(BF16))
            h = _dot(unperm_ref[...], jnp.concatenate(outs, axis=1))
            refs[d][3][:, j0 * RNN_HD:(j0 + group) * RNN_HD] = h.astype(BF16)


def _rnn_core(u, conv_w, conv_b, wg, lam, n_ctx_rows):
    bsz, tt, rw = u.shape
    tile = RNN_TILE
    n_all, n_ctx = tt // tile, n_ctx_rows // tile
    per = tile // HALO
    n_halo = tt // HALO
    fwd = lambda n: n
    bwd = lambda n: _bwd_order(n, n_ctx, n_all)

    def specs(order):
        return [
            pl.BlockSpec((None, HALO, rw), lambda b, n: (b, jnp.maximum(order(n) * per - 1, 0), 0)),
            pl.BlockSpec((None, tile, rw), lambda b, n: (b, order(n), 0)),
            pl.BlockSpec((None, HALO, rw), lambda b, n: (b, jnp.minimum((order(n) + 1) * per, n_halo - 1), 0)),
        ]

    out_spec = lambda order: pl.BlockSpec((None, tile, rw), lambda b, n: (b, order(n), 0))
    perm, unperm = _segment_permutation(tile)
    consts = (conv_w, conv_b, wg, lam, perm, unperm)
    return pl.pallas_call(
        functools.partial(_rnn_core_kernel, n_ctx=n_ctx, n_all=n_all),
        out_shape=[jax.ShapeDtypeStruct((bsz, tt, rw), BF16)] * 2,
        grid=(bsz, n_all),
        in_specs=specs(fwd) + specs(bwd) + [_full_spec(c.shape) for c in consts],
        out_specs=[out_spec(fwd), out_spec(bwd)],
        scratch_shapes=[pltpu.VMEM((2, RNN_HEADS, 8, RNN_HD), F32)],
        compiler_params=_cparams("arbitrary", "arbitrary"),
        name="rglru_core",
    )(u, u, u, u, u, u, *consts)


def _residual(u, x, mod_ref, gp_ref, w_ref):
    y = _dot(u.astype(BF16), w_ref[...])
    return x + _rms(y) * (mod_ref[2:3] * gp_ref[...])


def _gla_post_part(x, mod_ref, gp_ref, w_ref, of_ref, ob_ref, z_ref, gh_ref, rows=slice(None)):
    o = of_ref[rows].astype(F32) + ob_ref[rows].astype(F32)
    hv = o.shape[-1] // GLA_HEADS
    o = jnp.concatenate([_rms(o[:, h * hv:(h + 1) * hv]) for h in range(GLA_HEADS)], axis=1)
    u = o * gh_ref[...] * _silu(z_ref[rows].astype(F32))
    return _residual(u, x, mod_ref, gp_ref, w_ref)


def _swa_post_part(x, mod_ref, gp_ref, w_ref, a_ref, z_ref, rows=slice(None)):
    u = a_ref[rows].astype(F32) * _silu(z_ref[rows].astype(F32))
    return _residual(u, x, mod_ref, gp_ref, w_ref)


def _rnn_post_part(x, mod_ref, gp_ref, w_ref, hf_ref, hb_ref, z_ref, rows=slice(None)):
    u = (hf_ref[rows].astype(F32) + hb_ref[rows].astype(F32)) * _silu(z_ref[rows].astype(F32))
    return _residual(u, x, mod_ref, gp_ref, w_ref)


class _Pre(NamedTuple):
    name: str
    part: Callable
    mod: jax.Array
    g_pre: jax.Array
    consts: tuple
    row_consts: tuple
    widths: tuple
    dtypes: tuple
    stats: int


class _Post(NamedTuple):
    name: str
    part: Callable
    mod: jax.Array
    g_post: jax.Array
    w_out: jax.Array
    acts: tuple
    vecs: tuple


def _pre_kernel(*refs, part, n_x, n_ctx):
    part(_token_tile(refs[:n_x], n_ctx), *refs[n_x:])


def _post_halves(part, x, refs):
    half = x.shape[0] // 2
    return jnp.concatenate([part(x[rows], *refs, rows=rows)
                            for rows in (slice(0, half), slice(half, 2 * half))], axis=0)


def _post_kernel(*refs, part, n_x, n_ctx):
    refs[-1][...] = _post_halves(part, _token_tile(refs[:n_x], n_ctx), refs[n_x:-1])


def _mid_kernel(*refs, post_part, n_post, pre_part, n_pre, n_x, n_ctx):
    post_in, pre_in = refs[n_x:n_x + n_post], refs[n_x + n_post:n_x + n_post + n_pre]
    xo_ref, pre_out = refs[n_x + n_post + n_pre], refs[n_x + n_post + n_pre + 1:]
    x = _post_halves(post_part, _token_tile(refs[:n_x], n_ctx), post_in)
    xo_ref[...] = x
    pre_part(x, *pre_in, *pre_out)


def _stream_shape(xs):
    if isinstance(xs, tuple):
        bsz, n_ctx_rows, d = xs[0].shape
        return bsz, n_ctx_rows + xs[1].shape[1], d
    return xs.shape


def _pre_io(pre, bsz, tt, d, n_ctx):
    operands = (pre.mod, pre.g_pre, *pre.consts, *pre.row_consts)
    in_specs = ([_mod_spec(d, n_ctx, bsz), _full_spec(pre.g_pre.shape)]
                + [_full_spec(c.shape) for c in pre.consts]
                + [pl.BlockSpec((ROW_TILE, r.shape[-1]), lambda b, t: (t, 0)) for r in pre.row_consts])
    out_shape = ([jax.ShapeDtypeStruct((bsz, tt, w), dt) for w, dt in zip(pre.widths, pre.dtypes)]
                 + [jax.ShapeDtypeStruct((bsz, tt // ROW_TILE, 8, 128), F32)] * pre.stats)
    out_specs = ([_row_spec(w) for w in pre.widths]
                 + [pl.BlockSpec((None, None, 8, 128), lambda b, t: (b, t, 0, 0))] * pre.stats)
    return operands, in_specs, out_shape, out_specs


def _post_io(post, bsz, d, n_ctx, off):
    operands = (post.mod, post.g_post, post.w_out, *post.acts, *post.vecs)
    in_specs = ([_mod_spec(d, n_ctx, bsz, off), _full_spec(post.g_post.shape), _full_spec(post.w_out.shape)]
                + [_row_spec(a.shape[-1], off) for a in post.acts]
                + [_full_spec(v.shape) for v in post.vecs])
    return operands, in_specs


def _pre_call(pre, xs, n_ctx_rows):
    bsz, tt, d = _stream_shape(xs)
    n_ctx = n_ctx_rows // ROW_TILE
    streams = xs if isinstance(xs, tuple) else (xs,)
    operands, in_specs, out_shape, out_specs = _pre_io(pre, bsz, tt, d, n_ctx)
    return pl.pallas_call(
        functools.partial(_pre_kernel, part=pre.part, n_x=len(streams), n_ctx=n_ctx),
        out_shape=out_shape,
        grid=(bsz, tt // ROW_TILE),
        in_specs=_token_specs(xs, n_ctx) + in_specs,
        out_specs=out_specs,
        compiler_params=_cparams("parallel", "parallel"),
        name=pre.name,
    )(*streams, *operands)


def _post_call(post, xs, n_ctx_rows, latent_only):
    bsz, tt, d = _stream_shape(xs)
    n_ctx = n_ctx_rows // ROW_TILE
    off = n_ctx if latent_only else 0
    n_tiles = tt // ROW_TILE - off
    streams = xs if isinstance(xs, tuple) else (xs,)
    operands, in_specs = _post_io(post, bsz, d, n_ctx, off)
    return pl.pallas_call(
        functools.partial(_post_kernel, part=post.part, n_x=len(streams), n_ctx=n_ctx - off),
        out_shape=jax.ShapeDtypeStruct((bsz, n_tiles * ROW_TILE, d), F32),
        grid=(bsz, n_tiles),
        in_specs=_token_specs(xs, n_ctx, off) + in_specs,
        out_specs=_row_spec(d),
        compiler_params=_cparams("parallel", "parallel"),
        name=post.name,
    )(*streams, *operands)


def _final_call(post, xs, n_ctx_rows):
    bsz, tt, d = xs.shape
    rows = lambda w: pl.BlockSpec((pl.Element(1), pl.Element(FINAL_TILE), pl.Element(w)),
                                  lambda b, t: (b, pl.multiple_of(n_ctx_rows + t * FINAL_TILE, 8), 0))
    operands = (post.mod, post.g_post, post.w_out, *post.acts, *post.vecs)
    in_specs = ([rows(d), pl.BlockSpec((None, 3, d), lambda b, t: (b, 0, 0)),
                 _full_spec(post.g_post.shape), _full_spec(post.w_out.shape)]
                + [rows(a.shape[-1]) for a in post.acts]
                + [_full_spec(v.shape) for v in post.vecs])

    def body(x_ref, mod_ref, gp_ref, w_ref, *rest):
        acts = [r.at[0] for r in rest[:len(post.acts)]]
        vecs, o_ref = rest[len(post.acts):-1], rest[-1]
        o_ref[...] = _post_halves(post.part, x_ref[0], (mod_ref, gp_ref, w_ref, *acts, *vecs))

    return pl.pallas_call(
        body,
        out_shape=jax.ShapeDtypeStruct((bsz, tt - n_ctx_rows, d), F32),
        grid=(bsz, (tt - n_ctx_rows) // FINAL_TILE),
        in_specs=in_specs,
        out_specs=pl.BlockSpec((None, FINAL_TILE, d), lambda b, t: (b, t, 0)),
        compiler_params=_cparams("parallel", "parallel"),
        name=post.name,
    )(xs, *operands)


def _mid_call(post, pre, xs, n_ctx_rows):
    bsz, tt, d = _stream_shape(xs)
    n_ctx = n_ctx_rows // ROW_TILE
    streams = xs if isinstance(xs, tuple) else (xs,)
    post_ops, post_specs = _post_io(post, bsz, d, n_ctx, 0)
    pre_ops, pre_specs, pre_shape, pre_out_specs = _pre_io(pre, bsz, tt, d, n_ctx)
    outs = pl.pallas_call(
        functools.partial(_mid_kernel, post_part=post.part, n_post=len(post_ops),
                          pre_part=pre.part, n_pre=len(pre_ops), n_x=len(streams), n_ctx=n_ctx),
        out_shape=[jax.ShapeDtypeStruct((bsz, tt, d), F32)] + pre_shape,
        grid=(bsz, tt // ROW_TILE),
        in_specs=_token_specs(xs, n_ctx) + post_specs + pre_specs,
        out_specs=[_row_spec(d)] + pre_out_specs,
        compiler_params=_cparams("parallel", "parallel"),
        name=post.name + "_" + pre.name,
    )(*streams, *post_ops, *pre_ops)
    return outs[0], outs[1:]


def _gla_layer(mod, g_pre, g_post, w_in, w_g1, w_g2, b_g, g_head, w_out, n_ctx_rows):
    d = w_in.shape[0]
    rank, dk = w_g2.shape[1], w_g2.shape[2]
    dv = w_out.shape[0]
    wg1 = jnp.zeros((d, GLA_GATE_PAD), F32).at[:, :2 * rank].set(jnp.concatenate([w_g1[0], w_g1[1]], axis=1))
    wg2 = jnp.zeros((GLA_GATE_PAD, 2 * dk), F32)
    wg2 = wg2.at[:rank, :dk].set(w_g2[0]).at[rank:2 * rank, dk:].set(w_g2[1])
    pre = _Pre("gla_pre", functools.partial(_gla_pre_part, dk=dk), mod, g_pre,
               (w_in.astype(BF16), wg1.astype(BF16), wg2.astype(BF16), b_g.reshape(1, 2 * dk)), (),
               (2 * dk, dv, dv, 2 * dk), (BF16, BF16, BF16, BF16), 1)

    def mixer(outs):
        qk, v, z, la, peak = outs
        exact = (peak[:, :, 0, 0] * (GLA_BLOCK // 2) > GLA_SAFE_EXPONENT).astype(jnp.int32)
        o_f, o_b = _gla_core(qk, v, la, exact, n_ctx_rows)
        return _Post("gla_post", _gla_post_part, mod, g_post, w_out.astype(BF16), (o_f, o_b, z),
                     (g_head.reshape(1, dv),))

    return pre, mixer


def _swa_layer(mod, g_pre, g_post, w_in, sink, w_out, rope, n_ctx_rows):
    qw = w_out.shape[0]
    kw = (w_in.shape[1] - 2 * qw) // 2
    pre = _Pre("swa_pre", _swa_pre_part, mod, g_pre, (w_in.astype(BF16),), rope,
               (qw, 2 * kw, qw), (BF16, BF16, BF16), 0)

    def mixer(outs):
        q, kv, z = outs
        a = _swa_core(q, kv, sink, n_ctx_rows)
        return _Post("swa_post", _swa_post_part, mod, g_post, w_out.astype(BF16), (a, z), ())

    return pre, mixer


def _rnn_layer(mod, g_pre, g_post, w_in, conv_w, conv_b, w_ra, b_ra, w_ri, b_ri, lam, w_out, n_ctx_rows):
    rw = w_out.shape[0]
    pre = _Pre("rglru_pre", _rnn_pre_part, mod, g_pre, (w_in.astype(BF16),), (), (rw, rw), (BF16, BF16), 0)
    vec = lambda a: a.reshape(2, RNN_HEADS, 1, RNN_HD)
    bias = 0.5 * jnp.concatenate([vec(b_ra), vec(b_ri)], axis=-1)
    bias_hi = bias.astype(BF16)
    bias_lo = (bias - bias_hi.astype(F32)).astype(BF16)
    wg = jnp.concatenate([jnp.concatenate([w_ra, w_ri], axis=-1).astype(BF16), bias_hi, bias_lo,
                          jnp.zeros((2, RNN_HEADS, RNN_HD - 2, 2 * RNN_HD), BF16)], axis=2)

    def mixer(outs):
        u, z = outs
        h_f, h_b = _rnn_core(u, 0.5 * conv_w, 0.5 * conv_b.reshape(1, rw), wg, vec(lam), n_ctx_rows)
        return _Post("rglru_post", _rnn_post_part, mod, g_post, w_out.astype(BF16), (h_f, h_b, z), ())

    return pre, mixer


def _rope_tables(t, n_ctx_rows):
    pairs = ATT_HEAD_DIM // 4
    pos = jnp.arange(t, dtype=jnp.int32)
    row = (pos // GRID_W).astype(F32)
    col = (pos % GRID_W).astype(F32)
    freqs = ROPE_BASE ** (-jnp.arange(pairs, dtype=F32) / pairs)
    ang_r, ang_c = row[:, None] * freqs, col[:, None] * freqs
    cos = jnp.concatenate([jnp.cos(ang_r)] * 2 + [jnp.cos(ang_c)] * 2, axis=1)
    sin = jnp.concatenate([-jnp.sin(ang_r), jnp.sin(ang_r), -jnp.sin(ang_c), jnp.sin(ang_c)], axis=1)
    cos = jnp.concatenate([jnp.ones((n_ctx_rows, ATT_HEAD_DIM), F32), cos], axis=0)
    sin = jnp.concatenate([jnp.zeros((n_ctx_rows, ATT_HEAD_DIM), F32), sin], axis=0)
    return jnp.tile(cos, (1, 2)), jnp.tile(sin, (1, 2))


def kernel(x, c, ctx, c_ctx, w_mod, b_mod, g_pre, g_post, a_w_in, a_w_g1, a_w_g2, a_b_g, a_g_head, a_w_out, b_w_in, b_sink, b_w_out, c_w_in, c_conv_w, c_conv_b, c_w_ra, c_b_ra, c_w_ri, c_b_ri, c_lam, c_w_out):
    bsz, t, d = x.shape
    n_ctx_rows = ctx.shape[1]
    depth = w_mod.shape[0]
    assert bsz < MOD_ROWS and n_ctx_rows % ROW_TILE == 0 and t % ROW_TILE == 0

    cvec = jnp.zeros((MOD_ROWS, d), F32).at[:bsz].set(c).at[bsz].set(c_ctx)
    mods = _modulation(cvec, w_mod, b_mod)
    rope = _rope_tables(t, n_ctx_rows)
    layers = []
    for i in range(depth):
        kind, j = i % N_MIXERS, i // N_MIXERS
        gpre, gpost = g_pre[i].reshape(1, d), g_post[i].reshape(1, d)
        if kind == 0:
            layers.append(_gla_layer(mods[i], gpre, gpost, a_w_in[j], a_w_g1[j], a_w_g2[j], a_b_g[j],
                                     a_g_head[j], a_w_out[j], n_ctx_rows))
        elif kind == 1:
            layers.append(_swa_layer(mods[i], gpre, gpost, b_w_in[j], b_sink[j], b_w_out[j], rope, n_ctx_rows))
        else:
            layers.append(_rnn_layer(mods[i], gpre, gpost, c_w_in[j], c_conv_w[j], c_conv_b[j], c_w_ra[j],
                                     c_b_ra[j], c_w_ri[j], c_b_ri[j], c_lam[j], c_w_out[j], n_ctx_rows))

    xs = (ctx, x)
    outs = _pre_call(layers[0][0], xs, n_ctx_rows)
    for i in range(depth - 1):
        xs, outs = _mid_call(layers[i][1](outs), layers[i + 1][0], xs, n_ctx_rows)
    if isinstance(xs, tuple) or t % FINAL_TILE:
        return _post_call(layers[-1][1](outs), xs, n_ctx_rows, latent_only=True)
    return _final_call(layers[-1][1](outs), xs, n_ctx_rows)
```
